```python
import math
import jax
import jax.numpy as jnp
from jax import lax
import numpy as np

D_MODEL = 1024
BATCH = 8
SEQ = 4096
DEPTH = 2

GRID_W = 64
CTX_LEN = 256
N_MIXERS = 2
RMS_EPS = 1e-6
HG_HEADS = 8
HG_KDIM = D_MODEL // HG_HEADS
HG_VDIM = D_MODEL // HG_HEADS
HG_CHUNK = 64
N_HG = (DEPTH + N_MIXERS - 1) // N_MIXERS
S5_GROUP = 16
S5_GROUPS = D_MODEL // S5_GROUP
S5_STATE = 64
S5_EIG_MAX = -1e-4
S5_DT_MIN = 1e-3
S5_DT_MAX = 1e-1
N_S5 = DEPTH // N_MIXERS
N_EXPERTS = 32
TOP_K = 4
D_FF = D_MODEL
SWIGLU_ALPHA = 1.702
SWIGLU_LIMIT = 7.0
MOE_BLOCK = 128

kernel_name = "hybrid_hgrn2_s5_moe_prefix_dit"

F32 = jnp.float32


def rmsnorm(x, g):
    xf = x.astype(F32)
    y = xf * lax.rsqrt(jnp.mean(xf * xf, axis=-1, keepdims=True) + RMS_EPS)
    return (y * g.astype(F32)).astype(x.dtype)


def modulate(h, shift, scale):
    return h * (1.0 + scale) + shift


def _ident(t):
    return t


def _flip_seq(t):
    return jnp.flip(t, axis=1)


def _flip0(t):
    return jnp.flip(t, axis=0)


def gla_chunked(q, k, v, log_f, s0):
    bsz, t_len, n_h, _ = q.shape
    vd = v.shape[-1]
    n = t_len // HG_CHUNK

    def chunks(t):
        return t.astype(F32).reshape(bsz, n, HG_CHUNK, n_h, t.shape[-1])

    q, k, v, log_f = chunks(q), chunks(k), chunks(v), chunks(log_f)
    b = jnp.cumsum(log_f, axis=2)
    b_ref = b[:, :, HG_CHUNK // 2 - 1:HG_CHUNK // 2]
    scores = jnp.einsum('bnthk,bnshk->bnhts', q * jnp.exp(b - b_ref), k * jnp.exp(b_ref - b))
    prefix = jnp.tril(jnp.ones((HG_CHUNK, HG_CHUNK), bool))
    o_intra = jnp.einsum('bnhts,bnshv->bnthv', jnp.where(prefix, scores, 0.0), v)
    b_last = b[:, :, -1]
    q_in = q * jnp.exp(b)
    k_out = k * jnp.exp(b_last[:, :, None] - b)

    def step(s, xs):
        qc, kc, vc, dc = xs
        o = jnp.einsum('bthk,bhkv->bthv', qc, s)
        s = s * jnp.exp(dc)[..., None] + jnp.einsum('bthk,bthv->bhkv', kc, vc)
        return s, o

    xs = tuple(jnp.moveaxis(t, 1, 0) for t in (q_in, k_out, v, b_last))
    s_fin, o_inter = lax.scan(step, s0.astype(F32), xs)
    o = o_intra + jnp.moveaxis(o_inter, 0, 1)
    return o.reshape(bsz, t_len, n_h, vd), s_fin


def hgrn2_mixer(a_ctx, a_lat, w_in, lb, norm_g, w_out, need_ctx):
    bsz = a_lat.shape[0]

    def heads(t):
        return t.reshape(t.shape[:-1] + (HG_HEADS, -1))

    def project(a):
        q, i, z_fwd, z_bwd, g = jnp.split(a @ w_in, 5, axis=-1)
        gates = []
        for d, z in enumerate((z_fwd, z_bwd)):
            f = lb[d] + (1.0 - lb[d]) * jax.nn.sigmoid(z.astype(F32))
            gates.append((heads(1.0 - f), heads(jnp.log(f))))
        return heads(jax.nn.silu(q)), heads(i), gates, g

    qc, vc, gates_c, gc = project(a_ctx)
    ql, vl, gates_l, gl = project(a_lat)
    s0 = jnp.zeros((bsz, HG_HEADS, HG_KDIM, HG_VDIM), F32)
    o_ctx = 0.0
    o_lat = 0.0
    for d, rev in enumerate((_ident, _flip_seq)):
        (kc, lfc), (kl, lfl) = gates_c[d], gates_l[d]
        oc, s_ctx = gla_chunked(rev(qc), rev(kc), rev(vc), rev(lfc), s0)
        ol, _ = gla_chunked(rev(ql), rev(kl), rev(vl), rev(lfl), s_ctx)
        o_lat = o_lat + rev(ol)
        if need_ctx:
            o_ctx = o_ctx + rev(oc)

    def readout(o, g):
        o = o * lax.rsqrt(jnp.mean(o * o, axis=-1, keepdims=True) + RMS_EPS)
        o = o * norm_g.astype(F32).reshape(HG_HEADS, HG_VDIM)
        o = o.reshape(o.shape[:2] + (D_MODEL,)).astype(g.dtype) * jax.nn.silu(g)
        return o @ w_out

    y_ctx = readout(o_ctx, gc) if need_ctx else None
    return y_ctx, readout(o_lat, gl)


def s5_discretise(a_re, a_im, log_dt, b_re, b_im):
    lam_re = jnp.minimum(a_re.astype(F32), S5_EIG_MAX)
    lam_im = a_im.astype(F32)
    dt = jnp.exp(log_dt.astype(F32))[:, None]
    mag = jnp.exp(lam_re * dt)
    ab_re = mag * jnp.cos(lam_im * dt)
    ab_im = mag * jnp.sin(lam_im * dt)
    den = lam_re * lam_re + lam_im * lam_im
    coef_re = ((ab_re - 1.0) * lam_re + ab_im * lam_im) / den
    coef_im = (ab_im * lam_re - (ab_re - 1.0) * lam_im) / den
    b_re = b_re.astype(F32)
    b_im = b_im.astype(F32)
    bb_re = coef_re[..., None] * b_re - coef_im[..., None] * b_im
    bb_im = coef_re[..., None] * b_im + coef_im[..., None] * b_re
    return ab_re, ab_im, bb_re, bb_im


def _cmul(ar, ai, br, bi):
    return ar * br - ai * bi, ar * bi + ai * br


def _lin_rec_op(e1, e2):
    a1r, a1i, b1r, b1i = e1
    a2r, a2i, b2r, b2i = e2
    ar, ai = _cmul(a2r, a2i, a1r, a1i)
    br, bi = _cmul(a2r, a2i, b1r, b1i)
    return ar, ai, br + b2r, bi + b2i


def complex_diag_scan(ab_re, ab_im, bu_re, bu_im, s0_re, s0_im):
    i_re, i_im = _cmul(ab_re, ab_im, s0_re, s0_im)
    bu_re = bu_re.at[0].add(i_re)
    bu_im = bu_im.at[0].add(i_im)
    a_re = jnp.broadcast_to(ab_re, bu_re.shape)
    a_im = jnp.broadcast_to(ab_im, bu_im.shape)
    _, _, s_re, s_im = lax.associative_scan(_lin_rec_op, (a_re, a_im, bu_re, bu_im), axis=0)
    return s_re, s_im


def s5_mixer(a_ctx, a_lat, a_re, a_im, log_dt, b_re, b_im, c_re, c_im, d_skip, w_glu, b_glu, need_ctx):
    bsz, t_len, _ = a_lat.shape
    rows = t_len // GRID_W
    to_cols = lambda t: t.reshape(bsz, rows, GRID_W, D_MODEL).transpose(0, 2, 1, 3).reshape(bsz, t_len, D_MODEL)
    to_rows = lambda t: t.reshape(bsz, GRID_W, rows, D_MODEL).transpose(0, 2, 1, 3).reshape(bsz, t_len, D_MODEL)
    u_lat = to_cols(a_lat)
    disc = [s5_discretise(a_re[d], a_im[d], log_dt[d], b_re, b_im) for d in range(2)]
    cr = c_re.astype(F32)
    ci = c_im.astype(F32)

    def per_sample(args):
        uc, ul = args
        ucg = uc.astype(F32).reshape(-1, S5_GROUPS, S5_GROUP)
        ulg = ul.astype(F32).reshape(-1, S5_GROUPS, S5_GROUP)
        zero = jnp.zeros((S5_GROUPS, S5_STATE), F32)
        yc = 0.0
        yl = 0.0
        for d, rev in enumerate((_ident, _flip0)):
            ab_re, ab_im, bb_re, bb_im = disc[d]
            drive = lambda u: (jnp.einsum('tgi,gpi->tgp', u, bb_re), jnp.einsum('tgi,gpi->tgp', u, bb_im))
            read = lambda s: jnp.einsum('tgp,gop->tgo', s[0], cr[d]) - jnp.einsum('tgp,gop->tgo', s[1], ci[d])
            sc = complex_diag_scan(ab_re, ab_im, *drive(rev(ucg)), zero, zero)
            sl = complex_diag_scan(ab_re, ab_im, *drive(rev(ulg)), sc[0][-1], sc[1][-1])
            yl = yl + rev(read(sl))
            if need_ctx:
                yc = yc + rev(read(sc))
        yl = yl.reshape(ul.shape)
        if need_ctx:
            return yc.reshape(uc.shape), yl
        return yl

    out = lax.map(per_sample, (a_ctx, u_lat))
    dsk = d_skip.astype(F32)

    def glu(y, u):
        y = (y + dsk * u.astype(F32)).astype(u.dtype)
        za, zb = jnp.split(jax.nn.gelu(y) @ w_glu + b_glu, 2, axis=-1)
        return za * jax.nn.sigmoid(zb)

    if need_ctx:
        y_ctx_raw, y_lat_raw = out
        return glu(y_ctx_raw, a_ctx), glu(to_rows(y_lat_raw), a_lat)
    return None, glu(to_rows(out), a_lat)


def moe(h, router_w, router_b, w1, b1, w2, b2):
    n_tok, d = h.shape
    logits = (h @ router_w + router_b).astype(F32)
    top_v, top_i = lax.top_k(logits, TOP_K)
    top_w = jax.nn.softmax(top_v, axis=-1)
    n_asg = n_tok * TOP_K
    flat_e = top_i.reshape(-1)
    flat_t = jnp.arange(n_asg, dtype=jnp.int32) // TOP_K
    flat_w = top_w.reshape(-1)
    order = jnp.argsort(flat_e)
    e_sorted = flat_e[order]
    counts = jnp.bincount(flat_e, length=N_EXPERTS)
    padded = (counts + MOE_BLOCK - 1) // MOE_BLOCK * MOE_BLOCK
    pad_end = jnp.cumsum(padded)
    pad_start = pad_end - padded
    start = jnp.cumsum(counts) - counts
    dest = pad_start[e_sorted] + jnp.arange(n_asg, dtype=jnp.int32) - start[e_sorted]
    n_blocks = -(-n_asg // MOE_BLOCK) + N_EXPERTS
    slot_tok = jnp.zeros((n_blocks * MOE_BLOCK,), jnp.int32).at[dest].set(flat_t[order])
    slot_w = jnp.zeros((n_blocks * MOE_BLOCK,), F32).at[dest].set(flat_w[order])
    block_pos = jnp.arange(n_blocks, dtype=jnp.int32) * MOE_BLOCK
    block_e = jnp.minimum(jnp.searchsorted(pad_end, block_pos, side='right'), N_EXPERTS - 1)

    def block_ffn(args):
        tok, e = args
        z = h[tok] @ w1[e] + b1[e]
        z_glu, z_lin = jnp.split(z, 2, axis=-1)
        z_glu = jnp.minimum(z_glu, SWIGLU_LIMIT)
        z_lin = jnp.clip(z_lin, -SWIGLU_LIMIT, SWIGLU_LIMIT)
        act = z_glu * jax.nn.sigmoid(SWIGLU_ALPHA * z_glu) * (z_lin + 1.0)
        return act @ w2[e] + b2[e]

    y = lax.map(block_ffn, (slot_tok.reshape(n_blocks, MOE_BLOCK), block_e))
    y = y.reshape(-1, d) * slot_w[:, None].astype(h.dtype)
    return jnp.zeros_like(h).at[slot_tok].add(y)


def setup_inputs(seed: int = 0) -> dict:
    key = jax.random.key(seed)
    keys = iter(jax.random.split(key, 40))

    def normal(shape, scale):
        return jax.random.normal(next(keys), shape, F32) * scale

    def gain(shape):
        return 1.0 + normal(shape, 0.02)

    d, g, p, gc = D_MODEL, S5_GROUPS, S5_STATE, S5_GROUP
    n_idx = jnp.arange(p, dtype=F32)
    return {
        "x": normal((BATCH, SEQ, d), 1.0),
        "c": normal((BATCH, d), 1.0),
        "ctx": normal((BATCH, CTX_LEN, d), 1.0),
        "c_ctx": normal((d,), 1.0),
        "mod_w": normal((DEPTH, d, 6 * d), d ** -0.5),
        "mod_b": normal((DEPTH, 6 * d), 0.02),
        "norm1_g": gain((DEPTH, d)),
        "norm2_g": gain((DEPTH, d)),
        "hg_w_in": normal((N_HG, d, 5 * d), d ** -0.5),
        "hg_lb_logits": normal((N_HG + 1, 2, d), 0.1),
        "hg_norm_g": gain((N_HG, d)),
        "hg_w_out": normal((N_HG, d, d), d ** -0.5),
        "s5_a_re": -0.5 + normal((N_S5, 2, g, p), 0.01),
        "s5_a_im": math.pi * n_idx + normal((N_S5, 2, g, p), 0.01),
        "s5_log_dt": jax.random.uniform(next(keys), (N_S5, 2, g), F32, math.log(S5_DT_MIN), math.log(S5_DT_MAX)),
        "s5_b_re": normal((N_S5, g, p, gc), (2 * gc) ** -0.5),
        "s5_b_im": normal((N_S5, g, p, gc), (2 * gc) ** -0.5),
        "s5_c_re": normal((N_S5, 2, g, gc, p), p ** -0.5),
        "s5_c_im": normal((N_S5, 2, g, gc, p), p ** -0.5),
        "s5_d": normal((N_S5, d), 1.0),
        "s5_w_glu": normal((N_S5, d, 2 * d), d ** -0.5),
        "s5_b_glu": normal((N_S5, 2 * d), 0.02),
        "router_w": normal((DEPTH, d, N_EXPERTS), d ** -0.5),
        "router_b": normal((DEPTH, N_EXPERTS), 0.01),
        "moe_w1": normal((DEPTH, N_EXPERTS, d, 2 * D_FF), d ** -0.5),
        "moe_b1": normal((DEPTH, N_EXPERTS, 2 * D_FF), 0.02),
        "moe_w2": normal((DEPTH, N_EXPERTS, D_FF, d), D_FF ** -0.5),
        "moe_b2": normal((DEPTH, N_EXPERTS, d), 0.02),
        "final_g": gain((d,)),
    }


def reference(x, c, ctx, c_ctx, mod_w, mod_b, norm1_g, norm2_g, hg_w_in, hg_lb_logits, hg_norm_g,
              hg_w_out, s5_a_re, s5_a_im, s5_log_dt, s5_b_re, s5_b_im, s5_c_re, s5_c_im, s5_d, s5_w_glu,
              s5_b_glu, router_w, router_b, moe_w1, moe_b1, moe_w2, moe_b2, final_g):
    d = x.shape[-1]
    lb_all = jnp.cumsum(jax.nn.softmax(hg_lb_logits.astype(F32), axis=0), axis=0)
    h_lat, h_ctx = x, ctx
    for i in range(DEPTH):
        last = i == DEPTH - 1
        j = i // N_MIXERS
        m_lat = (jax.nn.silu(c) @ mod_w[i] + mod_b[i])[:, None, :]
        m_ctx = jax.nn.silu(c_ctx) @ mod_w[i] + mod_b[i]
        sh1, sc1, g1, sh2, sc2, g2 = jnp.split(m_lat, 6, axis=-1)
        csh1, csc1, cg1, csh2, csc2, cg2 = jnp.split(m_ctx, 6, axis=-1)
        a_lat = modulate(rmsnorm(h_lat, norm1_g[i]), sh1, sc1)
        a_ctx = modulate(rmsnorm(h_ctx, norm1_g[i]), csh1, csc1)
        if i % N_MIXERS == 0:
            y_ctx, y_lat = hgrn2_mixer(a_ctx, a_lat, hg_w_in[j], lb_all[j], hg_norm_g[j], hg_w_out[j], not last)
        else:
            y_ctx, y_lat = s5_mixer(a_ctx, a_lat, s5_a_re[j], s5_a_im[j], s5_log_dt[j], s5_b_re[j], s5_b_im[j],
                                    s5_c_re[j], s5_c_im[j], s5_d[j], s5_w_glu[j], s5_b_glu[j], not last)
        h_lat = h_lat + g1 * y_lat
        f_lat = modulate(rmsnorm(h_lat, norm2_g[i]), sh2, sc2)
        if last:
            y = moe(f_lat.reshape(-1, d), router_w[i], router_b[i], moe_w1[i], moe_b1[i], moe_w2[i], moe_b2[i])
            h_lat = h_lat + g2 * y.reshape(h_lat.shape)
        else:
            h_ctx = h_ctx + cg1 * y_ctx
            f_ctx = modulate(rmsnorm(h_ctx, norm2_g[i]), csh2, csc2)
            tokens = jnp.concatenate([f_ctx.reshape(-1, d), f_lat.reshape(-1, d)], axis=0)
            y = moe(tokens, router_w[i], router_b[i], moe_w1[i], moe_b1[i], moe_w2[i], moe_b2[i])
            n_ctx = h_ctx.shape[0] * h_ctx.shape[1]
            h_ctx = h_ctx + cg2 * y[:n_ctx].reshape(h_ctx.shape)
            h_lat = h_lat + g2 * y[n_ctx:].reshape(h_lat.shape)
    return rmsnorm(h_lat, final_g)
```

```python
import functools

import jax
import jax.numpy as jnp
from jax import lax
from jax.experimental import pallas as pl
from jax.experimental.pallas import tpu as pltpu

F32 = jnp.float32
BF16 = jnp.bfloat16
HIGHEST = lax.Precision.HIGHEST

RMS_EPS = 1e-6
GRID_W = 64
HEAD_DIM = 128
HG_CHUNK = 64
S5_GROUP = 16
S5_STATE = 64
S5_EIG_MAX = -1e-4
S5_CHUNK = 8
S5_COLS = 8
LANES = 128
N_EXPERTS = 32
TOP_K = 4
SWIGLU_ALPHA = 1.702
SWIGLU_LIMIT = 7.0
TOK_TILE = 256
MOE_ROWS = 256
VMEM_LIMIT = 56 * 1024 * 1024


def _params(*sem):
    return pltpu.CompilerParams(dimension_semantics=sem, vmem_limit_bytes=VMEM_LIMIT)


def _rms(x, g):
    return x * lax.rsqrt(jnp.mean(x * x, axis=-1, keepdims=True) + RMS_EPS) * g


def _sigmoid(x):
    return 1.0 / (1.0 + jnp.exp(-x))


def _silu(x):
    return x * _sigmoid(x)


def _mod_kernel(c_ref, w_ref, b_ref, o_ref):
    s = _silu(c_ref[...])
    o_ref[0] = jnp.dot(s, w_ref[0], precision=HIGHEST, preferred_element_type=F32) + b_ref[0]


def _modulation(c, c_ctx, mod_w, mod_b):
    bsz, d = c.shape
    depth = mod_w.shape[0]
    rows = jnp.concatenate([c, c_ctx[None], jnp.zeros((16 - bsz - 1, d), F32)], axis=0)
    bn = 6 * d // 4
    out = pl.pallas_call(
        _mod_kernel,
        grid=(depth, 4),
        in_specs=[pl.BlockSpec((16, d), lambda l, j: (0, 0)),
                  pl.BlockSpec((1, d, bn), lambda l, j: (l, 0, j)),
                  pl.BlockSpec((1, 1, bn), lambda l, j: (l, 0, j))],
        out_specs=pl.BlockSpec((1, 16, bn), lambda l, j: (l, 0, j)),
        out_shape=jax.ShapeDtypeStruct((depth, 16, 6 * d), F32),
        compiler_params=_params("parallel", "parallel"),
        name="modulation",
    )(rows, mod_w, mod_b.reshape(depth, 1, 6 * d))
    m_lat = out[:, :bsz].reshape(depth, bsz, 1, 6, d)
    m_ctx = jnp.broadcast_to(out[:, bsz].reshape(depth, 1, 1, 6, d), m_lat.shape)
    return jnp.concatenate([m_ctx, m_lat], axis=2).reshape(depth, 2 * bsz, 6, d)


def _hg_proj_kernel(h_ref, m_ref, g_ref, w_ref, lb_ref,
                    qs_ref, v_ref, sg_ref, kf_ref, kb_ref, lff_ref, lfb_ref):
    d = h_ref.shape[-1]
    m = m_ref[0]
    a = _rms(h_ref[0], g_ref[...]) * (1.0 + m[1:2]) + m[0:1]
    p = jnp.dot(a.astype(BF16), w_ref[...], preferred_element_type=F32)
    qs_ref[0] = _silu(p[:, 0:d]).astype(BF16)
    v_ref[0] = p[:, d:2 * d].astype(BF16)
    sg_ref[0] = _silu(p[:, 4 * d:5 * d]).astype(BF16)
    for di, (k_ref, lf_ref) in enumerate(((kf_ref, lff_ref), (kb_ref, lfb_ref))):
        lb = lb_ref[di:di + 1]
        f = lb + (1.0 - lb) * _sigmoid(p[:, (2 + di) * d:(3 + di) * d])
        k_ref[0] = (1.0 - f).astype(BF16)
        lf_ref[0] = jnp.log(f)


def _hg_proj(h, mod, g1, w_in, lb, n_ctx_tiles):
    bsz, t, d = h.shape
    tok = pl.BlockSpec((1, TOK_TILE, d), lambda b, i: (b, i, 0))
    bf = jax.ShapeDtypeStruct((bsz, t, d), BF16)
    ff = jax.ShapeDtypeStruct((bsz, t, d), F32)
    return pl.pallas_call(
        _hg_proj_kernel,
        grid=(bsz, t // TOK_TILE),
        in_specs=[tok,
                  pl.BlockSpec((1, 6, d), lambda b, i: (2 * b + (i >= n_ctx_tiles).astype(jnp.int32), 0, 0)),
                  pl.BlockSpec((1, d), lambda b, i: (0, 0)),
                  pl.BlockSpec((d, 5 * d), lambda b, i: (0, 0)),
                  pl.BlockSpec((2, d), lambda b, i: (0, 0))],
        out_specs=[tok] * 7,
        out_shape=[bf, bf, bf, bf, bf, ff, ff],
        compiler_params=_params("parallel", "parallel"),
        name="hg_proj",
    )(h, mod, g1.reshape(1, d), w_in.astype(BF16), lb)


def _split3(x):
    hi = x.astype(BF16)
    r = x - hi.astype(F32)
    mid = r.astype(BF16)
    lo = (r - mid.astype(F32)).astype(BF16)
    return hi, mid, lo


def _gla_kernel(qf_ref, vf_ref, kf_ref, lf_ref, qb_ref, vb_ref, kb_ref, lb_ref,
                of_ref, ob_ref, sf_ref, sb_ref):
    c = HG_CHUNK
    n_heads = sf_ref.shape[0]

    @pl.when(pl.program_id(1) == 0)
    def _():
        sf_ref[...] = jnp.zeros_like(sf_ref)
        sb_ref[...] = jnp.zeros_like(sb_ref)

    row = lax.broadcasted_iota(jnp.int32, (c, c), 0)
    col = lax.broadcasted_iota(jnp.int32, (c, c), 1)
    dirs = ((qf_ref, vf_ref, kf_ref, lf_ref, of_ref, sf_ref, col <= row, c // 2 - 1, c - 1),
            (qb_ref, vb_ref, kb_ref, lb_ref, ob_ref, sb_ref, col >= row, c // 2, 0))
    for q_ref, v_ref, k_ref, l_ref, o_ref, s_ref, keep, r_ref, r_last in dirs:
        tri = keep.astype(BF16)
        b = sum(jnp.dot(tri, part, preferred_element_type=F32) for part in _split3(l_ref[0]))
        b_ref = b[r_ref:r_ref + 1]
        b_last = b[r_last:r_last + 1]
        a_in = q_ref[0].astype(F32) * jnp.exp(b - b_ref)
        k_in = k_ref[0].astype(F32) * jnp.exp(b_ref - b)
        q_st = (a_in * jnp.exp(b_ref)).astype(BF16)
        k_st = (k_in * jnp.exp(b_last - b_ref)).astype(BF16)
        a_in = a_in.astype(BF16)
        k_in = k_in.astype(BF16)
        decay = jnp.exp(b_last)
        v = v_ref[0]
        for h in range(n_heads):
            sl = slice(h * HEAD_DIM, (h + 1) * HEAD_DIM)
            sc = lax.dot_general(a_in[:, sl], k_in[:, sl], (((1,), (1,)), ((), ())),
                                 preferred_element_type=F32)
            sc = jnp.where(keep, sc, 0.0).astype(BF16)
            o = jnp.dot(sc, v[:, sl], preferred_element_type=F32)
            st = s_ref[h]
            o = o + lax.dot_general(q_st[:, sl], st.astype(BF16), (((1,), (1,)), ((), ())),
                                    preferred_element_type=F32)
            s_ref[h] = st * decay[:, sl] + lax.dot_general(
                v[:, sl], k_st[:, sl], (((0,), (0,)), ((), ())), preferred_element_type=F32)
            o_ref[0, :, sl] = o


def _gla(qs, v, kf, kb, lff, lfb, n_ctx_chunks):
    bsz, t, d = qs.shape
    n = t // HG_CHUNK
    n_heads = d // HEAD_DIM

    def rev(j):
        return jnp.where(j < n_ctx_chunks, n_ctx_chunks - 1 - j, n + n_ctx_chunks - 1 - j)

    fwd = pl.BlockSpec((1, HG_CHUNK, d), lambda b, j: (b, j, 0))
    bwd = pl.BlockSpec((1, HG_CHUNK, d), lambda b, j: (b, rev(j), 0))
    out = jax.ShapeDtypeStruct((bsz, t, d), F32)
    state = pltpu.VMEM((n_heads, HEAD_DIM, HEAD_DIM), F32)
    return pl.pallas_call(
        _gla_kernel,
        grid=(bsz, n),
        in_specs=[fwd, fwd, fwd, fwd, bwd, bwd, bwd, bwd],
        out_specs=[fwd, bwd],
        out_shape=[out, out],
        scratch_shapes=[state, state],
        compiler_params=_params("parallel", "arbitrary"),
        name="gla",
    )(qs, v, kf, lff, qs, v, kb, lfb)


def _mixer_tail(h, y, m, g2, rw, rb, h1_ref, f_ref, ti_ref, tw_ref):
    h1 = h + m[2:3] * y
    f = _rms(h1, g2) * (1.0 + m[4:5]) + m[3:4]
    h1_ref[...] = h1.reshape(h1_ref.shape)
    f_ref[...] = f.reshape(f_ref.shape)
    logits = jnp.dot(f, rw, precision=HIGHEST, preferred_element_type=F32) + rb
    n_e = logits.shape[-1]
    lane = lax.broadcasted_iota(jnp.int32, logits.shape, 1).astype(F32)
    vals, idxs = [], []
    for _ in range(TOP_K):
        mx = jnp.max(logits, axis=-1, keepdims=True)
        ix = jnp.min(jnp.where(logits == mx, lane, float(n_e)), axis=-1, keepdims=True)
        vals.append(mx)
        idxs.append(ix)
        logits = jnp.where(lane == ix, -jnp.inf, logits)
    es = [jnp.exp(x - vals[0]) for x in vals]
    tot = sum(es)
    for k in range(TOP_K):
        ti_ref[:, k:k + 1] = idxs[k].astype(jnp.int32)
        tw_ref[:, k:k + 1] = es[k] / tot


def _hg_readout_kernel(of_ref, ob_ref, sg_ref, h_ref, m_ref, ng_ref, w_ref, g2_ref, rw_ref, rb_ref,
                       h1_ref, f_ref, ti_ref, tw_ref):
    d = h_ref.shape[-1]
    o = of_ref[0] + ob_ref[0]
    parts = []
    for h in range(d // HEAD_DIM):
        oh = o[:, h * HEAD_DIM:(h + 1) * HEAD_DIM]
        parts.append(oh * lax.rsqrt(jnp.mean(oh * oh, axis=-1, keepdims=True) + RMS_EPS))
    o = jnp.concatenate(parts, axis=-1) * ng_ref[...]
    y = jnp.dot((o * sg_ref[0].astype(F32)).astype(BF16), w_ref[...], preferred_element_type=F32)
    _mixer_tail(h_ref[0], y, m_ref[0], g2_ref[...], rw_ref[...], rb_ref[...],
                h1_ref, f_ref, ti_ref.at[0], tw_ref.at[0])


def _hg_readout(o_f, o_b, sg, h, mod, norm_g, w_out, g2, rw, rb, n_ctx_tiles):
    bsz, t, d = h.shape
    n_e = rw.shape[-1]
    tok = pl.BlockSpec((1, TOK_TILE, d), lambda b, i: (b, i, 0))
    top = pl.BlockSpec((1, TOK_TILE, TOP_K), lambda b, i: (b, i, 0))
    const = lambda shape: pl.BlockSpec(shape, lambda b, i: (0,) * len(shape))
    return pl.pallas_call(
        _hg_readout_kernel,
        grid=(bsz, t // TOK_TILE),
        in_specs=[tok, tok, tok, tok,
                  pl.BlockSpec((1, 6, d), lambda b, i: (2 * b + (i >= n_ctx_tiles).astype(jnp.int32), 0, 0)),
                  const((1, d)), const((d, d)), const((1, d)), const((d, n_e)), const((1, n_e))],
        out_specs=[tok, tok, top, top],
        out_shape=[jax.ShapeDtypeStruct((bsz, t, d), F32), jax.ShapeDtypeStruct((bsz, t, d), F32),
                   jax.ShapeDtypeStruct((bsz, t, TOP_K), jnp.int32),
                   jax.ShapeDtypeStruct((bsz, t, TOP_K), F32)],
        compiler_params=_params("parallel", "parallel"),
        name="hg_readout",
    )(o_f, o_b, sg, h, mod, norm_g.reshape(1, d), w_out.astype(BF16), g2.reshape(1, d),
      rw, rb.reshape(1, n_e))


def _moe_plan(top_i, rows):
    n_asg = top_i.size
    flat_e = top_i.reshape(-1)
    order = jnp.argsort(flat_e)
    e_sorted = flat_e[order]
    counts = jnp.bincount(flat_e, length=N_EXPERTS)
    padded = (counts + rows - 1) // rows * rows
    pad_end = jnp.cumsum(padded)
    pad_start = pad_end - padded
    start = jnp.cumsum(counts) - counts
    dest = pad_start[e_sorted] + jnp.arange(n_asg, dtype=jnp.int32) - start[e_sorted]
    n_blocks = -(-n_asg // rows) + N_EXPERTS
    n_slots = n_blocks * rows
    order = order.astype(jnp.int32)
    slot_tok = jnp.zeros((n_slots,), jnp.int32).at[dest].set(order // TOP_K)
    slot_dst = jnp.full((n_slots,), -1, jnp.int32).at[dest].set(order)
    is_pad = slot_dst < 0
    spill = n_asg + jnp.cumsum(is_pad.astype(jnp.int32)) - 1
    slot_dst = jnp.where(is_pad, spill, slot_dst)
    block_pos = jnp.arange(n_blocks, dtype=jnp.int32) * rows
    block_e = jnp.minimum(jnp.searchsorted(pad_end, block_pos, side='right'), N_EXPERTS - 1)
    return (block_e.astype(jnp.int32), slot_tok.reshape(n_blocks, 1, rows),
            slot_dst.reshape(n_blocks, 1, rows))


def _ffn_kernel(be_ref, tok_ref, tok_next_ref, dst_ref, x_hbm, w1_ref, b1_ref, w2_ref, b2_ref,
                y_hbm, xbuf, ybuf, gsem, ssem):
    del be_ref
    rows = xbuf.shape[1]
    b = pl.program_id(0)
    nb = pl.num_programs(0)
    slot = b % 2

    def gather(idx_ref, s):
        def body(r, carry):
            pltpu.make_async_copy(x_hbm.at[pl.ds(idx_ref[0, 0, r], 1)], xbuf.at[s, pl.ds(r, 1)],
                                  gsem.at[s]).start()
            return carry
        lax.fori_loop(0, rows, body, 0, unroll=8)

    def wait_gather(s):
        pltpu.make_async_copy(x_hbm.at[pl.ds(0, rows)], xbuf.at[s], gsem.at[s]).wait()

    def wait_scatter(s):
        pltpu.make_async_copy(ybuf.at[s], y_hbm.at[pl.ds(0, rows)], ssem.at[s]).wait()

    @pl.when(b == 0)
    def _():
        gather(tok_ref, 0)

    @pl.when(b + 1 < nb)
    def _():
        gather(tok_next_ref, 1 - slot)

    wait_gather(slot)

    @pl.when(b >= 2)
    def _():
        wait_scatter(slot)

    f = w2_ref.shape[1]
    z = jnp.dot(xbuf[slot].astype(BF16), w1_ref[0], preferred_element_type=F32) + b1_ref[0]
    z_glu = jnp.minimum(z[:, :f], SWIGLU_LIMIT)
    z_lin = jnp.clip(z[:, f:], -SWIGLU_LIMIT, SWIGLU_LIMIT)
    act = z_glu * _sigmoid(SWIGLU_ALPHA * z_glu) * (z_lin + 1.0)
    ybuf[slot] = jnp.dot(act.astype(BF16), w2_ref[0], preferred_element_type=F32) + b2_ref[0]

    def scatter(r, carry):
        pltpu.make_async_copy(ybuf.at[slot, pl.ds(r, 1)], y_hbm.at[pl.ds(dst_ref[0, 0, r], 1)],
                              ssem.at[slot]).start()
        return carry
    lax.fori_loop(0, rows, scatter, 0, unroll=8)

    @pl.when(b == nb - 1)
    def _():
        wait_scatter(slot)

        @pl.when(nb >= 2)
        def _():
            wait_scatter(1 - slot)


def _moe_ffn(x, top_i, w1, b1, w2, b2):
    n_tok, d = x.shape
    n_e, _, f2 = w1.shape
    block_e, slot_tok, slot_dst = _moe_plan(top_i, MOE_ROWS)
    n_blocks = block_e.shape[0]
    n_slots = n_blocks * MOE_ROWS
    smem = lambda imap: pl.BlockSpec((1, 1, MOE_ROWS), imap, memory_space=pltpu.SMEM)
    grid_spec = pltpu.PrefetchScalarGridSpec(
        num_scalar_prefetch=1,
        grid=(n_blocks,),
        in_specs=[smem(lambda b, be: (b, 0, 0)),
                  smem(lambda b, be: (jnp.minimum(b + 1, n_blocks - 1), 0, 0)),
                  smem(lambda b, be: (b, 0, 0)),
                  pl.BlockSpec(memory_space=pl.ANY),
                  pl.BlockSpec((1, d, f2), lambda b, be: (be[b], 0, 0)),
                  pl.BlockSpec((1, 1, f2), lambda b, be: (be[b], 0, 0)),
                  pl.BlockSpec((1, f2 // 2, d), lambda b, be: (be[b], 0, 0)),
                  pl.BlockSpec((1, 1, d), lambda b, be: (be[b], 0, 0))],
        out_specs=pl.BlockSpec(memory_space=pl.ANY),
        scratch_shapes=[pltpu.VMEM((2, MOE_ROWS, d), F32), pltpu.VMEM((2, MOE_ROWS, d), F32),
                        pltpu.SemaphoreType.DMA((2,)), pltpu.SemaphoreType.DMA((2,))])
    y = pl.pallas_call(
        _ffn_kernel,
        grid_spec=grid_spec,
        out_shape=jax.ShapeDtypeStruct((n_slots, d), F32),
        compiler_params=_params("arbitrary"),
        name="moe_ffn",
    )(block_e, slot_tok, slot_tok, slot_dst, x, w1.astype(BF16), b1.reshape(n_e, 1, f2),
      w2.astype(BF16), b2.reshape(n_e, 1, d))
    return y.reshape(n_slots // TOP_K, TOP_K * d)


def _combine(y_ref, tw_ref):
    d = y_ref.shape[-1] // TOP_K
    tw = tw_ref[0]
    return sum(tw[:, k:k + 1] * y_ref[:, k * d:(k + 1) * d] for k in range(TOP_K))


def _combine0_kernel(y_ref, tw_ref, h_ref, m0_ref, m1_ref, g_ref, hl_ref, a_ref):
    h2 = h_ref[0] + m0_ref[0][5:6] * _combine(y_ref, tw_ref)
    hl_ref[0] = h2
    m1 = m1_ref[0]
    a_ref[0] = _rms(h2, g_ref[...]) * (1.0 + m1[1:2]) + m1[0:1]


def _combine0(y4, top_w, h1, mod0, mod1, g1_next, n_ctx_tiles):
    bsz, t, d = h1.shape
    nt = t // TOK_TILE
    tok = pl.BlockSpec((1, TOK_TILE, d), lambda b, i: (b, i, 0))
    mod = pl.BlockSpec((1, 6, d), lambda b, i: (2 * b + (i >= n_ctx_tiles).astype(jnp.int32), 0, 0))
    return pl.pallas_call(
        _combine0_kernel,
        grid=(bsz, nt),
        in_specs=[pl.BlockSpec((TOK_TILE, TOP_K * d), lambda b, i: (b * nt + i, 0)),
                  pl.BlockSpec((1, TOK_TILE, TOP_K), lambda b, i: (b, i, 0)),
                  tok, mod, mod, pl.BlockSpec((1, d), lambda b, i: (0, 0))],
        out_specs=[tok, tok],
        out_shape=[jax.ShapeDtypeStruct((bsz, t, d), F32)] * 2,
        compiler_params=_params("parallel", "parallel"),
        name="combine0",
    )(y4, top_w, h1, mod0, mod1, g1_next.reshape(1, d))


def _s5_matrices(a_re, a_im, log_dt, b_re, b_im, c_re, c_im):
    ng, p = a_re.shape[1:]
    gc = b_re.shape[-1]
    ll = S5_CHUNK
    gpb = LANES // gc
    nblk = ng // gpb
    lam_re = jnp.minimum(a_re, S5_EIG_MAX)
    lam_im = a_im
    dt = jnp.exp(log_dt)[..., None]
    j = jnp.arange(ll + 1, dtype=F32).reshape(-1, 1, 1, 1)
    mag = jnp.exp(j * (lam_re * dt))
    pw_re = mag * jnp.cos(j * (lam_im * dt))
    pw_im = mag * jnp.sin(j * (lam_im * dt))
    ab_re, ab_im = pw_re[1], pw_im[1]
    den = lam_re * lam_re + lam_im * lam_im
    coef_re = ((ab_re - 1.0) * lam_re + ab_im * lam_im) / den
    coef_im = (ab_im * lam_re - (ab_re - 1.0) * lam_im) / den
    bb_re = coef_re[..., None] * b_re - coef_im[..., None] * b_im
    bb_im = coef_re[..., None] * b_im + coef_im[..., None] * b_re
    drv_re = pw_re[..., None] * bb_re - pw_im[..., None] * bb_im
    drv_im = pw_re[..., None] * bb_im + pw_im[..., None] * bb_re
    rd_re = c_re * pw_re[:, :, :, None, :] - c_im * pw_im[:, :, :, None, :]
    rd_im = -(c_re * pw_im[:, :, :, None, :] + c_im * pw_re[:, :, :, None, :])
    taps = (jnp.einsum('dgop,jdgpi->jdgoi', c_re, drv_re[:ll], precision=HIGHEST)
            - jnp.einsum('dgop,jdgpi->jdgoi', c_im, drv_im[:ll], precision=HIGHEST))
    r = jnp.arange(ll)
    eye = jnp.eye(gpb, dtype=F32)
    m_in, m_drv, m_rd = [], [], []
    for d in range(2):
        lag = (r[None, :] - r[:, None]) if d == 0 else (r[:, None] - r[None, :])
        tp = jnp.where((lag >= 0)[:, :, None, None, None],
                       taps[:, d][jnp.clip(lag, 0, ll - 1)], 0.0)
        tp = tp.reshape(ll, ll, nblk, gpb, gc, gc)
        m = jnp.einsum('rsbgoi,gh->brgisho', tp, eye)
        m_in.append(m.reshape(nblk, ll * LANES, ll * LANES))
        steps = (ll - 1 - r) if d == 0 else r
        dr = jnp.stack([drv_re[:, d][steps], drv_im[:, d][steps]], axis=1)
        dr = dr.reshape(ll, 2, nblk, gpb, p, gc)
        m = jnp.einsum('rcbgpi,gh->brgichp', dr, eye)
        m_drv.append(m.reshape(nblk, ll * LANES, 2 * gpb * p))
        steps = (r + 1) if d == 0 else (ll - r)
        rd = jnp.stack([rd_re[:, d][steps], rd_im[:, d][steps]], axis=1)
        rd = rd.reshape(ll, 2, nblk, gpb, gc, p)
        m = jnp.einsum('rcbgop,gh->bcgprho', rd, eye)
        m_rd.append(m.reshape(nblk, 2 * gpb * p, ll * LANES))
    a8 = jnp.stack([pw_re[ll], pw_im[ll]], axis=1).reshape(2, 2, nblk, gpb * p).transpose(0, 2, 1, 3)
    stack = lambda ms: jnp.stack(ms).astype(BF16)
    return stack(m_in), stack(m_drv), stack(m_rd), a8


def _s5_kernel(u_ref, min_ref, mdrv_ref, mrd_ref, a8_ref, y_ref, v_ref, st_ref, *, n_ctx_cols):
    bsz, n_oct, ll, n_col, lanes = u_ref.shape
    n_rows = bsz * n_oct * n_col
    half = st_ref.shape[-1] // 2
    d = pl.program_id(0)
    k = pl.program_id(2)

    @pl.when(k == 0)
    def _():
        st_ref[...] = jnp.zeros_like(st_ref)

    x = jnp.concatenate([u_ref[:, :, r].reshape(n_rows, lanes) for r in range(ll)], axis=-1)
    x = x.astype(BF16)
    inj_all = jnp.dot(x, mdrv_ref[0, 0], preferred_element_type=F32)
    n_pl = v_ref.shape[0]
    for c in range(n_pl):
        v_ref[c] = inj_all[:, c * lanes:(c + 1) * lanes]

    a_re = jnp.broadcast_to(a8_ref[0, 0, 0:1], (bsz, half))
    a_im = jnp.broadcast_to(a8_ref[0, 0, 1:2], (bsz, half))
    n_cols_here = jnp.where(k == 0, n_ctx_cols, n_col)
    n_steps = n_cols_here * n_oct

    def step(i, carry):
        s_re, s_im = carry
        i = jnp.where(d == 0, i, n_steps - 1 - i)
        row = (i % n_oct) * n_col + i // n_oct
        rows = pl.ds(row, bsz, stride=n_oct * n_col)
        inj = jnp.concatenate([v_ref[c, rows, :] for c in range(n_pl)], axis=-1)
        for c in range(n_pl // 2):
            v_ref[c, rows, :] = s_re[:, c * lanes:(c + 1) * lanes]
            v_ref[n_pl // 2 + c, rows, :] = s_im[:, c * lanes:(c + 1) * lanes]
        return (a_re * s_re - a_im * s_im + inj[:, :half],
                a_re * s_im + a_im * s_re + inj[:, half:])

    s_re, s_im = lax.fori_loop(0, n_steps, step, (st_ref[:, :half], st_ref[:, half:]))
    st_ref[:, :half] = s_re
    st_ref[:, half:] = s_im

    s_start = jnp.concatenate([v_ref[c] for c in range(n_pl)], axis=-1).astype(BF16)
    y = (jnp.dot(x, min_ref[0, 0], preferred_element_type=F32)
         + jnp.dot(s_start, mrd_ref[0, 0], preferred_element_type=F32))
    for r in range(ll):
        y_ref[0, :, :, r] = y[:, r * lanes:(r + 1) * lanes].reshape(bsz, n_oct, n_col, lanes)


def _s5_scan(a_all, n_ctx, mats):
    bsz, t, d = a_all.shape
    m_in, m_drv, m_rd, a8 = mats
    n_rows = (t - n_ctx) // GRID_W
    n_ctx_cols = n_ctx // n_rows
    a_lat = a_all[:, n_ctx:].reshape(bsz, n_rows, GRID_W, d)
    a_ctx = a_all[:, :n_ctx].reshape(bsz, n_ctx_cols, n_rows, d).transpose(0, 2, 1, 3)
    a_ctx = jnp.pad(a_ctx, ((0, 0), (0, 0), (0, S5_COLS - n_ctx_cols), (0, 0)))
    u = jnp.concatenate([a_ctx, a_lat], axis=2)
    n_tiles = u.shape[2] // S5_COLS
    n_oct = n_rows // S5_CHUNK
    nblk = d // LANES
    u = u.reshape(bsz, n_oct, S5_CHUNK, n_tiles * S5_COLS, d)

    def tile(dd, k):
        return jnp.where((dd == 0) | (k == 0), k, n_tiles - k)

    blk = (bsz, n_oct, S5_CHUNK, S5_COLS, LANES)
    wspec = lambda shape: pl.BlockSpec((1, 1) + shape, lambda dd, j, k: (dd, j, 0, 0))
    kl = S5_CHUNK * LANES
    ns = m_drv.shape[-1]
    y = pl.pallas_call(
        functools.partial(_s5_kernel, n_ctx_cols=n_ctx_cols),
        grid=(2, nblk, n_tiles),
        in_specs=[pl.BlockSpec(blk, lambda dd, j, k: (0, 0, 0, tile(dd, k), j)),
                  wspec((kl, kl)), wspec((kl, ns)), wspec((ns, kl)),
                  pl.BlockSpec((1, 1, 2, ns // 2), lambda dd, j, k: (dd, j, 0, 0))],
        out_specs=pl.BlockSpec((1,) + blk, lambda dd, j, k: (dd, 0, 0, 0, tile(dd, k), j)),
        out_shape=jax.ShapeDtypeStruct((2,) + u.shape, F32),
        scratch_shapes=[pltpu.VMEM((ns // LANES, bsz * n_oct * S5_COLS, LANES), F32),
                        pltpu.VMEM((bsz, ns), F32)],
        compiler_params=_params("parallel", "parallel", "arbitrary"),
        name="s5_scan",
    )(u, m_in, m_drv, m_rd, a8)
    return (y.reshape(2, bsz, n_rows, n_tiles * S5_COLS, d),
            u.reshape(bsz, n_rows, n_tiles * S5_COLS, d))


def _s5_glu_kernel(y_ref, u_ref, h_ref, m_ref, dsk_ref, w_ref, bg_ref, g2_ref, rw_ref, rb_ref,
                   h1_ref, f_ref, ti_ref, tw_ref):
    d = h_ref.shape[-1]
    n = h_ref.shape[1] * h_ref.shape[2]
    y = (y_ref[0, 0] + y_ref[1, 0] + dsk_ref[...] * u_ref[0]).reshape(n, d)
    z = jnp.dot(jax.nn.gelu(y).astype(BF16), w_ref[...], preferred_element_type=F32) + bg_ref[...]
    y = z[:, :d] * _sigmoid(z[:, d:])
    _mixer_tail(h_ref[0].reshape(n, d), y, m_ref[0], g2_ref[...], rw_ref[...], rb_ref[...],
                h1_ref, f_ref, ti_ref.at[0], tw_ref.at[0])


def _s5_glu(y, u, h_lat, mod, d_skip, w_glu, b_glu, g2, rw, rb):
    bsz, t, d = h_lat.shape
    n_rows = t // GRID_W
    n_e = rw.shape[-1]
    n_tok = n_rows * S5_COLS
    nt = GRID_W // S5_COLS
    h4 = h_lat.reshape(bsz, n_rows, GRID_W, d)
    lat = pl.BlockSpec((1, n_rows, S5_COLS, d), lambda b, i: (b, 0, i, 0))
    tiled = pl.BlockSpec((1, n_rows, S5_COLS, d), lambda b, i: (b, 0, i + 1, 0))
    flat = lambda w: pl.BlockSpec((1, n_tok, w), lambda b, i: (b * nt + i, 0, 0))
    const = lambda shape: pl.BlockSpec(shape, lambda b, i: (0,) * len(shape))
    return pl.pallas_call(
        _s5_glu_kernel,
        grid=(bsz, nt),
        in_specs=[pl.BlockSpec((2, 1, n_rows, S5_COLS, d), lambda b, i: (0, b, 0, i + 1, 0)),
                  tiled, lat,
                  pl.BlockSpec((1, 6, d), lambda b, i: (2 * b + 1, 0, 0)),
                  const((1, d)), const((d, 2 * d)), const((1, 2 * d)), const((1, d)),
                  const((d, n_e)), const((1, n_e))],
        out_specs=[lat, flat(d), flat(TOP_K), flat(TOP_K)],
        out_shape=[jax.ShapeDtypeStruct((bsz, n_rows, GRID_W, d), F32),
                   jax.ShapeDtypeStruct((bsz * nt, n_tok, d), F32),
                   jax.ShapeDtypeStruct((bsz * nt, n_tok, TOP_K), jnp.int32),
                   jax.ShapeDtypeStruct((bsz * nt, n_tok, TOP_K), F32)],
        compiler_params=_params("parallel", "parallel"),
        name="s5_glu",
    )(y, u, h4, mod, d_skip.reshape(1, d), w_glu.astype(BF16), b_glu.reshape(1, 2 * d),
      g2.reshape(1, d), rw, rb.reshape(1, n_e))


def _combine1t_kernel(y_ref, tw_ref, h_ref, m_ref, g_ref, o_ref):
    n, d = y_ref.shape[0], h_ref.shape[-1]
    h2 = h_ref[0].reshape(n, d) + m_ref[0][5:6] * _combine(y_ref, tw_ref)
    o_ref[0] = _rms(h2, g_ref[...]).reshape(o_ref.shape[1:])


def _combine1t(y4, top_w, h1, mod, final_g):
    bsz, n_rows, _, d = h1.shape
    nt = GRID_W // S5_COLS
    n_tok = n_rows * S5_COLS
    lat = pl.BlockSpec((1, n_rows, S5_COLS, d), lambda b, i: (b, 0, i, 0))
    return pl.pallas_call(
        _combine1t_kernel,
        grid=(bsz, nt),
        in_specs=[pl.BlockSpec((n_tok, TOP_K * d), lambda b, i: (b * nt + i, 0)),
                  pl.BlockSpec((1, n_tok, TOP_K), lambda b, i: (b * nt + i, 0, 0)),
                  lat, pl.BlockSpec((1, 6, d), lambda b, i: (2 * b + 1, 0, 0)),
                  pl.BlockSpec((1, d), lambda b, i: (0, 0))],
        out_specs=lat,
        out_shape=jax.ShapeDtypeStruct(h1.shape, F32),
        compiler_params=_params("parallel", "parallel"),
        name="combine1",
    )(y4, top_w, h1, mod, final_g.reshape(1, d))


def kernel(x, c, ctx, c_ctx, mod_w, mod_b, norm1_g, norm2_g, hg_w_in, hg_lb_logits, hg_norm_g,
           hg_w_out, s5_a_re, s5_a_im, s5_log_dt, s5_b_re, s5_b_im, s5_c_re, s5_c_im, s5_d, s5_w_glu,
           s5_b_glu, router_w, router_b, moe_w1, moe_b1, moe_w2, moe_b2, final_g):
    bsz, seq, d = x.shape
    n_ctx = ctx.shape[1]
    assert mod_w.shape[0] == 2 and n_ctx == TOK_TILE and seq % TOK_TILE == 0
    n_ctx_tiles = n_ctx // TOK_TILE
    lb_all = jnp.cumsum(jax.nn.softmax(hg_lb_logits.astype(F32), axis=0), axis=0)
    mod = _modulation(c, c_ctx, mod_w, mod_b)
    h = jnp.concatenate([ctx, x], axis=1)

    qs, v, sg, kf, kb, lff, lfb = _hg_proj(h, mod[0], norm1_g[0], hg_w_in[0], lb_all[0], n_ctx_tiles)
    o_f, o_b = _gla(qs, v, kf, kb, lff, lfb, n_ctx // HG_CHUNK)
    h1, f, top_i, top_w = _hg_readout(o_f, o_b, sg, h, mod[0], hg_norm_g[0], hg_w_out[0], norm2_g[0],
                                      router_w[0], router_b[0], n_ctx_tiles)
    y4 = _moe_ffn(f.reshape(-1, d), top_i.reshape(-1, TOP_K), moe_w1[0], moe_b1[0], moe_w2[0], moe_b2[0])
    h2, a1 = _combine0(y4, top_w, h1, mod[0], mod[1], norm1_g[1], n_ctx_tiles)

    mats = _s5_matrices(s5_a_re[0], s5_a_im[0], s5_log_dt[0], s5_b_re[0], s5_b_im[0],
                        s5_c_re[0], s5_c_im[0])
    y_s5, u = _s5_scan(a1, n_ctx, mats)
    h1, f, top_i, top_w = _s5_glu(y_s5, u, h2[:, n_ctx:], mod[1], s5_d[0], s5_w_glu[0], s5_b_glu[0],
                                  norm2_g[1], router_w[1], router_b[1])
    y4 = _moe_ffn(f.reshape(-1, d), top_i.reshape(-1, TOP_K), moe_w1[1], moe_b1[1], moe_w2[1], moe_b2[1])
    out = _combine1t(y4, top_w, h1, mod[1], final_g)
    return out.reshape(bsz, seq, d)
```

```python
import functools

import jax
import jax.numpy as jnp
from jax import lax
from jax.experimental import pallas as pl
from jax.experimental.pallas import tpu as pltpu

F32 = jnp.float32
BF16 = jnp.bfloat16
HIGHEST = lax.Precision.HIGHEST

RMS_EPS = 1e-6
GRID_W = 64
HEAD_DIM = 128
HG_CHUNK = 64
S5_GROUP = 16
S5_STATE = 64
S5_EIG_MAX = -1e-4
S5_CHUNK = 8
S5_COLS = 8
LANES = 128
N_EXPERTS = 32
TOP_K = 4
SWIGLU_ALPHA = 1.702
SWIGLU_LIMIT = 7.0
TOK_TILE = 256
MOE_ROWS = 256
VMEM_LIMIT = 56 * 1024 * 1024


def _params(*sem):
    return pltpu.CompilerParams(dimension_semantics=sem, vmem_limit_bytes=VMEM_LIMIT)


def _rms(x, g):
    return x * lax.rsqrt(jnp.mean(x * x, axis=-1, keepdims=True) + RMS_EPS) * g


def _sigmoid(x):
    return 1.0 / (1.0 + jnp.exp(-x))


def _silu(x):
    return x * _sigmoid(x)


def _mod_kernel(c_ref, w_ref, b_ref, o_ref):
    s = _silu(c_ref[...])
    o_ref[0] = jnp.dot(s, w_ref[0], precision=HIGHEST, preferred_element_type=F32) + b_ref[0]


def _modulation(c, c_ctx, mod_w, mod_b):
    bsz, d = c.shape
    depth = mod_w.shape[0]
    rows = jnp.concatenate([c, c_ctx[None], jnp.zeros((16 - bsz - 1, d), F32)], axis=0)
    bn = 6 * d // 4
    out = pl.pallas_call(
        _mod_kernel,
        grid=(depth, 4),
        in_specs=[pl.BlockSpec((16, d), lambda l, j: (0, 0)),
                  pl.BlockSpec((1, d, bn), lambda l, j: (l, 0, j)),
                  pl.BlockSpec((1, 1, bn), lambda l, j: (l, 0, j))],
        out_specs=pl.BlockSpec((1, 16, bn), lambda l, j: (l, 0, j)),
        out_shape=jax.ShapeDtypeStruct((depth, 16, 6 * d), F32),
        compiler_params=_params("parallel", "parallel"),
        name="modulation",
    )(rows, mod_w, mod_b.reshape(depth, 1, 6 * d))
    m_lat = out[:, :bsz].reshape(depth, bsz, 1, 6, d)
    m_ctx = jnp.broadcast_to(out[:, bsz].reshape(depth, 1, 1, 6, d), m_lat.shape)
    return jnp.concatenate([m_ctx, m_lat], axis=2).reshape(depth, 2 * bsz, 6, d)


def _hg_proj_kernel(h_ref, m_ref, g_ref, w_ref, lb_ref,
                    qs_ref, v_ref, sg_ref, kf_ref, kb_ref, lff_ref, lfb_ref):
    d = h_ref.shape[-1]
    m = m_ref[0]
    a = _rms(h_ref[0], g_ref[...]) * (1.0 + m[1:2]) + m[0:1]
    p = jnp.dot(a.astype(BF16), w_ref[...], preferred_element_type=F32)
    qs_ref[0] = _silu(p[:, 0:d]).astype(BF16)
    v_ref[0] = p[:, d:2 * d].astype(BF16)
    sg_ref[0] = _silu(p[:, 4 * d:5 * d]).astype(BF16)
    for di, (k_ref, lf_ref) in enumerate(((kf_ref, lff_ref), (kb_ref, lfb_ref))):
        lb = lb_ref[di:di + 1]
        f = lb + (1.0 - lb) * _sigmoid(p[:, (2 + di) * d:(3 + di) * d])
        k_ref[0] = (1.0 - f).astype(BF16)
        lf_ref[0] = jnp.log(f)


def _mod_spec(d, n_ctx_tiles):
    return pl.BlockSpec((1, 6, d), lambda b, i: (2 * b + (i >= n_ctx_tiles).astype(jnp.int32), 0, 0))


def _hg_proj(h, mod, g1, w_in, lb, n_ctx_tiles):
    bsz, t, d = h.shape
    tok = pl.BlockSpec((1, TOK_TILE, d), lambda b, i: (b, i, 0))
    bf = jax.ShapeDtypeStruct((bsz, t, d), BF16)
    ff = jax.ShapeDtypeStruct((bsz, t, d), F32)
    return pl.pallas_call(
        _hg_proj_kernel,
        grid=(bsz, t // TOK_TILE),
        in_specs=[tok, _mod_spec(d, n_ctx_tiles),
                  pl.BlockSpec((1, d), lambda b, i: (0, 0)),
                  pl.BlockSpec((d, 5 * d), lambda b, i: (0, 0)),
                  pl.BlockSpec((2, d), lambda b, i: (0, 0))],
        out_specs=[tok] * 7,
        out_shape=[bf, bf, bf, bf, bf, ff, ff],
        compiler_params=_params("parallel", "parallel"),
        name="hg_proj",
    )(h, mod, g1.reshape(1, d), w_in.astype(BF16), lb)


def _split3(x):
    hi = x.astype(BF16)
    r = x - hi.astype(F32)
    mid = r.astype(BF16)
    lo = (r - mid.astype(F32)).astype(BF16)
    return hi, mid, lo


def _gla_kernel(qf_ref, vf_ref, kf_ref, lf_ref, qb_ref, vb_ref, kb_ref, lb_ref,
                of_ref, ob_ref, sf_ref, sb_ref):
    c = HG_CHUNK
    n_heads = sf_ref.shape[0]

    @pl.when(pl.program_id(1) == 0)
    def _():
        sf_ref[...] = jnp.zeros_like(sf_ref)
        sb_ref[...] = jnp.zeros_like(sb_ref)

    row = lax.broadcasted_iota(jnp.int32, (c, c), 0)
    col = lax.broadcasted_iota(jnp.int32, (c, c), 1)
    dirs = ((qf_ref, vf_ref, kf_ref, lf_ref, of_ref, sf_ref, col <= row, c // 2 - 1, c - 1),
            (qb_ref, vb_ref, kb_ref, lb_ref, ob_ref, sb_ref, col >= row, c // 2, 0))
    for q_ref, v_ref, k_ref, l_ref, o_ref, s_ref, keep, r_ref, r_last in dirs:
        tri = keep.astype(BF16)
        b = sum(jnp.dot(tri, part, preferred_element_type=F32) for part in _split3(l_ref[0]))
        b_ref = b[r_ref:r_ref + 1]
        b_last = b[r_last:r_last + 1]
        a_in = q_ref[0].astype(F32) * jnp.exp(b - b_ref)
        k_in = k_ref[0].astype(F32) * jnp.exp(b_ref - b)
        q_st = (a_in * jnp.exp(b_ref)).astype(BF16)
        k_st = (k_in * jnp.exp(b_last - b_ref)).astype(BF16)
        a_in = a_in.astype(BF16)
        k_in = k_in.astype(BF16)
        decay = jnp.exp(b_last)
        v = v_ref[0]
        for h in range(n_heads):
            sl = slice(h * HEAD_DIM, (h + 1) * HEAD_DIM)
            sc = lax.dot_general(a_in[:, sl], k_in[:, sl], (((1,), (1,)), ((), ())),
                                 preferred_element_type=F32)
            sc = jnp.where(keep, sc, 0.0).astype(BF16)
            o = jnp.dot(sc, v[:, sl], preferred_element_type=F32)
            st = s_ref[h]
            o = o + lax.dot_general(q_st[:, sl], st.astype(BF16), (((1,), (1,)), ((), ())),
                                    preferred_element_type=F32)
            s_ref[h] = st * decay[:, sl] + lax.dot_general(
                v[:, sl], k_st[:, sl], (((0,), (0,)), ((), ())), preferred_element_type=F32)
            o_ref[0, :, sl] = o


def _gla(qs, v, kf, kb, lff, lfb, n_ctx_chunks):
    bsz, t, d = qs.shape
    n = t // HG_CHUNK
    n_heads = d // HEAD_DIM

    def rev(j):
        return jnp.where(j < n_ctx_chunks, n_ctx_chunks - 1 - j, n + n_ctx_chunks - 1 - j)

    fwd = pl.BlockSpec((1, HG_CHUNK, d), lambda b, j: (b, j, 0))
    bwd = pl.BlockSpec((1, HG_CHUNK, d), lambda b, j: (b, rev(j), 0))
    out = jax.ShapeDtypeStruct((bsz, t, d), F32)
    state = pltpu.VMEM((n_heads, HEAD_DIM, HEAD_DIM), F32)
    return pl.pallas_call(
        _gla_kernel,
        grid=(bsz, n),
        in_specs=[fwd, fwd, fwd, fwd, bwd, bwd, bwd, bwd],
        out_specs=[fwd, bwd],
        out_shape=[out, out],
        scratch_shapes=[state, state],
        compiler_params=_params("parallel", "arbitrary"),
        name="gla",
    )(qs, v, kf, lff, qs, v, kb, lfb)


def _mixer_tail(first, h, y, m, g2, rw, rb, run_ref, h1_ref, f_ref, tw_ref, meta_ref, cnt_ref):
    h1 = h + m[2:3] * y
    f = _rms(h1, g2) * (1.0 + m[4:5]) + m[3:4]
    h1_ref[...] = h1.reshape(h1_ref.shape)
    f_ref[...] = f.reshape(f_ref.shape)
    logits = jnp.dot(f, rw, precision=HIGHEST, preferred_element_type=F32) + rb
    n, n_e = logits.shape
    lane = lax.broadcasted_iota(jnp.int32, logits.shape, 1).astype(F32)
    vals, idxs, hots = [], [], []
    for _ in range(TOP_K):
        mx = jnp.max(logits, axis=-1, keepdims=True)
        ix = jnp.min(jnp.where(logits == mx, lane, float(n_e)), axis=-1, keepdims=True)
        hot = lane == ix
        vals.append(mx)
        idxs.append(ix)
        hots.append(hot)
        logits = jnp.where(hot, -jnp.inf, logits)
    es = [jnp.exp(x - vals[0]) for x in vals]
    tot = sum(es)
    for k in range(TOP_K):
        tw_ref[:, k:k + 1] = es[k] / tot

    @pl.when(first)
    def _():
        run_ref[...] = jnp.zeros_like(run_ref)

    picked = sum(hot.astype(F32) for hot in hots)
    r_i = lax.broadcasted_iota(jnp.int32, (n, n), 0)
    c_i = lax.broadcasted_iota(jnp.int32, (n, n), 1)
    earlier = jnp.dot((c_i < r_i).astype(BF16), picked.astype(BF16), preferred_element_type=F32)
    rank = earlier + run_ref[...]
    total = run_ref[...] + jnp.sum(picked, axis=0, keepdims=True)
    run_ref[...] = total
    cnt_ref[...] = total
    col = lax.broadcasted_iota(jnp.int32, (n, LANES), 1)
    z = jnp.zeros((n, LANES), F32)
    for k in range(TOP_K):
        pos = jnp.sum(jnp.where(hots[k], rank, 0.0), axis=-1, keepdims=True)
        z = jnp.where(col == k, idxs[k], z)
        z = jnp.where(col == TOP_K + k, pos, z)
    meta_ref[0] = z.T[0:2 * TOP_K].astype(jnp.int32)


def _tail_out(n_tiles, tile, lead_shape, d, n_e):
    del lead_shape
    return [jax.ShapeDtypeStruct((n_tiles, tile, TOP_K), F32),
            jax.ShapeDtypeStruct((n_tiles, 2 * TOP_K, tile), jnp.int32),
            jax.ShapeDtypeStruct((1, n_e), F32)]


def _hg_readout_kernel(of_ref, ob_ref, sg_ref, h_ref, m_ref, ng_ref, w_ref, g2_ref, rw_ref, rb_ref,
                       h1_ref, f_ref, tw_ref, meta_ref, cnt_ref, run_ref):
    d = h_ref.shape[-1]
    o = of_ref[0] + ob_ref[0]
    parts = []
    for h in range(d // HEAD_DIM):
        oh = o[:, h * HEAD_DIM:(h + 1) * HEAD_DIM]
        parts.append(oh * lax.rsqrt(jnp.mean(oh * oh, axis=-1, keepdims=True) + RMS_EPS))
    o = jnp.concatenate(parts, axis=-1) * ng_ref[...]
    y = jnp.dot((o * sg_ref[0].astype(F32)).astype(BF16), w_ref[...], preferred_element_type=F32)
    first = (pl.program_id(0) == 0) & (pl.program_id(1) == 0)
    _mixer_tail(first, h_ref[0], y, m_ref[0], g2_ref[...], rw_ref[...], rb_ref[...], run_ref,
                h1_ref, f_ref, tw_ref.at[0], meta_ref, cnt_ref)


def _hg_readout(o_f, o_b, sg, h, mod, norm_g, w_out, g2, rw, rb, n_ctx_tiles):
    bsz, t, d = h.shape
    n_e = rw.shape[-1]
    nt = t // TOK_TILE
    tok = pl.BlockSpec((1, TOK_TILE, d), lambda b, i: (b, i, 0))
    const = lambda shape: pl.BlockSpec(shape, lambda b, i: (0,) * len(shape))
    return pl.pallas_call(
        _hg_readout_kernel,
        grid=(bsz, nt),
        in_specs=[tok, tok, tok, tok, _mod_spec(d, n_ctx_tiles),
                  const((1, d)), const((d, d)), const((1, d)), const((d, n_e)), const((1, n_e))],
        out_specs=[tok, tok,
                   pl.BlockSpec((1, TOK_TILE, TOP_K), lambda b, i: (b * nt + i, 0, 0)),
                   pl.BlockSpec((1, 2 * TOP_K, TOK_TILE), lambda b, i: (b * nt + i, 0, 0)),
                   const((1, n_e))],
        out_shape=[jax.ShapeDtypeStruct((bsz, t, d), F32), jax.ShapeDtypeStruct((bsz, t, d), F32)]
        + _tail_out(bsz * nt, TOK_TILE, None, d, n_e),
        scratch_shapes=[pltpu.VMEM((1, n_e), F32)],
        compiler_params=_params("arbitrary", "arbitrary"),
        name="hg_readout",
    )(o_f, o_b, sg, h, mod, norm_g.reshape(1, d), w_out.astype(BF16), g2.reshape(1, d),
      rw, rb.reshape(1, n_e))


def _moe_plan(meta, counts, rows):
    n_tiles, _, tile = meta.shape
    n_asg = n_tiles * tile * TOP_K
    counts = counts.reshape(-1).astype(jnp.int32)
    padded = (counts + rows - 1) // rows * rows
    pad_end = jnp.cumsum(padded)
    pad_start = pad_end - padded
    n_blocks = -(-n_asg // rows) + N_EXPERTS
    e_ids = meta[:, :TOP_K]
    hot = e_ids[..., None] == jnp.arange(N_EXPERTS, dtype=jnp.int32)
    dest = meta[:, TOP_K:] + jnp.sum(jnp.where(hot, pad_start, 0), axis=-1)
    block_pos = jnp.arange(n_blocks, dtype=jnp.int32) * rows
    block_e = jnp.minimum(jnp.sum((block_pos[:, None] >= pad_end[None, :]).astype(jnp.int32), axis=1),
                          N_EXPERTS - 1)
    n_used = (pad_end[-1:] // rows).astype(jnp.int32)
    pad_lo = jnp.concatenate([pad_start + counts, pad_end[-1:]])
    pad_hi = jnp.concatenate([pad_end, jnp.full((1,), n_blocks * rows, jnp.int32)])
    return dest.astype(jnp.int32), block_e, n_used, pad_lo, pad_hi, n_blocks


def _dispatch_kernel(lo_ref, hi_ref, dest_ref, f_ref, x_hbm, zrow, sem, zsem):
    tile = f_ref.shape[0]

    @pl.when(pl.program_id(0) == 0)
    def _():
        zrow[...] = jnp.zeros_like(zrow)

        def per_expert(e, total):
            def fill(r, carry):
                pltpu.make_async_copy(zrow, x_hbm.at[pl.ds(r, 1)], zsem).start()
                return carry
            lax.fori_loop(lo_ref[e], hi_ref[e], fill, 0)
            return total + hi_ref[e] - lo_ref[e]
        total = lax.fori_loop(0, lo_ref.shape[0], per_expert, 0)

        def drain(r, carry):
            pltpu.make_async_copy(zrow, x_hbm.at[pl.ds(0, 1)], zsem).wait()
            return carry
        lax.fori_loop(0, total, drain, 0)

    for k in range(TOP_K):
        def body(r, carry):
            pltpu.make_async_copy(f_ref.at[pl.ds(r, 1)], x_hbm.at[pl.ds(dest_ref[0, k, r], 1)],
                                  sem).start()
            return carry
        lax.fori_loop(0, tile, body, 0, unroll=8)
    for k in range(TOP_K):
        pltpu.make_async_copy(f_ref, x_hbm.at[pl.ds(0, tile)], sem).wait()


def _dispatch(f, dest, pad_lo, pad_hi, n_slots):
    n_tiles, _, tile = dest.shape
    d = f.shape[-1]
    grid_spec = pltpu.PrefetchScalarGridSpec(
        num_scalar_prefetch=2,
        grid=(n_tiles,),
        in_specs=[pl.BlockSpec((1, TOP_K, tile), lambda i, lo, hi: (i, 0, 0), memory_space=pltpu.SMEM),
                  pl.BlockSpec((tile, d), lambda i, lo, hi: (i, 0))],
        out_specs=pl.BlockSpec(memory_space=pl.ANY),
        scratch_shapes=[pltpu.VMEM((1, d), F32), pltpu.SemaphoreType.DMA(()),
                        pltpu.SemaphoreType.DMA(())])
    return pl.pallas_call(
        _dispatch_kernel,
        grid_spec=grid_spec,
        out_shape=jax.ShapeDtypeStruct((n_slots, d), F32),
        compiler_params=_params("arbitrary"),
        name="moe_dispatch",
    )(pad_lo, pad_hi, dest, f.reshape(n_tiles * tile, d))


def _ffn_kernel(be_ref, nu_ref, x_ref, w1_ref, b1_ref, w2_ref, b2_ref, y_ref, w1c, w2c):
    b = pl.program_id(0)

    @pl.when(b >= nu_ref[0])
    def _():
        y_ref[...] = jnp.zeros_like(y_ref)

    @pl.when(b < nu_ref[0])
    def _():
        @pl.when((b == 0) | (be_ref[b] != be_ref[jnp.maximum(b - 1, 0)]))
        def _():
            w1c[...] = w1_ref[0].astype(BF16)
            w2c[...] = w2_ref[0].astype(BF16)

        f = w2c.shape[0]
        z = jnp.dot(x_ref[...].astype(BF16), w1c[...], preferred_element_type=F32) + b1_ref[0]
        z_glu = jnp.minimum(z[:, :f], SWIGLU_LIMIT)
        z_lin = jnp.clip(z[:, f:], -SWIGLU_LIMIT, SWIGLU_LIMIT)
        act = z_glu * _sigmoid(SWIGLU_ALPHA * z_glu) * (z_lin + 1.0)
        y_ref[...] = jnp.dot(act.astype(BF16), w2c[...], preferred_element_type=F32) + b2_ref[0]


def _moe_ffn(x_sorted, block_e, n_used, w1, b1, w2, b2):
    n_slots, d = x_sorted.shape
    n_e, _, f2 = w1.shape
    n_blocks = n_slots // MOE_ROWS
    live = lambda b, nu: jnp.minimum(b, nu[0] - 1)
    rows = pl.BlockSpec((MOE_ROWS, d), lambda b, be, nu: (live(b, nu), 0))
    per_e = lambda shape: pl.BlockSpec((1,) + shape, lambda b, be, nu: (be[live(b, nu)], 0, 0))
    grid_spec = pltpu.PrefetchScalarGridSpec(
        num_scalar_prefetch=2,
        grid=(n_blocks,),
        in_specs=[rows, per_e((d, f2)), per_e((1, f2)), per_e((f2 // 2, d)), per_e((1, d))],
        out_specs=pl.BlockSpec((MOE_ROWS, d), lambda b, be, nu: (b, 0)),
        scratch_shapes=[pltpu.VMEM((d, f2), BF16), pltpu.VMEM((f2 // 2, d), BF16)])
    return pl.pallas_call(
        _ffn_kernel,
        grid_spec=grid_spec,
        out_shape=jax.ShapeDtypeStruct((n_slots, d), F32),
        compiler_params=_params("arbitrary"),
        name="moe_ffn",
    )(block_e, n_used, x_sorted, w1, b1.reshape(n_e, 1, f2), w2, b2.reshape(n_e, 1, d))


def _moe(f, meta, counts, w1, b1, w2, b2):
    dest, block_e, n_used, pad_lo, pad_hi, n_blocks = _moe_plan(meta, counts, MOE_ROWS)
    x_sorted = _dispatch(f, dest, pad_lo, pad_hi, n_blocks * MOE_ROWS)
    return _moe_ffn(x_sorted, block_e, n_used, w1, b1, w2, b2), dest


def _gather_combine(dest_ref, y_hbm, tw_ref, ybuf, sem):
    tile = ybuf.shape[1]
    for k in range(TOP_K):
        def body(r, carry):
            pltpu.make_async_copy(y_hbm.at[pl.ds(dest_ref[0, k, r], 1)], ybuf.at[k, pl.ds(r, 1)],
                                  sem).start()
            return carry
        lax.fori_loop(0, tile, body, 0, unroll=8)
    for k in range(TOP_K):
        pltpu.make_async_copy(y_hbm.at[pl.ds(0, tile)], ybuf.at[k], sem).wait()
    tw = tw_ref[0]
    return sum(tw[:, k:k + 1] * ybuf[k] for k in range(TOP_K))


def _combine0_kernel(dest_ref, y_hbm, tw_ref, h_ref, m0_ref, m1_ref, g_ref,
                     hl_ref, al_ref, ac_ref, ybuf, sem, *, n_ctx_tiles):
    h2 = h_ref[0] + m0_ref[0][5:6] * _gather_combine(dest_ref, y_hbm, tw_ref, ybuf, sem)
    m1 = m1_ref[0]
    a = _rms(h2, g_ref[...]) * (1.0 + m1[1:2]) + m1[0:1]
    is_ctx = pl.program_id(1) < n_ctx_tiles

    @pl.when(is_ctx)
    def _():
        ac_ref[0] = a

    @pl.when(jnp.logical_not(is_ctx))
    def _():
        hl_ref[0] = h2
        al_ref[0] = a


def _combine0(y_sorted, dest, top_w, h1, mod0, mod1, g1_next, n_ctx_tiles):
    bsz, t, d = h1.shape
    nt = t // TOK_TILE
    n_ctx = n_ctx_tiles * TOK_TILE
    tok = pl.BlockSpec((1, TOK_TILE, d), lambda b, i: (b, i, 0))
    lat = pl.BlockSpec((1, TOK_TILE, d), lambda b, i: (b, jnp.maximum(i - n_ctx_tiles, 0), 0))
    ctx = pl.BlockSpec((1, TOK_TILE, d), lambda b, i: (b, jnp.minimum(i, n_ctx_tiles - 1), 0))
    return pl.pallas_call(
        functools.partial(_combine0_kernel, n_ctx_tiles=n_ctx_tiles),
        grid=(bsz, nt),
        in_specs=[pl.BlockSpec((1, TOP_K, TOK_TILE), lambda b, i: (b * nt + i, 0, 0),
                               memory_space=pltpu.SMEM),
                  pl.BlockSpec(memory_space=pl.ANY),
                  pl.BlockSpec((1, TOK_TILE, TOP_K), lambda b, i: (b * nt + i, 0, 0)),
                  tok, _mod_spec(d, n_ctx_tiles), _mod_spec(d, n_ctx_tiles),
                  pl.BlockSpec((1, d), lambda b, i: (0, 0))],
        out_specs=[lat, lat, ctx],
        out_shape=[jax.ShapeDtypeStruct((bsz, t - n_ctx, d), F32),
                   jax.ShapeDtypeStruct((bsz, t - n_ctx, d), F32),
                   jax.ShapeDtypeStruct((bsz, n_ctx, d), F32)],
        scratch_shapes=[pltpu.VMEM((TOP_K, TOK_TILE, d), F32), pltpu.SemaphoreType.DMA(())],
        compiler_params=_params("arbitrary", "arbitrary"),
        name="combine0",
    )(dest, y_sorted, top_w, h1, mod0, mod1, g1_next.reshape(1, d))


def _s5_matrices(a_re, a_im, log_dt, b_re, b_im, c_re, c_im):
    ng, p = a_re.shape[1:]
    gc = b_re.shape[-1]
    ll = S5_CHUNK
    gpb = LANES // gc
    nblk = ng // gpb
    lam_re = jnp.minimum(a_re, S5_EIG_MAX)
    lam_im = a_im
    dt = jnp.exp(log_dt)[..., None]
    j = jnp.arange(ll + 1, dtype=F32).reshape(-1, 1, 1, 1)
    mag = jnp.exp(j * (lam_re * dt))
    pw_re = mag * jnp.cos(j * (lam_im * dt))
    pw_im = mag * jnp.sin(j * (lam_im * dt))
    ab_re, ab_im = pw_re[1], pw_im[1]
    den = lam_re * lam_re + lam_im * lam_im
    coef_re = ((ab_re - 1.0) * lam_re + ab_im * lam_im) / den
    coef_im = (ab_im * lam_re - (ab_re - 1.0) * lam_im) / den
    bb_re = coef_re[..., None] * b_re - coef_im[..., None] * b_im
    bb_im = coef_re[..., None] * b_im + coef_im[..., None] * b_re
    drv_re = pw_re[..., None] * bb_re - pw_im[..., None] * bb_im
    drv_im = pw_re[..., None] * bb_im + pw_im[..., None] * bb_re
    rd_re = c_re * pw_re[:, :, :, None, :] - c_im * pw_im[:, :, :, None, :]
    rd_im = -(c_re * pw_im[:, :, :, None, :] + c_im * pw_re[:, :, :, None, :])
    taps = (jnp.einsum('dgop,jdgpi->jdgoi', c_re, drv_re[:ll], precision=HIGHEST)
            - jnp.einsum('dgop,jdgpi->jdgoi', c_im, drv_im[:ll], precision=HIGHEST))
    r = jnp.arange(ll)
    eye = jnp.eye(gpb, dtype=F32)
    m_in, m_drv, m_rd = [], [], []
    for d in range(2):
        lag = (r[None, :] - r[:, None]) if d == 0 else (r[:, None] - r[None, :])
        tp = jnp.where((lag >= 0)[:, :, None, None, None],
                       taps[:, d][jnp.clip(lag, 0, ll - 1)], 0.0)
        tp = tp.reshape(ll, ll, nblk, gpb, gc, gc)
        m = jnp.einsum('rsbgoi,gh->brgisho', tp, eye)
        m_in.append(m.reshape(nblk, ll * LANES, ll * LANES))
        steps = (ll - 1 - r) if d == 0 else r
        dr = jnp.stack([drv_re[:, d][steps], drv_im[:, d][steps]], axis=1)
        dr = dr.reshape(ll, 2, nblk, gpb, p, gc)
        m = jnp.einsum('rcbgpi,gh->brgichp', dr, eye)
        m_drv.append(m.reshape(nblk, ll * LANES, 2 * gpb * p))
        steps = (r + 1) if d == 0 else (ll - r)
        rd = jnp.stack([rd_re[:, d][steps], rd_im[:, d][steps]], axis=1)
        rd = rd.reshape(ll, 2, nblk, gpb, gc, p)
        m = jnp.einsum('rcbgop,gh->bcgprho', rd, eye)
        m_rd.append(m.reshape(nblk, 2 * gpb * p, ll * LANES))
    a8 = jnp.stack([pw_re[ll], pw_im[ll]], axis=1).reshape(2, 2, nblk, gpb * p).transpose(0, 2, 1, 3)
    stack = lambda ms: jnp.stack(ms).astype(BF16)
    return stack(m_in), stack(m_drv), stack(m_rd), a8


def _s5_kernel(uc_ref, ul_ref, min_ref, mdrv_ref, mrd_ref, a8_ref, y_ref, v_ref, st_ref, *, n_ctx_cols):
    bsz, n_oct, ll, n_col, lanes = ul_ref.shape
    n_rows = bsz * n_oct * n_col
    half = st_ref.shape[-1] // 2
    d = pl.program_id(0)
    k = pl.program_id(2)

    @pl.when(k == 0)
    def _():
        st_ref[...] = jnp.zeros_like(st_ref)

    def chunks(u_ref):
        return jnp.concatenate([u_ref[:, :, r].reshape(n_rows, lanes) for r in range(ll)], axis=-1)

    x = jnp.where(k == 0, chunks(uc_ref), chunks(ul_ref)).astype(BF16)
    inj_all = jnp.dot(x, mdrv_ref[0, 0], preferred_element_type=F32)
    n_pl = v_ref.shape[0]
    for c in range(n_pl):
        v_ref[c] = inj_all[:, c * lanes:(c + 1) * lanes]

    a_re = jnp.broadcast_to(a8_ref[0, 0, 0:1], (bsz, half))
    a_im = jnp.broadcast_to(a8_ref[0, 0, 1:2], (bsz, half))
    n_cols_here = jnp.where(k == 0, n_ctx_cols, n_col)
    n_steps = n_cols_here * n_oct

    def step(i, carry):
        s_re, s_im = carry
        i = jnp.where(d == 0, i, n_steps - 1 - i)
        row = (i % n_oct) * n_col + i // n_oct
        rows = pl.ds(row, bsz, stride=n_oct * n_col)
        inj = jnp.concatenate([v_ref[c, rows, :] for c in range(n_pl)], axis=-1)
        for c in range(n_pl // 2):
            v_ref[c, rows, :] = s_re[:, c * lanes:(c + 1) * lanes]
            v_ref[n_pl // 2 + c, rows, :] = s_im[:, c * lanes:(c + 1) * lanes]
        return (a_re * s_re - a_im * s_im + inj[:, :half],
                a_re * s_im + a_im * s_re + inj[:, half:])

    s_re, s_im = lax.fori_loop(0, n_steps, step, (st_ref[:, :half], st_ref[:, half:]))
    st_ref[:, :half] = s_re
    st_ref[:, half:] = s_im

    s_start = jnp.concatenate([v_ref[c] for c in range(n_pl)], axis=-1).astype(BF16)
    y = (jnp.dot(x, min_ref[0, 0], preferred_element_type=F32)
         + jnp.dot(s_start, mrd_ref[0, 0], preferred_element_type=F32))
    for r in range(ll):
        y_ref[0, :, :, r] = y[:, r * lanes:(r + 1) * lanes].reshape(bsz, n_oct, n_col, lanes)


def _s5_scan(a_lat, a_ctx, mats):
    bsz, t, d = a_lat.shape
    n_ctx = a_ctx.shape[1]
    m_in, m_drv, m_rd, a8 = mats
    n_rows = t // GRID_W
    n_oct = n_rows // S5_CHUNK
    n_ctx_cols = n_ctx // n_rows
    u_lat = a_lat.reshape(bsz, n_oct, S5_CHUNK, GRID_W, d)
    u_ctx = a_ctx.reshape(bsz, n_ctx_cols, n_rows, d).transpose(0, 2, 1, 3)
    u_ctx = jnp.pad(u_ctx, ((0, 0), (0, 0), (0, S5_COLS - n_ctx_cols), (0, 0)))
    u_ctx = u_ctx.reshape(bsz, n_oct, S5_CHUNK, S5_COLS, d)
    n_lat = GRID_W // S5_COLS
    nblk = d // LANES

    def lat_tile(dd, k):
        kk = jnp.maximum(k, 1) - 1
        return jnp.where(dd == 0, kk, n_lat - 1 - kk)

    blk = (bsz, n_oct, S5_CHUNK, S5_COLS, LANES)
    wspec = lambda shape: pl.BlockSpec((1, 1) + shape, lambda dd, j, k: (dd, j, 0, 0))
    kl = S5_CHUNK * LANES
    ns = m_drv.shape[-1]
    y = pl.pallas_call(
        functools.partial(_s5_kernel, n_ctx_cols=n_ctx_cols),
        grid=(2, nblk, n_lat + 1),
        in_specs=[pl.BlockSpec(blk, lambda dd, j, k: (0, 0, 0, 0, j)),
                  pl.BlockSpec(blk, lambda dd, j, k: (0, 0, 0, lat_tile(dd, k), j)),
                  wspec((kl, kl)), wspec((kl, ns)), wspec((ns, kl)),
                  pl.BlockSpec((1, 1, 2, ns // 2), lambda dd, j, k: (dd, j, 0, 0))],
        out_specs=pl.BlockSpec((1,) + blk, lambda dd, j, k: (dd, 0, 0, 0, lat_tile(dd, k), j)),
        out_shape=jax.ShapeDtypeStruct((2,) + u_lat.shape, F32),
        scratch_shapes=[pltpu.VMEM((ns // LANES, bsz * n_oct * S5_COLS, LANES), F32),
                        pltpu.VMEM((bsz, ns), F32)],
        compiler_params=_params("parallel", "parallel", "arbitrary"),
        name="s5_scan",
    )(u_ctx, u_lat, m_in, m_drv, m_rd, a8)
    return y.reshape(2, bsz, n_rows, GRID_W, d)


def _s5_glu_kernel(y_ref, u_ref, h_ref, m_ref, dsk_ref, w_ref, bg_ref, g2_ref, rw_ref, rb_ref,
                   h1_ref, f_ref, tw_ref, meta_ref, cnt_ref, run_ref):
    d = h_ref.shape[-1]
    n = h_ref.shape[1] * h_ref.shape[2]
    y = (y_ref[0, 0] + y_ref[1, 0] + dsk_ref[...] * u_ref[0]).reshape(n, d)
    z = jnp.dot(jax.nn.gelu(y).astype(BF16), w_ref[...], preferred_element_type=F32) + bg_ref[...]
    y = z[:, :d] * _sigmoid(z[:, d:])
    first = (pl.program_id(0) == 0) & (pl.program_id(1) == 0)
    _mixer_tail(first, h_ref[0].reshape(n, d), y, m_ref[0], g2_ref[...], rw_ref[...], rb_ref[...],
                run_ref, h1_ref, f_ref, tw_ref.at[0], meta_ref, cnt_ref)


def _s5_glu(y, a_lat, h_lat, mod, d_skip, w_glu, b_glu, g2, rw, rb):
    bsz, t, d = h_lat.shape
    n_rows = t // GRID_W
    n_e = rw.shape[-1]
    n_tok = n_rows * S5_COLS
    nt = GRID_W // S5_COLS
    lat = pl.BlockSpec((1, n_rows, S5_COLS, d), lambda b, i: (b, 0, i, 0))
    flat = lambda s1, s2: pl.BlockSpec((1, s1, s2), lambda b, i: (b * nt + i, 0, 0))
    const = lambda shape: pl.BlockSpec(shape, lambda b, i: (0,) * len(shape))
    return pl.pallas_call(
        _s5_glu_kernel,
        grid=(bsz, nt),
        in_specs=[pl.BlockSpec((2, 1, n_rows, S5_COLS, d), lambda b, i: (0, b, 0, i, 0)),
                  lat, lat, pl.BlockSpec((1, 6, d), lambda b, i: (2 * b + 1, 0, 0)),
                  const((1, d)), const((d, 2 * d)), const((1, 2 * d)), const((1, d)),
                  const((d, n_e)), const((1, n_e))],
        out_specs=[lat, flat(n_tok, d), flat(n_tok, TOP_K), flat(2 * TOP_K, n_tok), const((1, n_e))],
        out_shape=[jax.ShapeDtypeStruct((bsz, n_rows, GRID_W, d), F32),
                   jax.ShapeDtypeStruct((bsz * nt, n_tok, d), F32)]
        + _tail_out(bsz * nt, n_tok, None, d, n_e),
        scratch_shapes=[pltpu.VMEM((1, n_e), F32)],
        compiler_params=_params("arbitrary", "arbitrary"),
        name="s5_glu",
    )(y, a_lat.reshape(bsz, n_rows, GRID_W, d), h_lat.reshape(bsz, n_rows, GRID_W, d), mod,
      d_skip.reshape(1, d), w_glu.astype(BF16), b_glu.reshape(1, 2 * d),
      g2.reshape(1, d), rw, rb.reshape(1, n_e))


def _combine1_kernel(dest_ref, y_hbm, tw_ref, h_ref, m_ref, g_ref, o_ref, ybuf, sem):
    n, d = ybuf.shape[1:]
    h2 = h_ref[0].reshape(n, d) + m_ref[0][5:6] * _gather_combine(dest_ref, y_hbm, tw_ref, ybuf, sem)
    o_ref[0] = _rms(h2, g_ref[...]).reshape(o_ref.shape[1:])


def _combine1(y_sorted, dest, top_w, h1, mod, final_g):
    bsz, n_rows, _, d = h1.shape
    nt = GRID_W // S5_COLS
    n_tok = n_rows * S5_COLS
    lat = pl.BlockSpec((1, n_rows, S5_COLS, d), lambda b, i: (b, 0, i, 0))
    return pl.pallas_call(
        _combine1_kernel,
        grid=(bsz, nt),
        in_specs=[pl.BlockSpec((1, TOP_K, n_tok), lambda b, i: (b * nt + i, 0, 0),
                               memory_space=pltpu.SMEM),
                  pl.BlockSpec(memory_space=pl.ANY),
                  pl.BlockSpec((1, n_tok, TOP_K), lambda b, i: (b * nt + i, 0, 0)),
                  lat, pl.BlockSpec((1, 6, d), lambda b, i: (2 * b + 1, 0, 0)),
                  pl.BlockSpec((1, d), lambda b, i: (0, 0))],
        out_specs=lat,
        out_shape=jax.ShapeDtypeStruct(h1.shape, F32),
        scratch_shapes=[pltpu.VMEM((TOP_K, n_tok, d), F32), pltpu.SemaphoreType.DMA(())],
        compiler_params=_params("arbitrary", "arbitrary"),
        name="combine1",
    )(dest, y_sorted, top_w, h1, mod, final_g.reshape(1, d))


def kernel(x, c, ctx, c_ctx, mod_w, mod_b, norm1_g, norm2_g, hg_w_in, hg_lb_logits, hg_norm_g,
           hg_w_out, s5_a_re, s5_a_im, s5_log_dt, s5_b_re, s5_b_im, s5_c_re, s5_c_im, s5_d, s5_w_glu,
           s5_b_glu, router_w, router_b, moe_w1, moe_b1, moe_w2, moe_b2, final_g):
    bsz, seq, d = x.shape
    n_ctx = ctx.shape[1]
    assert mod_w.shape[0] == 2 and n_ctx % TOK_TILE == 0 and seq % TOK_TILE == 0
    n_ctx_tiles = n_ctx // TOK_TILE
    lb_all = jnp.cumsum(jax.nn.softmax(hg_lb_logits.astype(F32), axis=0), axis=0)
    mod = _modulation(c, c_ctx, mod_w, mod_b)
    h = jnp.concatenate([ctx, x], axis=1)

    qs, v, sg, kf, kb, lff, lfb = _hg_proj(h, mod[0], norm1_g[0], hg_w_in[0], lb_all[0], n_ctx_tiles)
    o_f, o_b = _gla(qs, v, kf, kb, lff, lfb, n_ctx // HG_CHUNK)
    h1, f, top_w, meta, counts = _hg_readout(o_f, o_b, sg, h, mod[0], hg_norm_g[0], hg_w_out[0],
                                             norm2_g[0], router_w[0], router_b[0], n_ctx_tiles)
    y_sorted, dest = _moe(f, meta, counts, moe_w1[0], moe_b1[0], moe_w2[0], moe_b2[0])
    h_lat, a_lat, a_ctx = _combine0(y_sorted, dest, top_w, h1, mod[0], mod[1], norm1_g[1], n_ctx_tiles)

    mats = _s5_matrices(s5_a_re[0], s5_a_im[0], s5_log_dt[0], s5_b_re[0], s5_b_im[0],
                        s5_c_re[0], s5_c_im[0])
    y_s5 = _s5_scan(a_lat, a_ctx, mats)
    h1, f, top_w, meta, counts = _s5_glu(y_s5, a_lat, h_lat, mod[1], s5_d[0], s5_w_glu[0], s5_b_glu[0],
                                         norm2_g[1], router_w[1], router_b[1])
    y_sorted, dest = _moe(f, meta, counts, moe_w1[1], moe_b1[1], moe_w2[1], moe_b2[1])
    out = _combine1(y_sorted, dest, top_w, h1, mod[1], final_g)
    return out.reshape(bsz, seq, d)
```

```python
import functools

import jax
import jax.numpy as jnp
from jax import lax
from jax.experimental import pallas as pl
from jax.experimental.pallas import tpu as pltpu

F32 = jnp.float32
BF16 = jnp.bfloat16
HIGHEST = lax.Precision.HIGHEST

RMS_EPS = 1e-6
GRID_W = 64
HEAD_DIM = 128
HG_CHUNK = 64
S5_GROUP = 16
S5_STATE = 64
S5_EIG_MAX = -1e-4
S5_CHUNK = 8
S5_COLS = 8
LANES = 128
N_EXPERTS = 32
TOP_K = 4
SWIGLU_ALPHA = 1.702
SWIGLU_LIMIT = 7.0
TOK_TILE = 256
MOE_ROWS = 256
VMEM_LIMIT = 56 * 1024 * 1024


def _params(*sem):
    return pltpu.CompilerParams(dimension_semantics=sem, vmem_limit_bytes=VMEM_LIMIT)


def _rms(x, g):
    return x * lax.rsqrt(jnp.mean(x * x, axis=-1, keepdims=True) + RMS_EPS) * g


def _sigmoid(x):
    return 1.0 / (1.0 + jnp.exp(-x))


def _silu(x):
    return x * _sigmoid(x)


def _to_token_tiles(ref, x):
    n, d = x.shape
    for c in range(d // LANES):
        ref[pl.ds(c, n, stride=d // LANES), :] = x[:, c * LANES:(c + 1) * LANES]


def _from_token_tiles(ref, n, d):
    return jnp.concatenate([ref[pl.ds(c, n, stride=d // LANES), :] for c in range(d // LANES)], axis=-1)


def _mod_kernel(c_ref, w_ref, b_ref, o_ref):
    s = _silu(c_ref[...])
    o_ref[0] = jnp.dot(s, w_ref[0], precision=HIGHEST, preferred_element_type=F32) + b_ref[0]


def _modulation(c, c_ctx, mod_w, mod_b):
    bsz, d = c.shape
    depth = mod_w.shape[0]
    rows = jnp.concatenate([c, c_ctx[None], jnp.zeros((16 - bsz - 1, d), F32)], axis=0)
    bn = 6 * d // 4
    out = pl.pallas_call(
        _mod_kernel,
        grid=(depth, 4),
        in_specs=[pl.BlockSpec((16, d), lambda l, j: (0, 0)),
                  pl.BlockSpec((1, d, bn), lambda l, j: (l, 0, j)),
                  pl.BlockSpec((1, 1, bn), lambda l, j: (l, 0, j))],
        out_specs=pl.BlockSpec((1, 16, bn), lambda l, j: (l, 0, j)),
        out_shape=jax.ShapeDtypeStruct((depth, 16, 6 * d), F32),
        compiler_params=_params("parallel", "parallel"),
        name="modulation",
    )(rows, mod_w, mod_b.reshape(depth, 1, 6 * d))
    m_lat = out[:, :bsz].reshape(depth, bsz, 1, 6, d)
    m_ctx = jnp.broadcast_to(out[:, bsz].reshape(depth, 1, 1, 6, d), m_lat.shape)
    return jnp.concatenate([m_ctx, m_lat], axis=2).reshape(depth, 2 * bsz, 6, d)


def _hg_proj_kernel(h_ref, m_ref, g_ref, w_ref, lb_ref,
                    qs_ref, v_ref, sg_ref, kf_ref, kb_ref, lff_ref, lfb_ref):
    d = h_ref.shape[-1]
    m = m_ref[0]
    a = _rms(h_ref[0], g_ref[...]) * (1.0 + m[1:2]) + m[0:1]
    p = jnp.dot(a.astype(BF16), w_ref[...], preferred_element_type=F32)
    qs_ref[0] = _silu(p[:, 0:d]).astype(BF16)
    v_ref[0] = p[:, d:2 * d].astype(BF16)
    sg_ref[0] = _silu(p[:, 4 * d:5 * d]).astype(BF16)
    for di, (k_ref, lf_ref) in enumerate(((kf_ref, lff_ref), (kb_ref, lfb_ref))):
        lb = lb_ref[di:di + 1]
        f = lb + (1.0 - lb) * _sigmoid(p[:, (2 + di) * d:(3 + di) * d])
        k_ref[0] = (1.0 - f).astype(BF16)
        lf_ref[0] = jnp.log(f)


def _mod_spec(d, n_ctx_tiles):
    return pl.BlockSpec((1, 6, d), lambda b, i: (2 * b + (i >= n_ctx_tiles).astype(jnp.int32), 0, 0))


def _hg_proj(h, mod, g1, w_in, lb, n_ctx_tiles):
    bsz, t, d = h.shape
    tok = pl.BlockSpec((1, TOK_TILE, d), lambda b, i: (b, i, 0))
    bf = jax.ShapeDtypeStruct((bsz, t, d), BF16)
    ff = jax.ShapeDtypeStruct((bsz, t, d), F32)
    return pl.pallas_call(
        _hg_proj_kernel,
        grid=(bsz, t // TOK_TILE),
        in_specs=[tok, _mod_spec(d, n_ctx_tiles),
                  pl.BlockSpec((1, d), lambda b, i: (0, 0)),
                  pl.BlockSpec((d, 5 * d), lambda b, i: (0, 0)),
                  pl.BlockSpec((2, d), lambda b, i: (0, 0))],
        out_specs=[tok] * 7,
        out_shape=[bf, bf, bf, bf, bf, ff, ff],
        compiler_params=_params("parallel", "parallel"),
        name="hg_proj",
    )(h, mod, g1.reshape(1, d), w_in.astype(BF16), lb)


def _split3(x):
    hi = x.astype(BF16)
    r = x - hi.astype(F32)
    mid = r.astype(BF16)
    lo = (r - mid.astype(F32)).astype(BF16)
    return hi, mid, lo


def _gla_kernel(qf_ref, vf_ref, kf_ref, lf_ref, qb_ref, vb_ref, kb_ref, lb_ref,
                of_ref, ob_ref, sf_ref, sb_ref):
    c = HG_CHUNK
    n_heads = sf_ref.shape[0]

    @pl.when(pl.program_id(1) == 0)
    def _():
        sf_ref[...] = jnp.zeros_like(sf_ref)
        sb_ref[...] = jnp.zeros_like(sb_ref)

    row = lax.broadcasted_iota(jnp.int32, (c, c), 0)
    col = lax.broadcasted_iota(jnp.int32, (c, c), 1)
    dirs = ((qf_ref, vf_ref, kf_ref, lf_ref, of_ref, sf_ref, col <= row, c // 2 - 1, c - 1),
            (qb_ref, vb_ref, kb_ref, lb_ref, ob_ref, sb_ref, col >= row, c // 2, 0))
    for q_ref, v_ref, k_ref, l_ref, o_ref, s_ref, keep, r_ref, r_last in dirs:
        tri = keep.astype(BF16)
        b = sum(jnp.dot(tri, part, preferred_element_type=F32) for part in _split3(l_ref[0]))
        b_ref = b[r_ref:r_ref + 1]
        b_last = b[r_last:r_last + 1]
        a_in = q_ref[0].astype(F32) * jnp.exp(b - b_ref)
        k_in = k_ref[0].astype(F32) * jnp.exp(b_ref - b)
        q_st = (a_in * jnp.exp(b_ref)).astype(BF16)
        k_st = (k_in * jnp.exp(b_last - b_ref)).astype(BF16)
        a_in = a_in.astype(BF16)
        k_in = k_in.astype(BF16)
        decay = jnp.exp(b_last)
        v = v_ref[0]
        for h in range(n_heads):
            sl = slice(h * HEAD_DIM, (h + 1) * HEAD_DIM)
            sc = lax.dot_general(a_in[:, sl], k_in[:, sl], (((1,), (1,)), ((), ())),
                                 preferred_element_type=F32)
            sc = jnp.where(keep, sc, 0.0).astype(BF16)
            o = jnp.dot(sc, v[:, sl], preferred_element_type=F32)
            st = s_ref[h]
            o = o + lax.dot_general(q_st[:, sl], st.astype(BF16), (((1,), (1,)), ((), ())),
                                    preferred_element_type=F32)
            s_ref[h] = st * decay[:, sl] + lax.dot_general(
                v[:, sl], k_st[:, sl], (((0,), (0,)), ((), ())), preferred_element_type=F32)
            o_ref[0, :, sl] = o


def _gla(qs, v, kf, kb, lff, lfb, n_ctx_chunks):
    bsz, t, d = qs.shape
    n = t // HG_CHUNK
    n_heads = d // HEAD_DIM

    def rev(j):
        return jnp.where(j < n_ctx_chunks, n_ctx_chunks - 1 - j, n + n_ctx_chunks - 1 - j)

    fwd = pl.BlockSpec((1, HG_CHUNK, d), lambda b, j: (b, j, 0))
    bwd = pl.BlockSpec((1, HG_CHUNK, d), lambda b, j: (b, rev(j), 0))
    out = jax.ShapeDtypeStruct((bsz, t, d), F32)
    state = pltpu.VMEM((n_heads, HEAD_DIM, HEAD_DIM), F32)
    return pl.pallas_call(
        _gla_kernel,
        grid=(bsz, n),
        in_specs=[fwd, fwd, fwd, fwd, bwd, bwd, bwd, bwd],
        out_specs=[fwd, bwd],
        out_shape=[out, out],
        scratch_shapes=[state, state],
        compiler_params=_params("parallel", "arbitrary"),
        name="gla",
    )(qs, v, kf, lff, qs, v, kb, lfb)


def _mixer_tail(first, h, y, m, g2, rw, rb, run_ref, h1_ref, f_ref, tw_ref, meta_ref, cnt_ref):
    h1 = h + m[2:3] * y
    f = _rms(h1, g2) * (1.0 + m[4:5]) + m[3:4]
    h1_ref[...] = h1.reshape(h1_ref.shape)
    _to_token_tiles(f_ref, f)
    logits = jnp.dot(f, rw, precision=HIGHEST, preferred_element_type=F32) + rb
    n, n_e = logits.shape
    lane = lax.broadcasted_iota(jnp.int32, logits.shape, 1).astype(F32)
    vals, idxs, hots = [], [], []
    for _ in range(TOP_K):
        mx = jnp.max(logits, axis=-1, keepdims=True)
        ix = jnp.min(jnp.where(logits == mx, lane, float(n_e)), axis=-1, keepdims=True)
        hot = lane == ix
        vals.append(mx)
        idxs.append(ix)
        hots.append(hot)
        logits = jnp.where(hot, -jnp.inf, logits)
    es = [jnp.exp(x - vals[0]) for x in vals]
    tot = sum(es)
    for k in range(TOP_K):
        tw_ref[:, k:k + 1] = es[k] / tot

    @pl.when(first)
    def _():
        run_ref[...] = jnp.zeros_like(run_ref)

    picked = sum(hot.astype(F32) for hot in hots)
    r_i = lax.broadcasted_iota(jnp.int32, (n, n), 0)
    c_i = lax.broadcasted_iota(jnp.int32, (n, n), 1)
    earlier = jnp.dot((c_i < r_i).astype(BF16), picked.astype(BF16), preferred_element_type=F32)
    rank = earlier + run_ref[...]
    total = run_ref[...] + jnp.sum(picked, axis=0, keepdims=True)
    run_ref[...] = total
    cnt_ref[...] = total
    col = lax.broadcasted_iota(jnp.int32, (n, LANES), 1)
    z = jnp.zeros((n, LANES), F32)
    for k in range(TOP_K):
        pos = jnp.sum(jnp.where(hots[k], rank, 0.0), axis=-1, keepdims=True)
        z = jnp.where(col == k, idxs[k], z)
        z = jnp.where(col == TOP_K + k, pos, z)
    meta_ref[0] = z.T[0:2 * TOP_K].astype(jnp.int32)


def _tail_outputs(n_tiles, tile, d, n_e, index):
    rt = d // LANES
    specs = [pl.BlockSpec((tile * rt, LANES), lambda b, i: (index(b, i), 0)),
             pl.BlockSpec((1, tile, TOP_K), lambda b, i: (index(b, i), 0, 0)),
             pl.BlockSpec((1, 2 * TOP_K, tile), lambda b, i: (index(b, i), 0, 0)),
             pl.BlockSpec((1, n_e), lambda b, i: (0, 0))]
    shapes = [jax.ShapeDtypeStruct((n_tiles * tile * rt, LANES), F32),
              jax.ShapeDtypeStruct((n_tiles, tile, TOP_K), F32),
              jax.ShapeDtypeStruct((n_tiles, 2 * TOP_K, tile), jnp.int32),
              jax.ShapeDtypeStruct((1, n_e), F32)]
    return specs, shapes


def _hg_readout_kernel(of_ref, ob_ref, sg_ref, h_ref, m_ref, ng_ref, w_ref, g2_ref, rw_ref, rb_ref,
                       h1_ref, f_ref, tw_ref, meta_ref, cnt_ref, run_ref):
    d = h_ref.shape[-1]
    o = of_ref[0] + ob_ref[0]
    parts = []
    for h in range(d // HEAD_DIM):
        oh = o[:, h * HEAD_DIM:(h + 1) * HEAD_DIM]
        parts.append(oh * lax.rsqrt(jnp.mean(oh * oh, axis=-1, keepdims=True) + RMS_EPS))
    o = jnp.concatenate(parts, axis=-1) * ng_ref[...]
    y = jnp.dot((o * sg_ref[0].astype(F32)).astype(BF16), w_ref[...], preferred_element_type=F32)
    first = (pl.program_id(0) == 0) & (pl.program_id(1) == 0)
    _mixer_tail(first, h_ref[0], y, m_ref[0], g2_ref[...], rw_ref[...], rb_ref[...], run_ref,
                h1_ref, f_ref, tw_ref.at[0], meta_ref, cnt_ref)


def _hg_readout(o_f, o_b, sg, h, mod, norm_g, w_out, g2, rw, rb, n_ctx_tiles):
    bsz, t, d = h.shape
    n_e = rw.shape[-1]
    nt = t // TOK_TILE
    tok = pl.BlockSpec((1, TOK_TILE, d), lambda b, i: (b, i, 0))
    const = lambda shape: pl.BlockSpec(shape, lambda b, i: (0,) * len(shape))
    tail_specs, tail_shapes = _tail_outputs(bsz * nt, TOK_TILE, d, n_e, lambda b, i: b * nt + i)
    return pl.pallas_call(
        _hg_readout_kernel,
        grid=(bsz, nt),
        in_specs=[tok, tok, tok, tok, _mod_spec(d, n_ctx_tiles),
                  const((1, d)), const((d, d)), const((1, d)), const((d, n_e)), const((1, n_e))],
        out_specs=[tok] + tail_specs,
        out_shape=[jax.ShapeDtypeStruct((bsz, t, d), F32)] + tail_shapes,
        scratch_shapes=[pltpu.VMEM((1, n_e), F32)],
        compiler_params=_params("arbitrary", "arbitrary"),
        name="hg_readout",
    )(o_f, o_b, sg, h, mod, norm_g.reshape(1, d), w_out.astype(BF16), g2.reshape(1, d),
      rw, rb.reshape(1, n_e))


def _moe_plan(meta, counts, rows):
    n_tiles, _, tile = meta.shape
    n_asg = n_tiles * tile * TOP_K
    counts = counts.reshape(-1).astype(jnp.int32)
    padded = (counts + rows - 1) // rows * rows
    pad_end = jnp.cumsum(padded)
    pad_start = pad_end - padded
    n_blocks = -(-n_asg // rows) + N_EXPERTS
    e_ids = meta[:, :TOP_K]
    hot = e_ids[..., None] == jnp.arange(N_EXPERTS, dtype=jnp.int32)
    dest = meta[:, TOP_K:] + jnp.sum(jnp.where(hot, pad_start, 0), axis=-1)
    block_pos = jnp.arange(n_blocks, dtype=jnp.int32) * rows
    block_e = jnp.minimum(jnp.sum((block_pos[:, None] >= pad_end[None, :]).astype(jnp.int32), axis=1),
                          N_EXPERTS - 1)
    n_used = (pad_end[-1:] // rows).astype(jnp.int32)
    pad_lo = jnp.concatenate([pad_start + counts, pad_end[-1:]])
    pad_hi = jnp.concatenate([pad_end, jnp.full((1,), n_blocks * rows, jnp.int32)])
    return dest.astype(jnp.int32), block_e, n_used, pad_lo, pad_hi, n_blocks


def _dispatch_kernel(lo_ref, hi_ref, dest_ref, f_ref, x_hbm, zrow, sem, zsem):
    rt = zrow.shape[0]
    tile = f_ref.shape[0] // rt

    def slot(r):
        return x_hbm.at[pl.ds(pl.multiple_of(r * rt, rt), rt)]

    @pl.when(pl.program_id(0) == 0)
    def _():
        zrow[...] = jnp.zeros_like(zrow)

        def per_range(e, total):
            def fill(r, carry):
                pltpu.make_async_copy(zrow, slot(r), zsem).start()
                return carry
            lax.fori_loop(lo_ref[e], hi_ref[e], fill, 0)
            return total + hi_ref[e] - lo_ref[e]
        total = lax.fori_loop(0, lo_ref.shape[0], per_range, 0)

        def drain(r, carry):
            pltpu.make_async_copy(zrow, slot(0), zsem).wait()
            return carry
        lax.fori_loop(0, total, drain, 0)

    for k in range(TOP_K):
        for r in range(tile):
            pltpu.make_async_copy(f_ref.at[pl.ds(r * rt, rt)], slot(dest_ref[0, k, r]), sem).start()
    for k in range(TOP_K):
        pltpu.make_async_copy(f_ref, x_hbm.at[pl.ds(0, tile * rt)], sem).wait()


def _dispatch(f_tiles, dest, pad_lo, pad_hi, n_slots):
    n_tiles, _, tile = dest.shape
    rt = f_tiles.shape[0] // (n_tiles * tile)
    grid_spec = pltpu.PrefetchScalarGridSpec(
        num_scalar_prefetch=2,
        grid=(n_tiles,),
        in_specs=[pl.BlockSpec((1, TOP_K, tile), lambda i, lo, hi: (i, 0, 0), memory_space=pltpu.SMEM),
                  pl.BlockSpec((tile * rt, LANES), lambda i, lo, hi: (i, 0))],
        out_specs=pl.BlockSpec(memory_space=pl.ANY),
        scratch_shapes=[pltpu.VMEM((rt, LANES), F32), pltpu.SemaphoreType.DMA(()),
                        pltpu.SemaphoreType.DMA(())])
    return pl.pallas_call(
        _dispatch_kernel,
        grid_spec=grid_spec,
        out_shape=jax.ShapeDtypeStruct((n_slots * rt, LANES), F32),
        compiler_params=_params("arbitrary"),
        name="moe_dispatch",
    )(pad_lo, pad_hi, dest, f_tiles)


def _ffn_kernel(be_ref, nu_ref, x_ref, w1_ref, b1_ref, w2_ref, b2_ref, y_ref, w1c, w2c):
    b = pl.program_id(0)
    d, f2 = w1c.shape
    rows = x_ref.shape[0] // (d // LANES)

    @pl.when(b >= nu_ref[0])
    def _():
        y_ref[...] = jnp.zeros_like(y_ref)

    @pl.when(b < nu_ref[0])
    def _():
        @pl.when((b == 0) | (be_ref[b] != be_ref[jnp.maximum(b - 1, 0)]))
        def _():
            w1c[...] = w1_ref[0, 0].astype(BF16)
            w2c[...] = w2_ref[0, 0].astype(BF16)

        x = _from_token_tiles(x_ref, rows, d).astype(BF16)
        z = jnp.dot(x, w1c[...], preferred_element_type=F32) + b1_ref[0, 0]
        z_glu = jnp.minimum(z[:, :f2 // 2], SWIGLU_LIMIT)
        z_lin = jnp.clip(z[:, f2 // 2:], -SWIGLU_LIMIT, SWIGLU_LIMIT)
        act = z_glu * _sigmoid(SWIGLU_ALPHA * z_glu) * (z_lin + 1.0)
        y = jnp.dot(act.astype(BF16), w2c[...], preferred_element_type=F32) + b2_ref[0, 0]
        _to_token_tiles(y_ref, y)


def _moe_ffn(x_sorted, block_e, n_used, layer, w1, b1, w2, b2):
    _, n_e, d, f2 = w1.shape
    rt = d // LANES
    n_blocks = x_sorted.shape[0] // (MOE_ROWS * rt)
    live = lambda b, nu: jnp.minimum(b, nu[0] - 1)
    per_e = lambda shape: pl.BlockSpec((1, 1) + shape, lambda b, be, nu: (layer, be[live(b, nu)], 0, 0))
    grid_spec = pltpu.PrefetchScalarGridSpec(
        num_scalar_prefetch=2,
        grid=(n_blocks,),
        in_specs=[pl.BlockSpec((MOE_ROWS * rt, LANES), lambda b, be, nu: (live(b, nu), 0)),
                  per_e((d, f2)), per_e((1, f2)), per_e((f2 // 2, d)), per_e((1, d))],
        out_specs=pl.BlockSpec((MOE_ROWS * rt, LANES), lambda b, be, nu: (b, 0)),
        scratch_shapes=[pltpu.VMEM((d, f2), BF16), pltpu.VMEM((f2 // 2, d), BF16)])
    return pl.pallas_call(
        _ffn_kernel,
        grid_spec=grid_spec,
        out_shape=jax.ShapeDtypeStruct(x_sorted.shape, F32),
        compiler_params=_params("arbitrary"),
        name="moe_ffn",
    )(block_e, n_used, x_sorted, w1, b1.reshape(b1.shape[0], n_e, 1, f2), w2,
      b2.reshape(b2.shape[0], n_e, 1, d))


def _moe(f_tiles, meta, counts, layer, w1, b1, w2, b2):
    dest, block_e, n_used, pad_lo, pad_hi, n_blocks = _moe_plan(meta, counts, MOE_ROWS)
    x_sorted = _dispatch(f_tiles, dest, pad_lo, pad_hi, n_blocks * MOE_ROWS)
    return _moe_ffn(x_sorted, block_e, n_used, layer, w1, b1, w2, b2), dest


def _gather_combine(dest_ref, y_hbm, tw_ref, ybuf, sem, d):
    rt = d // LANES
    tile = ybuf.shape[1] // rt
    for k in range(TOP_K):
        for r in range(tile):
            src = y_hbm.at[pl.ds(pl.multiple_of(dest_ref[0, k, r] * rt, rt), rt)]
            pltpu.make_async_copy(src, ybuf.at[k, pl.ds(r * rt, rt)], sem).start()
    for k in range(TOP_K):
        pltpu.make_async_copy(y_hbm.at[pl.ds(0, tile * rt)], ybuf.at[k], sem).wait()
    tw = tw_ref[0]
    return sum(tw[:, k:k + 1] * _from_token_tiles(ybuf.at[k], tile, d) for k in range(TOP_K))


def _combine0_kernel(dest_ref, y_hbm, tw_ref, h_ref, m0_ref, m1_ref, g_ref,
                     hl_ref, al_ref, ac_ref, ybuf, sem, *, n_ctx_tiles):
    h2 = h_ref[0] + m0_ref[0][5:6] * _gather_combine(dest_ref, y_hbm, tw_ref, ybuf, sem, h_ref.shape[-1])
    m1 = m1_ref[0]
    a = _rms(h2, g_ref[...]) * (1.0 + m1[1:2]) + m1[0:1]
    is_ctx = pl.program_id(1) < n_ctx_tiles

    @pl.when(is_ctx)
    def _():
        ac_ref[0] = a

    @pl.when(jnp.logical_not(is_ctx))
    def _():
        hl_ref[0] = h2
        al_ref[0] = a


def _combine0(y_sorted, dest, top_w, h1, mod0, mod1, g1_next, n_ctx_tiles):
    bsz, t, d = h1.shape
    nt = t // TOK_TILE
    n_ctx = n_ctx_tiles * TOK_TILE
    tok = pl.BlockSpec((1, TOK_TILE, d), lambda b, i: (b, i, 0))
    lat = pl.BlockSpec((1, TOK_TILE, d), lambda b, i: (b, jnp.maximum(i - n_ctx_tiles, 0), 0))
    ctx = pl.BlockSpec((1, TOK_TILE, d), lambda b, i: (b, jnp.minimum(i, n_ctx_tiles - 1), 0))
    return pl.pallas_call(
        functools.partial(_combine0_kernel, n_ctx_tiles=n_ctx_tiles),
        grid=(bsz, nt),
        in_specs=[pl.BlockSpec((1, TOP_K, TOK_TILE), lambda b, i: (b * nt + i, 0, 0),
                               memory_space=pltpu.SMEM),
                  pl.BlockSpec(memory_space=pl.ANY),
                  pl.BlockSpec((1, TOK_TILE, TOP_K), lambda b, i: (b * nt + i, 0, 0)),
                  tok, _mod_spec(d, n_ctx_tiles), _mod_spec(d, n_ctx_tiles),
                  pl.BlockSpec((1, d), lambda b, i: (0, 0))],
        out_specs=[lat, lat, ctx],
        out_shape=[jax.ShapeDtypeStruct((bsz, t - n_ctx, d), F32),
                   jax.ShapeDtypeStruct((bsz, t - n_ctx, d), F32),
                   jax.ShapeDtypeStruct((bsz, n_ctx, d), F32)],
        scratch_shapes=[pltpu.VMEM((TOP_K, TOK_TILE * d // LANES, LANES), F32), pltpu.SemaphoreType.DMA(())],
        compiler_params=_params("arbitrary", "arbitrary"),
        name="combine0",
    )(dest, y_sorted, top_w, h1, mod0, mod1, g1_next.reshape(1, d))


def _s5_matrices(a_re, a_im, log_dt, b_re, b_im, c_re, c_im):
    ng, p = a_re.shape[1:]
    gc = b_re.shape[-1]
    ll = S5_CHUNK
    gpb = LANES // gc
    nblk = ng // gpb
    lam_re = jnp.minimum(a_re, S5_EIG_MAX)
    lam_im = a_im
    dt = jnp.exp(log_dt)[..., None]
    j = jnp.arange(ll + 1, dtype=F32).reshape(-1, 1, 1, 1)
    mag = jnp.exp(j * (lam_re * dt))
    pw_re = mag * jnp.cos(j * (lam_im * dt))
    pw_im = mag * jnp.sin(j * (lam_im * dt))
    ab_re, ab_im = pw_re[1], pw_im[1]
    den = lam_re * lam_re + lam_im * lam_im
    coef_re = ((ab_re - 1.0) * lam_re + ab_im * lam_im) / den
    coef_im = (ab_im * lam_re - (ab_re - 1.0) * lam_im) / den
    bb_re = coef_re[..., None] * b_re - coef_im[..., None] * b_im
    bb_im = coef_re[..., None] * b_im + coef_im[..., None] * b_re
    drv_re = pw_re[..., None] * bb_re - pw_im[..., None] * bb_im
    drv_im = pw_re[..., None] * bb_im + pw_im[..., None] * bb_re
    rd_re = c_re * pw_re[:, :, :, None, :] - c_im * pw_im[:, :, :, None, :]
    rd_im = -(c_re * pw_im[:, :, :, None, :] + c_im * pw_re[:, :, :, None, :])
    taps = (jnp.einsum('dgop,jdgpi->jdgoi', c_re, drv_re[:ll], precision=HIGHEST)
            - jnp.einsum('dgop,jdgpi->jdgoi', c_im, drv_im[:ll], precision=HIGHEST))
    r = jnp.arange(ll)
    eye = jnp.eye(gpb, dtype=F32)
    m_in, m_drv, m_rd = [], [], []
    for d in range(2):
        lag = (r[None, :] - r[:, None]) if d == 0 else (r[:, None] - r[None, :])
        tp = jnp.where((lag >= 0)[:, :, None, None, None],
                       taps[:, d][jnp.clip(lag, 0, ll - 1)], 0.0)
        tp = tp.reshape(ll, ll, nblk, gpb, gc, gc)
        m = jnp.einsum('rsbgoi,gh->brgisho', tp, eye)
        m_in.append(m.reshape(nblk, ll * LANES, ll * LANES))
        steps = (ll - 1 - r) if d == 0 else r
        dr = jnp.stack([drv_re[:, d][steps], drv_im[:, d][steps]], axis=1)
        dr = dr.reshape(ll, 2, nblk, gpb, p, gc)
        m = jnp.einsum('rcbgpi,gh->brgichp', dr, eye)
        m_drv.append(m.reshape(nblk, ll * LANES, 2 * gpb * p))
        steps = (r + 1) if d == 0 else (ll - r)
        rd = jnp.stack([rd_re[:, d][steps], rd_im[:, d][steps]], axis=1)
        rd = rd.reshape(ll, 2, nblk, gpb, gc, p)
        m = jnp.einsum('rcbgop,gh->bcgprho', rd, eye)
        m_rd.append(m.reshape(nblk, 2 * gpb * p, ll * LANES))
    a8 = jnp.stack([pw_re[ll], pw_im[ll]], axis=1).reshape(2, 2, nblk, gpb * p).transpose(0, 2, 1, 3)
    stack = lambda ms: jnp.stack(ms).astype(BF16)
    return stack(m_in), stack(m_drv), stack(m_rd), a8


def _s5_kernel(uc_ref, ul_ref, min_ref, mdrv_ref, mrd_ref, a8_ref, y_ref, v_ref, st_ref, *, n_ctx_cols):
    bsz, n_oct, ll, n_col, lanes = ul_ref.shape
    n_rows = bsz * n_oct * n_col
    half = st_ref.shape[-1] // 2
    d = pl.program_id(0)
    k = pl.program_id(2)

    @pl.when(k == 0)
    def _():
        st_ref[...] = jnp.zeros_like(st_ref)

    def chunks(u_ref):
        return jnp.concatenate([u_ref[:, :, r].reshape(n_rows, lanes) for r in range(ll)], axis=-1)

    x = jnp.where(k == 0, chunks(uc_ref), chunks(ul_ref)).astype(BF16)
    inj_all = jnp.dot(x, mdrv_ref[0, 0], preferred_element_type=F32)
    n_pl = v_ref.shape[0]
    for c in range(n_pl):
        v_ref[c] = inj_all[:, c * lanes:(c + 1) * lanes]

    a_re = jnp.broadcast_to(a8_ref[0, 0, 0:1], (bsz, half))
    a_im = jnp.broadcast_to(a8_ref[0, 0, 1:2], (bsz, half))
    n_cols_here = jnp.where(k == 0, n_ctx_cols, n_col)
    n_steps = n_cols_here * n_oct

    def step(i, carry):
        s_re, s_im = carry
        i = jnp.where(d == 0, i, n_steps - 1 - i)
        row = (i % n_oct) * n_col + i // n_oct
        rows = pl.ds(row, bsz, stride=n_oct * n_col)
        inj = jnp.concatenate([v_ref[c, rows, :] for c in range(n_pl)], axis=-1)
        for c in range(n_pl // 2):
            v_ref[c, rows, :] = s_re[:, c * lanes:(c + 1) * lanes]
            v_ref[n_pl // 2 + c, rows, :] = s_im[:, c * lanes:(c + 1) * lanes]
        return (a_re * s_re - a_im * s_im + inj[:, :half],
                a_re * s_im + a_im * s_re + inj[:, half:])

    s_re, s_im = lax.fori_loop(0, n_steps, step, (st_ref[:, :half], st_ref[:, half:]))
    st_ref[:, :half] = s_re
    st_ref[:, half:] = s_im

    s_start = jnp.concatenate([v_ref[c] for c in range(n_pl)], axis=-1).astype(BF16)
    y = (jnp.dot(x, min_ref[0, 0], preferred_element_type=F32)
         + jnp.dot(s_start, mrd_ref[0, 0], preferred_element_type=F32))
    for r in range(ll):
        y_ref[0, :, :, r] = y[:, r * lanes:(r + 1) * lanes].reshape(bsz, n_oct, n_col, lanes)


def _s5_scan(a_lat, a_ctx, mats):
    bsz, t, d = a_lat.shape
    n_ctx = a_ctx.shape[1]
    m_in, m_drv, m_rd, a8 = mats
    n_rows = t // GRID_W
    n_oct = n_rows // S5_CHUNK
    n_ctx_cols = n_ctx // n_rows
    u_lat = a_lat.reshape(bsz, n_oct, S5_CHUNK, GRID_W, d)
    u_ctx = a_ctx.reshape(bsz, n_ctx_cols, n_rows, d).transpose(0, 2, 1, 3)
    u_ctx = jnp.pad(u_ctx, ((0, 0), (0, 0), (0, S5_COLS - n_ctx_cols), (0, 0)))
    u_ctx = u_ctx.reshape(bsz, n_oct, S5_CHUNK, S5_COLS, d)
    n_lat = GRID_W // S5_COLS
    nblk = d // LANES

    def lat_tile(dd, k):
        kk = jnp.maximum(k, 1) - 1
        return jnp.where(dd == 0, kk, n_lat - 1 - kk)

    blk = (bsz, n_oct, S5_CHUNK, S5_COLS, LANES)
    wspec = lambda shape: pl.BlockSpec((1, 1) + shape, lambda dd, j, k: (dd, j, 0, 0))
    kl = S5_CHUNK * LANES
    ns = m_drv.shape[-1]
    y = pl.pallas_call(
        functools.partial(_s5_kernel, n_ctx_cols=n_ctx_cols),
        grid=(2, nblk, n_lat + 1),
        in_specs=[pl.BlockSpec(blk, lambda dd, j, k: (0, 0, 0, 0, j)),
                  pl.BlockSpec(blk, lambda dd, j, k: (0, 0, 0, lat_tile(dd, k), j)),
                  wspec((kl, kl)), wspec((kl, ns)), wspec((ns, kl)),
                  pl.BlockSpec((1, 1, 2, ns // 2), lambda dd, j, k: (dd, j, 0, 0))],
        out_specs=pl.BlockSpec((1,) + blk, lambda dd, j, k: (dd, 0, 0, 0, lat_tile(dd, k), j)),
        out_shape=jax.ShapeDtypeStruct((2,) + u_lat.shape, F32),
        scratch_shapes=[pltpu.VMEM((ns // LANES, bsz * n_oct * S5_COLS, LANES), F32),
                        pltpu.VMEM((bsz, ns), F32)],
        compiler_params=_params("parallel", "parallel", "arbitrary"),
        name="s5_scan",
    )(u_ctx, u_lat, m_in, m_drv, m_rd, a8)
    return y.reshape(2, bsz, n_rows, GRID_W, d)


def _s5_glu_kernel(y_ref, u_ref, h_ref, m_ref, dsk_ref, w_ref, bg_ref, g2_ref, rw_ref, rb_ref,
                   h1_ref, f_ref, tw_ref, meta_ref, cnt_ref, run_ref):
    d = h_ref.shape[-1]
    n = h_ref.shape[1] * h_ref.shape[2]
    y = (y_ref[0, 0] + y_ref[1, 0] + dsk_ref[...] * u_ref[0]).reshape(n, d)
    z = jnp.dot(jax.nn.gelu(y).astype(BF16), w_ref[...], preferred_element_type=F32) + bg_ref[...]
    y = z[:, :d] * _sigmoid(z[:, d:])
    first = (pl.program_id(0) == 0) & (pl.program_id(1) == 0)
    _mixer_tail(first, h_ref[0].reshape(n, d), y, m_ref[0], g2_ref[...], rw_ref[...], rb_ref[...],
                run_ref, h1_ref, f_ref, tw_ref.at[0], meta_ref, cnt_ref)


def _s5_glu(y, a_lat, h_lat, mod, d_skip, w_glu, b_glu, g2, rw, rb):
    bsz, t, d = h_lat.shape
    n_rows = t // GRID_W
    n_e = rw.shape[-1]
    n_tok = n_rows * S5_COLS
    nt = GRID_W // S5_COLS
    lat = pl.BlockSpec((1, n_rows, S5_COLS, d), lambda b, i: (b, 0, i, 0))
    tail_specs, tail_shapes = _tail_outputs(bsz * nt, n_tok, d, n_e, lambda b, i: b * nt + i)
    const = lambda shape: pl.BlockSpec(shape, lambda b, i: (0,) * len(shape))
    return pl.pallas_call(
        _s5_glu_kernel,
        grid=(bsz, nt),
        in_specs=[pl.BlockSpec((2, 1, n_rows, S5_COLS, d), lambda b, i: (0, b, 0, i, 0)),
                  lat, lat, pl.BlockSpec((1, 6, d), lambda b, i: (2 * b + 1, 0, 0)),
                  const((1, d)), const((d, 2 * d)), const((1, 2 * d)), const((1, d)),
                  const((d, n_e)), const((1, n_e))],
        out_specs=[lat] + tail_specs,
        out_shape=[jax.ShapeDtypeStruct((bsz, n_rows, GRID_W, d), F32)] + tail_shapes,
        scratch_shapes=[pltpu.VMEM((1, n_e), F32)],
        compiler_params=_params("arbitrary", "arbitrary"),
        name="s5_glu",
    )(y, a_lat.reshape(bsz, n_rows, GRID_W, d), h_lat.reshape(bsz, n_rows, GRID_W, d), mod,
      d_skip.reshape(1, d), w_glu.astype(BF16), b_glu.reshape(1, 2 * d),
      g2.reshape(1, d), rw, rb.reshape(1, n_e))


def _combine1_kernel(dest_ref, y_hbm, tw_ref, h_ref, m_ref, g_ref, o_ref, ybuf, sem):
    d = h_ref.shape[-1]
    n = h_ref.shape[1] * h_ref.shape[2]
    h2 = h_ref[0].reshape(n, d) + m_ref[0][5:6] * _gather_combine(dest_ref, y_hbm, tw_ref, ybuf, sem, d)
    o_ref[0] = _rms(h2, g_ref[...]).reshape(o_ref.shape[1:])


def _combine1(y_sorted, dest, top_w, h1, mod, final_g):
    bsz, n_rows, _, d = h1.shape
    nt = GRID_W // S5_COLS
    n_tok = n_rows * S5_COLS
    lat = pl.BlockSpec((1, n_rows, S5_COLS, d), lambda b, i: (b, 0, i, 0))
    return pl.pallas_call(
        _combine1_kernel,
        grid=(bsz, nt),
        in_specs=[pl.BlockSpec((1, TOP_K, n_tok), lambda b, i: (b * nt + i, 0, 0),
                               memory_space=pltpu.SMEM),
                  pl.BlockSpec(memory_space=pl.ANY),
                  pl.BlockSpec((1, n_tok, TOP_K), lambda b, i: (b * nt + i, 0, 0)),
                  lat, pl.BlockSpec((1, 6, d), lambda b, i: (2 * b + 1, 0, 0)),
                  pl.BlockSpec((1, d), lambda b, i: (0, 0))],
        out_specs=lat,
        out_shape=jax.ShapeDtypeStruct(h1.shape, F32),
        scratch_shapes=[pltpu.VMEM((TOP_K, n_tok * d // LANES, LANES), F32), pltpu.SemaphoreType.DMA(())],
        compiler_params=_params("arbitrary", "arbitrary"),
        name="combine1",
    )(dest, y_sorted, top_w, h1, mod, final_g.reshape(1, d))


def kernel(x, c, ctx, c_ctx, mod_w, mod_b, norm1_g, norm2_g, hg_w_in, hg_lb_logits, hg_norm_g,
           hg_w_out, s5_a_re, s5_a_im, s5_log_dt, s5_b_re, s5_b_im, s5_c_re, s5_c_im, s5_d, s5_w_glu,
           s5_b_glu, router_w, router_b, moe_w1, moe_b1, moe_w2, moe_b2, final_g):
    bsz, seq, d = x.shape
    n_ctx = ctx.shape[1]
    assert mod_w.shape[0] == 2 and n_ctx % TOK_TILE == 0 and seq % TOK_TILE == 0
    n_ctx_tiles = n_ctx // TOK_TILE
    lb_all = jnp.cumsum(jax.nn.softmax(hg_lb_logits.astype(F32), axis=0), axis=0)
    mod = _modulation(c, c_ctx, mod_w, mod_b)
    h = jnp.concatenate([ctx, x], axis=1)

    qs, v, sg, kf, kb, lff, lfb = _hg_proj(h, mod[0], norm1_g[0], hg_w_in[0], lb_all[0], n_ctx_tiles)
    o_f, o_b = _gla(qs, v, kf, kb, lff, lfb, n_ctx // HG_CHUNK)
    h1, f, top_w, meta, counts = _hg_readout(o_f, o_b, sg, h, mod[0], hg_norm_g[0], hg_w_out[0],
                                             norm2_g[0], router_w[0], router_b[0], n_ctx_tiles)
    y_sorted, dest = _moe(f, meta, counts, 0, moe_w1, moe_b1, moe_w2, moe_b2)
    h_lat, a_lat, a_ctx = _combine0(y_sorted, dest, top_w, h1, mod[0], mod[1], norm1_g[1], n_ctx_tiles)

    mats = _s5_matrices(s5_a_re[0], s5_a_im[0], s5_log_dt[0], s5_b_re[0], s5_b_im[0],
                        s5_c_re[0], s5_c_im[0])
    y_s5 = _s5_scan(a_lat, a_ctx, mats)
    h1, f, top_w, meta, counts = _s5_glu(y_s5, a_lat, h_lat, mod[1], s5_d[0], s5_w_glu[0], s5_b_glu[0],
                                         norm2_g[1], router_w[1], router_b[1])
    y_sorted, dest = _moe(f, meta, counts, 1, moe_w1, moe_b1, moe_w2, moe_b2)
    out = _combine1(y_sorted, dest, top_w, h1, mod[1], final_g)
    return out.reshape(bsz, seq, d)
```

```python
import functools

import jax
import jax.numpy as jnp
from jax import lax
from jax.experimental import pallas as pl
from jax.experimental.pallas import tpu as pltpu

F32 = jnp.float32
BF16 = jnp.bfloat16
HIGHEST = lax.Precision.HIGHEST

RMS_EPS = 1e-6
GRID_W = 64
HEAD_DIM = 128
HG_CHUNK = 64
HG_BATCH = 2
S5_GROUP = 16
S5_STATE = 64
S5_EIG_MAX = -1e-4
S5_CHUNK = 8
S5_COLS = 8
LANES = 128
N_EXPERTS = 32
TOP_K = 4
SWIGLU_ALPHA = 1.702
SWIGLU_LIMIT = 7.0
TOK_TILE = 256
MOE_ROWS = 256
VMEM_LIMIT = 56 * 1024 * 1024


def _params(*sem):
    return pltpu.CompilerParams(dimension_semantics=sem, vmem_limit_bytes=VMEM_LIMIT)


def _rms(x, g):
    return x * lax.rsqrt(jnp.mean(x * x, axis=-1, keepdims=True) + RMS_EPS) * g


def _sigmoid(x):
    return 1.0 / (1.0 + jnp.exp(-x))


def _silu(x):
    return x * _sigmoid(x)


def _to_token_tiles(ref, x):
    n, d = x.shape
    for c in range(d // LANES):
        ref[pl.ds(c, n, stride=d // LANES), :] = x[:, c * LANES:(c + 1) * LANES]


def _from_token_tiles(ref, n, d):
    return jnp.concatenate([ref[pl.ds(c, n, stride=d // LANES), :] for c in range(d // LANES)], axis=-1)


def _mod_kernel(c_ref, w_ref, b_ref, o_ref):
    s = _silu(c_ref[...])
    o_ref[0] = jnp.dot(s, w_ref[0], precision=HIGHEST, preferred_element_type=F32) + b_ref[0]


def _modulation(c, c_ctx, mod_w, mod_b):
    bsz, d = c.shape
    depth = mod_w.shape[0]
    rows = jnp.concatenate([c, c_ctx[None], jnp.zeros((16 - bsz - 1, d), F32)], axis=0)
    bn = 6 * d // 4
    out = pl.pallas_call(
        _mod_kernel,
        grid=(depth, 4),
        in_specs=[pl.BlockSpec((16, d), lambda l, j: (0, 0)),
                  pl.BlockSpec((1, d, bn), lambda l, j: (l, 0, j)),
                  pl.BlockSpec((1, 1, bn), lambda l, j: (l, 0, j))],
        out_specs=pl.BlockSpec((1, 16, bn), lambda l, j: (l, 0, j)),
        out_shape=jax.ShapeDtypeStruct((depth, 16, 6 * d), F32),
        compiler_params=_params("parallel", "parallel"),
        name="modulation",
    )(rows, mod_w, mod_b.reshape(depth, 1, 6 * d))
    m_lat = out[:, :bsz].reshape(depth, bsz, 1, 6, d)
    m_ctx = jnp.broadcast_to(out[:, bsz].reshape(depth, 1, 1, 6, d), m_lat.shape)
    return jnp.concatenate([m_ctx, m_lat], axis=2).reshape(depth, 2 * bsz, 6, d)


def _hg_proj_kernel(h_ref, m_ref, g_ref, w_ref, lb_ref,
                    qs_ref, v_ref, sg_ref, kf_ref, kb_ref, lff_ref, lfb_ref):
    d = h_ref.shape[-1]
    m = m_ref[0]
    a = _rms(h_ref[0], g_ref[...]) * (1.0 + m[1:2]) + m[0:1]
    p = jnp.dot(a.astype(BF16), w_ref[...], preferred_element_type=F32)
    qs_ref[0] = _silu(p[:, 0:d]).astype(BF16)
    v_ref[0] = p[:, d:2 * d].astype(BF16)
    sg_ref[0] = _silu(p[:, 4 * d:5 * d]).astype(BF16)
    for di, (k_ref, lf_ref) in enumerate(((kf_ref, lff_ref), (kb_ref, lfb_ref))):
        lb = lb_ref[di:di + 1]
        f = lb + (1.0 - lb) * _sigmoid(p[:, (2 + di) * d:(3 + di) * d])
        k_ref[0] = (1.0 - f).astype(BF16)
        lf_ref[0] = jnp.log(f)


def _mod_spec(d, n_ctx_tiles):
    return pl.BlockSpec((1, 6, d), lambda b, i: (2 * b + (i >= n_ctx_tiles).astype(jnp.int32), 0, 0))


def _hg_proj(h, mod, g1, w_in, lb, n_ctx_tiles):
    bsz, t, d = h.shape
    tok = pl.BlockSpec((1, TOK_TILE, d), lambda b, i: (b, i, 0))
    bf = jax.ShapeDtypeStruct((bsz, t, d), BF16)
    ff = jax.ShapeDtypeStruct((bsz, t, d), F32)
    return pl.pallas_call(
        _hg_proj_kernel,
        grid=(bsz, t // TOK_TILE),
        in_specs=[tok, _mod_spec(d, n_ctx_tiles),
                  pl.BlockSpec((1, d), lambda b, i: (0, 0)),
                  pl.BlockSpec((d, 5 * d), lambda b, i: (0, 0)),
                  pl.BlockSpec((2, d), lambda b, i: (0, 0))],
        out_specs=[tok] * 7,
        out_shape=[bf, bf, bf, bf, bf, ff, ff],
        compiler_params=_params("parallel", "parallel"),
        name="hg_proj",
    )(h, mod, g1.reshape(1, d), w_in.astype(BF16), lb)


def _split3(x):
    hi = x.astype(BF16)
    r = x - hi.astype(F32)
    mid = r.astype(BF16)
    lo = (r - mid.astype(F32)).astype(BF16)
    return hi, mid, lo


def _gla_kernel(qf_ref, vf_ref, kf_ref, lf_ref, qb_ref, vb_ref, kb_ref, lb_ref,
                of_ref, ob_ref, sf_ref, sb_ref):
    c = HG_CHUNK
    n_b = qf_ref.shape[0]
    n_heads = sf_ref.shape[0] // n_b

    @pl.when(pl.program_id(1) == 0)
    def _():
        sf_ref[...] = jnp.zeros_like(sf_ref)
        sb_ref[...] = jnp.zeros_like(sb_ref)

    row = lax.broadcasted_iota(jnp.int32, (c, c), 0)
    col = lax.broadcasted_iota(jnp.int32, (c, c), 1)
    dirs = ((qf_ref, vf_ref, kf_ref, lf_ref, of_ref, sf_ref, col <= row, c // 2 - 1, c - 1),
            (qb_ref, vb_ref, kb_ref, lb_ref, ob_ref, sb_ref, col >= row, c // 2, 0))
    for bb, (q_ref, v_ref, k_ref, l_ref, o_ref, s_ref, keep, r_ref, r_last) in (
            (bb, dr) for bb in range(n_b) for dr in dirs):
        tri = keep.astype(BF16)
        b = sum(jnp.dot(tri, part, preferred_element_type=F32) for part in _split3(l_ref[bb]))
        b_ref = b[r_ref:r_ref + 1]
        b_last = b[r_last:r_last + 1]
        a_in = q_ref[bb].astype(F32) * jnp.exp(b - b_ref)
        k_in = k_ref[bb].astype(F32) * jnp.exp(b_ref - b)
        q_st = (a_in * jnp.exp(b_ref)).astype(BF16)
        k_st = (k_in * jnp.exp(b_last - b_ref)).astype(BF16)
        a_in = a_in.astype(BF16)
        k_in = k_in.astype(BF16)
        decay = jnp.exp(b_last)
        v = v_ref[bb]
        for h in range(n_heads):
            sl = slice(h * HEAD_DIM, (h + 1) * HEAD_DIM)
            sc = lax.dot_general(a_in[:, sl], k_in[:, sl], (((1,), (1,)), ((), ())),
                                 preferred_element_type=F32)
            sc = jnp.where(keep, sc, 0.0).astype(BF16)
            o = jnp.dot(sc, v[:, sl], preferred_element_type=F32)
            st = s_ref[bb * n_heads + h]
            o = o + lax.dot_general(q_st[:, sl], st.astype(BF16), (((1,), (1,)), ((), ())),
                                    preferred_element_type=F32)
            s_ref[bb * n_heads + h] = st * decay[:, sl] + lax.dot_general(
                v[:, sl], k_st[:, sl], (((0,), (0,)), ((), ())), preferred_element_type=F32)
            o_ref[bb, :, sl] = o


def _gla(qs, v, kf, kb, lff, lfb, n_ctx_chunks):
    bsz, t, d = qs.shape
    n = t // HG_CHUNK
    n_heads = d // HEAD_DIM

    def rev(j):
        return jnp.where(j < n_ctx_chunks, n_ctx_chunks - 1 - j, n + n_ctx_chunks - 1 - j)

    fwd = pl.BlockSpec((HG_BATCH, HG_CHUNK, d), lambda b, j: (b, j, 0))
    bwd = pl.BlockSpec((HG_BATCH, HG_CHUNK, d), lambda b, j: (b, rev(j), 0))
    out = jax.ShapeDtypeStruct((bsz, t, d), F32)
    state = pltpu.VMEM((HG_BATCH * n_heads, HEAD_DIM, HEAD_DIM), F32)
    return pl.pallas_call(
        _gla_kernel,
        grid=(bsz // HG_BATCH, n),
        in_specs=[fwd, fwd, fwd, fwd, bwd, bwd, bwd, bwd],
        out_specs=[fwd, bwd],
        out_shape=[out, out],
        scratch_shapes=[state, state],
        compiler_params=_params("parallel", "arbitrary"),
        name="gla",
    )(qs, v, kf, lff, qs, v, kb, lfb)


def _mixer_tail(first, h, y, m, g2, rw, rb, run_ref, h1_ref, f_ref, tw_ref, meta_ref, cnt_ref):
    h1 = h + m[2:3] * y
    f = _rms(h1, g2) * (1.0 + m[4:5]) + m[3:4]
    h1_ref[...] = h1.reshape(h1_ref.shape)
    _to_token_tiles(f_ref, f)
    logits = jnp.dot(f, rw, precision=HIGHEST, preferred_element_type=F32) + rb
    n, n_e = logits.shape
    lane = lax.broadcasted_iota(jnp.int32, logits.shape, 1).astype(F32)
    vals, idxs, hots = [], [], []
    for _ in range(TOP_K):
        mx = jnp.max(logits, axis=-1, keepdims=True)
        ix = jnp.min(jnp.where(logits == mx, lane, float(n_e)), axis=-1, keepdims=True)
        hot = lane == ix
        vals.append(mx)
        idxs.append(ix)
        hots.append(hot)
        logits = jnp.where(hot, -jnp.inf, logits)
    es = [jnp.exp(x - vals[0]) for x in vals]
    tot = sum(es)
    for k in range(TOP_K):
        tw_ref[:, k:k + 1] = es[k] / tot

    @pl.when(first)
    def _():
        run_ref[...] = jnp.zeros_like(run_ref)

    picked = sum(hot.astype(F32) for hot in hots)
    r_i = lax.broadcasted_iota(jnp.int32, (n, n), 0)
    c_i = lax.broadcasted_iota(jnp.int32, (n, n), 1)
    earlier = jnp.dot((c_i < r_i).astype(BF16), picked.astype(BF16), preferred_element_type=F32)
    rank = earlier + run_ref[...]
    total = run_ref[...] + jnp.sum(picked, axis=0, keepdims=True)
    run_ref[...] = total
    cnt_ref[...] = total
    col = lax.broadcasted_iota(jnp.int32, (n, LANES), 1)
    z = jnp.zeros((n, LANES), F32)
    for k in range(TOP_K):
        pos = jnp.sum(jnp.where(hots[k], rank, 0.0), axis=-1, keepdims=True)
        z = jnp.where(col == k, idxs[k], z)
        z = jnp.where(col == TOP_K + k, pos, z)
    meta_ref[0] = z.T[0:2 * TOP_K].astype(jnp.int32)


def _tail_outputs(n_tiles, tile, d, n_e, index):
    rt = d // LANES
    specs = [pl.BlockSpec((tile * rt, LANES), lambda b, i: (index(b, i), 0)),
             pl.BlockSpec((1, tile, TOP_K), lambda b, i: (index(b, i), 0, 0)),
             pl.BlockSpec((1, 2 * TOP_K, tile), lambda b, i: (index(b, i), 0, 0)),
             pl.BlockSpec((1, n_e), lambda b, i: (0, 0))]
    shapes = [jax.ShapeDtypeStruct((n_tiles * tile * rt, LANES), F32),
              jax.ShapeDtypeStruct((n_tiles, tile, TOP_K), F32),
              jax.ShapeDtypeStruct((n_tiles, 2 * TOP_K, tile), jnp.int32),
              jax.ShapeDtypeStruct((1, n_e), F32)]
    return specs, shapes


def _hg_readout_kernel(of_ref, ob_ref, sg_ref, h_ref, m_ref, ng_ref, w_ref, g2_ref, rw_ref, rb_ref,
                       h1_ref, f_ref, tw_ref, meta_ref, cnt_ref, run_ref):
    d = h_ref.shape[-1]
    o = of_ref[0] + ob_ref[0]
    parts = []
    for h in range(d // HEAD_DIM):
        oh = o[:, h * HEAD_DIM:(h + 1) * HEAD_DIM]
        parts.append(oh * lax.rsqrt(jnp.mean(oh * oh, axis=-1, keepdims=True) + RMS_EPS))
    o = jnp.concatenate(parts, axis=-1) * ng_ref[...]
    y = jnp.dot((o * sg_ref[0].astype(F32)).astype(BF16), w_ref[...], preferred_element_type=F32)
    first = (pl.program_id(0) == 0) & (pl.program_id(1) == 0)
    _mixer_tail(first, h_ref[0], y, m_ref[0], g2_ref[...], rw_ref[...], rb_ref[...], run_ref,
                h1_ref, f_ref, tw_ref.at[0], meta_ref, cnt_ref)


def _hg_readout(o_f, o_b, sg, h, mod, norm_g, w_out, g2, rw, rb, n_ctx_tiles):
    bsz, t, d = h.shape
    n_e = rw.shape[-1]
    nt = t // TOK_TILE
    tok = pl.BlockSpec((1, TOK_TILE, d), lambda b, i: (b, i, 0))
    const = lambda shape: pl.BlockSpec(shape, lambda b, i: (0,) * len(shape))
    tail_specs, tail_shapes = _tail_outputs(bsz * nt, TOK_TILE, d, n_e, lambda b, i: b * nt + i)
    return pl.pallas_call(
        _hg_readout_kernel,
        grid=(bsz, nt),
        in_specs=[tok, tok, tok, tok, _mod_spec(d, n_ctx_tiles),
                  const((1, d)), const((d, d)), const((1, d)), const((d, n_e)), const((1, n_e))],
        out_specs=[tok] + tail_specs,
        out_shape=[jax.ShapeDtypeStruct((bsz, t, d), F32)] + tail_shapes,
        scratch_shapes=[pltpu.VMEM((1, n_e), F32)],
        compiler_params=_params("arbitrary", "arbitrary"),
        name="hg_readout",
    )(o_f, o_b, sg, h, mod, norm_g.reshape(1, d), w_out.astype(BF16), g2.reshape(1, d),
      rw, rb.reshape(1, n_e))


def _moe_plan(meta, counts, rows):
    n_tiles, _, tile = meta.shape
    n_asg = n_tiles * tile * TOP_K
    counts = counts.reshape(-1).astype(jnp.int32)
    padded = (counts + rows - 1) // rows * rows
    pad_end = jnp.cumsum(padded)
    pad_start = pad_end - padded
    n_blocks = -(-n_asg // rows) + N_EXPERTS
    e_ids = meta[:, :TOP_K]
    hot = e_ids[..., None] == jnp.arange(N_EXPERTS, dtype=jnp.int32)
    dest = meta[:, TOP_K:] + jnp.sum(jnp.where(hot, pad_start, 0), axis=-1)
    block_pos = jnp.arange(n_blocks, dtype=jnp.int32) * rows
    block_e = jnp.minimum(jnp.sum((block_pos[:, None] >= pad_end[None, :]).astype(jnp.int32), axis=1),
                          N_EXPERTS - 1)
    n_used = (pad_end[-1:] // rows).astype(jnp.int32)
    pad_lo = jnp.concatenate([pad_start + counts, pad_end[-1:]])
    pad_hi = jnp.concatenate([pad_end, jnp.full((1,), n_blocks * rows, jnp.int32)])
    return dest.astype(jnp.int32), block_e, n_used, pad_lo, pad_hi, n_blocks


def _dispatch_kernel(lo_ref, hi_ref, dest_ref, f_ref, x_hbm, zrow, sem, zsem):
    rt = zrow.shape[0]
    tile = f_ref.shape[0] // rt

    def slot(r):
        return x_hbm.at[pl.ds(pl.multiple_of(r * rt, rt), rt)]

    @pl.when(pl.program_id(0) == 0)
    def _():
        zrow[...] = jnp.zeros_like(zrow)

        def per_range(e, total):
            def fill(r, carry):
                pltpu.make_async_copy(zrow, slot(r), zsem).start()
                return carry
            lax.fori_loop(lo_ref[e], hi_ref[e], fill, 0)
            return total + hi_ref[e] - lo_ref[e]
        total = lax.fori_loop(0, lo_ref.shape[0], per_range, 0)

        def drain(r, carry):
            pltpu.make_async_copy(zrow, slot(0), zsem).wait()
            return carry
        lax.fori_loop(0, total, drain, 0)

    for k in range(TOP_K):
        for r in range(tile):
            pltpu.make_async_copy(f_ref.at[pl.ds(r * rt, rt)], slot(dest_ref[0, k, r]), sem).start()
    for k in range(TOP_K):
        pltpu.make_async_copy(f_ref, x_hbm.at[pl.ds(0, tile * rt)], sem).wait()


def _dispatch(f_tiles, dest, pad_lo, pad_hi, n_slots):
    n_tiles, _, tile = dest.shape
    rt = f_tiles.shape[0] // (n_tiles * tile)
    grid_spec = pltpu.PrefetchScalarGridSpec(
        num_scalar_prefetch=2,
        grid=(n_tiles,),
        in_specs=[pl.BlockSpec((1, TOP_K, tile), lambda i, lo, hi: (i, 0, 0), memory_space=pltpu.SMEM),
                  pl.BlockSpec((tile * rt, LANES), lambda i, lo, hi: (i, 0))],
        out_specs=pl.BlockSpec(memory_space=pl.ANY),
        scratch_shapes=[pltpu.VMEM((rt, LANES), F32), pltpu.SemaphoreType.DMA(()),
                        pltpu.SemaphoreType.DMA(())])
    return pl.pallas_call(
        _dispatch_kernel,
        grid_spec=grid_spec,
        out_shape=jax.ShapeDtypeStruct((n_slots * rt, LANES), F32),
        compiler_params=_params("arbitrary"),
        name="moe_dispatch",
    )(pad_lo, pad_hi, dest, f_tiles)


def _ffn_kernel(be_ref, nu_ref, x_ref, w1_ref, b1_ref, w2_ref, b2_ref, y_ref, w1c, w2c):
    b = pl.program_id(0)
    d, f2 = w1c.shape
    rows = x_ref.shape[0] // (d // LANES)

    @pl.when(b >= nu_ref[0])
    def _():
        y_ref[...] = jnp.zeros_like(y_ref)

    @pl.when(b < nu_ref[0])
    def _():
        @pl.when((b == 0) | (be_ref[b] != be_ref[jnp.maximum(b - 1, 0)]))
        def _():
            w1c[...] = w1_ref[0, 0].astype(BF16)
            w2c[...] = w2_ref[0, 0].astype(BF16)

        x = _from_token_tiles(x_ref, rows, d).astype(BF16)
        z = jnp.dot(x, w1c[...], preferred_element_type=F32) + b1_ref[0, 0]
        z_glu = jnp.minimum(z[:, :f2 // 2], SWIGLU_LIMIT)
        z_lin = jnp.clip(z[:, f2 // 2:], -SWIGLU_LIMIT, SWIGLU_LIMIT)
        act = z_glu * _sigmoid(SWIGLU_ALPHA * z_glu) * (z_lin + 1.0)
        y = jnp.dot(act.astype(BF16), w2c[...], preferred_element_type=F32) + b2_ref[0, 0]
        _to_token_tiles(y_ref, y)


def _moe_ffn(x_sorted, block_e, n_used, layer, w1, b1, w2, b2):
    _, n_e, d, f2 = w1.shape
    rt = d // LANES
    n_blocks = x_sorted.shape[0] // (MOE_ROWS * rt)
    live = lambda b, nu: jnp.minimum(b, nu[0] - 1)
    per_e = lambda shape: pl.BlockSpec((1, 1) + shape, lambda b, be, nu: (layer, be[live(b, nu)], 0, 0))
    grid_spec = pltpu.PrefetchScalarGridSpec(
        num_scalar_prefetch=2,
        grid=(n_blocks,),
        in_specs=[pl.BlockSpec((MOE_ROWS * rt, LANES), lambda b, be, nu: (live(b, nu), 0)),
                  per_e((d, f2)), per_e((1, f2)), per_e((f2 // 2, d)), per_e((1, d))],
        out_specs=pl.BlockSpec((MOE_ROWS * rt, LANES), lambda b, be, nu: (b, 0)),
        scratch_shapes=[pltpu.VMEM((d, f2), BF16), pltpu.VMEM((f2 // 2, d), BF16)])
    return pl.pallas_call(
        _ffn_kernel,
        grid_spec=grid_spec,
        out_shape=jax.ShapeDtypeStruct(x_sorted.shape, F32),
        compiler_params=_params("arbitrary"),
        name="moe_ffn",
    )(block_e, n_used, x_sorted, w1, b1.reshape(b1.shape[0], n_e, 1, f2), w2,
      b2.reshape(b2.shape[0], n_e, 1, d))


def _moe(f_tiles, meta, counts, layer, w1, b1, w2, b2):
    dest, block_e, n_used, pad_lo, pad_hi, n_blocks = _moe_plan(meta, counts, MOE_ROWS)
    x_sorted = _dispatch(f_tiles, dest, pad_lo, pad_hi, n_blocks * MOE_ROWS)
    return _moe_ffn(x_sorted, block_e, n_used, layer, w1, b1, w2, b2), dest


def _gather_combine(dest_ref, y_hbm, tw_ref, ybuf, sem, d):
    rt = d // LANES
    tile = ybuf.shape[1] // rt
    for k in range(TOP_K):
        for r in range(tile):
            src = y_hbm.at[pl.ds(pl.multiple_of(dest_ref[0, k, r] * rt, rt), rt)]
            pltpu.make_async_copy(src, ybuf.at[k, pl.ds(r * rt, rt)], sem).start()
    for k in range(TOP_K):
        pltpu.make_async_copy(y_hbm.at[pl.ds(0, tile * rt)], ybuf.at[k], sem).wait()
    tw = tw_ref[0]
    return sum(tw[:, k:k + 1] * _from_token_tiles(ybuf.at[k], tile, d) for k in range(TOP_K))


def _combine0_kernel(dest_ref, y_hbm, tw_ref, h_ref, m0_ref, m1_ref, g_ref,
                     hl_ref, al_ref, ac_ref, ybuf, sem, *, n_ctx_tiles):
    h2 = h_ref[0] + m0_ref[0][5:6] * _gather_combine(dest_ref, y_hbm, tw_ref, ybuf, sem, h_ref.shape[-1])
    m1 = m1_ref[0]
    a = _rms(h2, g_ref[...]) * (1.0 + m1[1:2]) + m1[0:1]
    is_ctx = pl.program_id(1) < n_ctx_tiles

    @pl.when(is_ctx)
    def _():
        ac_ref[0] = a

    @pl.when(jnp.logical_not(is_ctx))
    def _():
        hl_ref[0] = h2
        al_ref[0] = a


def _combine0(y_sorted, dest, top_w, h1, mod0, mod1, g1_next, n_ctx_tiles):
    bsz, t, d = h1.shape
    nt = t // TOK_TILE
    n_ctx = n_ctx_tiles * TOK_TILE
    tok = pl.BlockSpec((1, TOK_TILE, d), lambda b, i: (b, i, 0))
    lat = pl.BlockSpec((1, TOK_TILE, d), lambda b, i: (b, jnp.maximum(i - n_ctx_tiles, 0), 0))
    ctx = pl.BlockSpec((1, TOK_TILE, d), lambda b, i: (b, jnp.minimum(i, n_ctx_tiles - 1), 0))
    return pl.pallas_call(
        functools.partial(_combine0_kernel, n_ctx_tiles=n_ctx_tiles),
        grid=(bsz, nt),
        in_specs=[pl.BlockSpec((1, TOP_K, TOK_TILE), lambda b, i: (b * nt + i, 0, 0),
                               memory_space=pltpu.SMEM),
                  pl.BlockSpec(memory_space=pl.ANY),
                  pl.BlockSpec((1, TOK_TILE, TOP_K), lambda b, i: (b * nt + i, 0, 0)),
                  tok, _mod_spec(d, n_ctx_tiles), _mod_spec(d, n_ctx_tiles),
                  pl.BlockSpec((1, d), lambda b, i: (0, 0))],
        out_specs=[lat, lat, ctx],
        out_shape=[jax.ShapeDtypeStruct((bsz, t - n_ctx, d), F32),
                   jax.ShapeDtypeStruct((bsz, t - n_ctx, d), F32),
                   jax.ShapeDtypeStruct((bsz, n_ctx, d), F32)],
        scratch_shapes=[pltpu.VMEM((TOP_K, TOK_TILE * d // LANES, LANES), F32), pltpu.SemaphoreType.DMA(())],
        compiler_params=_params("arbitrary", "arbitrary"),
        name="combine0",
    )(dest, y_sorted, top_w, h1, mod0, mod1, g1_next.reshape(1, d))


def _s5_matrices(a_re, a_im, log_dt, b_re, b_im, c_re, c_im):
    ng, p = a_re.shape[1:]
    gc = b_re.shape[-1]
    ll = S5_CHUNK
    gpb = LANES // gc
    nblk = ng // gpb
    lam_re = jnp.minimum(a_re, S5_EIG_MAX)
    lam_im = a_im
    dt = jnp.exp(log_dt)[..., None]
    j = jnp.arange(ll + 1, dtype=F32).reshape(-1, 1, 1, 1)
    mag = jnp.exp(j * (lam_re * dt))
    pw_re = mag * jnp.cos(j * (lam_im * dt))
    pw_im = mag * jnp.sin(j * (lam_im * dt))
    ab_re, ab_im = pw_re[1], pw_im[1]
    den = lam_re * lam_re + lam_im * lam_im
    coef_re = ((ab_re - 1.0) * lam_re + ab_im * lam_im) / den
    coef_im = (ab_im * lam_re - (ab_re - 1.0) * lam_im) / den
    bb_re = coef_re[..., None] * b_re - coef_im[..., None] * b_im
    bb_im = coef_re[..., None] * b_im + coef_im[..., None] * b_re
    drv_re = pw_re[..., None] * bb_re - pw_im[..., None] * bb_im
    drv_im = pw_re[..., None] * bb_im + pw_im[..., None] * bb_re
    rd_re = c_re * pw_re[:, :, :, None, :] - c_im * pw_im[:, :, :, None, :]
    rd_im = -(c_re * pw_im[:, :, :, None, :] + c_im * pw_re[:, :, :, None, :])
    taps = (jnp.einsum('dgop,jdgpi->jdgoi', c_re, drv_re[:ll], precision=HIGHEST)
            - jnp.einsum('dgop,jdgpi->jdgoi', c_im, drv_im[:ll], precision=HIGHEST))
    r = jnp.arange(ll)
    c_in, c_drv, c_rd = [], [], []
    for d in range(2):
        lag = (r[None, :] - r[:, None]) if d == 0 else (r[:, None] - r[None, :])
        tp = jnp.where((lag >= 0)[:, :, None, None, None],
                       taps[:, d][jnp.clip(lag, 0, ll - 1)], 0.0)
        tp = tp.reshape(ll, ll, nblk, gpb, gc, gc).transpose(2, 0, 3, 5, 1, 4)
        c_in.append(tp.reshape(nblk, ll * LANES, ll * gc))
        steps = (ll - 1 - r) if d == 0 else r
        dr = jnp.stack([drv_re[:, d][steps], drv_im[:, d][steps]], axis=1)
        dr = dr.reshape(ll, 2, nblk, gpb, p, gc).transpose(2, 0, 3, 5, 1, 4)
        c_drv.append(dr.reshape(nblk, ll * LANES, 2 * p))
        steps = (r + 1) if d == 0 else (ll - r)
        rd = jnp.stack([rd_re[:, d][steps], rd_im[:, d][steps]], axis=1)
        rd = rd.reshape(ll, 2, nblk, gpb, gc, p).transpose(2, 1, 3, 5, 0, 4)
        c_rd.append(rd.reshape(nblk, 2 * gpb * p, ll * gc))

    def expand(compact, row_unit, col_unit):
        compact = jnp.stack(compact).astype(BF16)
        n_r, n_c = compact.shape[2], compact.shape[3] * gpb
        col = jnp.arange(n_c)
        src = (col // (col_unit * gpb)) * col_unit + col % col_unit
        spread = (jnp.arange(n_c // gpb)[:, None] == src[None, :]).astype(BF16)
        full = jnp.einsum('dbrk,kc->dbrc', compact, spread, preferred_element_type=F32)
        same = ((jnp.arange(n_r) // row_unit) % gpb)[:, None] == ((col // col_unit) % gpb)[None, :]
        return jnp.where(same, full, 0.0).astype(BF16)

    a8 = jnp.stack([pw_re[ll], pw_im[ll]], axis=1).reshape(2, 2, nblk, gpb * p).transpose(0, 2, 1, 3)
    return expand(c_in, gc, gc), expand(c_drv, gc, p), expand(c_rd, p, gc), a8


def _s5_kernel(uc_ref, ul_ref, min_ref, mdrv_ref, mrd_ref, a8_ref, y_ref, v_ref, st_ref, *, n_ctx_cols):
    bsz, n_oct, ll, n_col, lanes = ul_ref.shape
    n_rows = bsz * n_oct * n_col
    half = st_ref.shape[-1] // 2
    d = pl.program_id(0)
    k = pl.program_id(2)

    @pl.when(k == 0)
    def _():
        st_ref[...] = jnp.zeros_like(st_ref)

    def chunks(u_ref):
        return jnp.concatenate([u_ref[:, :, r].reshape(n_rows, lanes) for r in range(ll)], axis=-1)

    x = jnp.where(k == 0, chunks(uc_ref), chunks(ul_ref)).astype(BF16)
    inj_all = jnp.dot(x, mdrv_ref[0, 0], preferred_element_type=F32)
    n_pl = v_ref.shape[0]
    for c in range(n_pl):
        v_ref[c] = inj_all[:, c * lanes:(c + 1) * lanes]

    a_re = jnp.broadcast_to(a8_ref[0, 0, 0:1], (bsz, half))
    a_im = jnp.broadcast_to(a8_ref[0, 0, 1:2], (bsz, half))
    def scan(n_steps):
        def step(i, carry):
            s_re, s_im = carry
            i = jnp.where(d == 0, i, n_steps - 1 - i)
            row = (i % n_oct) * n_col + i // n_oct
            rows = pl.ds(row, bsz, stride=n_oct * n_col)
            inj = jnp.concatenate([v_ref[c, rows, :] for c in range(n_pl)], axis=-1)
            for c in range(n_pl // 2):
                v_ref[c, rows, :] = s_re[:, c * lanes:(c + 1) * lanes]
                v_ref[n_pl // 2 + c, rows, :] = s_im[:, c * lanes:(c + 1) * lanes]
            return (a_re * s_re - a_im * s_im + inj[:, :half],
                    a_re * s_im + a_im * s_re + inj[:, half:])

        s_re, s_im = lax.fori_loop(0, n_steps, step, (st_ref[:, :half], st_ref[:, half:]), unroll=2)
        st_ref[:, :half] = s_re
        st_ref[:, half:] = s_im

    @pl.when(k == 0)
    def _():
        scan(n_ctx_cols * n_oct)

    @pl.when(k > 0)
    def _():
        scan(n_col * n_oct)

    s_start = jnp.concatenate([v_ref[c] for c in range(n_pl)], axis=-1).astype(BF16)
    y = (jnp.dot(x, min_ref[0, 0], preferred_element_type=F32)
         + jnp.dot(s_start, mrd_ref[0, 0], preferred_element_type=F32))
    for r in range(ll):
        y_ref[0, :, :, r] = y[:, r * lanes:(r + 1) * lanes].reshape(bsz, n_oct, n_col, lanes)


def _s5_scan(a_lat, a_ctx, mats):
    bsz, t, d = a_lat.shape
    n_ctx = a_ctx.shape[1]
    m_in, m_drv, m_rd, a8 = mats
    n_rows = t // GRID_W
    n_oct = n_rows // S5_CHUNK
    n_ctx_cols = n_ctx // n_rows
    u_lat = a_lat.reshape(bsz, n_oct, S5_CHUNK, GRID_W, d)
    u_ctx = a_ctx.reshape(bsz, n_ctx_cols, n_rows, d).transpose(0, 2, 1, 3)
    u_ctx = jnp.pad(u_ctx, ((0, 0), (0, 0), (0, S5_COLS - n_ctx_cols), (0, 0)))
    u_ctx = u_ctx.reshape(bsz, n_oct, S5_CHUNK, S5_COLS, d)
    n_lat = GRID_W // S5_COLS
    nblk = d // LANES

    def lat_tile(dd, k):
        kk = jnp.maximum(k, 1) - 1
        return jnp.where(dd == 0, kk, n_lat - 1 - kk)

    blk = (bsz, n_oct, S5_CHUNK, S5_COLS, LANES)
    wspec = lambda shape: pl.BlockSpec((1, 1) + shape, lambda dd, j, k: (dd, j, 0, 0))
    kl = S5_CHUNK * LANES
    ns = m_drv.shape[-1]
    y = pl.pallas_call(
        functools.partial(_s5_kernel, n_ctx_cols=n_ctx_cols),
        grid=(2, nblk, n_lat + 1),
        in_specs=[pl.BlockSpec(blk, lambda dd, j, k: (0, 0, 0, 0, j)),
                  pl.BlockSpec(blk, lambda dd, j, k: (0, 0, 0, lat_tile(dd, k), j)),
                  wspec((kl, kl)), wspec((kl, ns)), wspec((ns, kl)),
                  pl.BlockSpec((1, 1, 2, ns // 2), lambda dd, j, k: (dd, j, 0, 0))],
        out_specs=pl.BlockSpec((1,) + blk, lambda dd, j, k: (dd, 0, 0, 0, lat_tile(dd, k), j)),
        out_shape=jax.ShapeDtypeStruct((2,) + u_lat.shape, F32),
        scratch_shapes=[pltpu.VMEM((ns // LANES, bsz * n_oct * S5_COLS, LANES), F32),
                        pltpu.VMEM((bsz, ns), F32)],
        compiler_params=_params("parallel", "parallel", "arbitrary"),
        name="s5_scan",
    )(u_ctx, u_lat, m_in, m_drv, m_rd, a8)
    return y.reshape(2, bsz, n_rows, GRID_W, d)


def _s5_glu_kernel(y_ref, u_ref, h_ref, m_ref, dsk_ref, w_ref, bg_ref, g2_ref, rw_ref, rb_ref,
                   h1_ref, f_ref, tw_ref, meta_ref, cnt_ref, run_ref):
    d = h_ref.shape[-1]
    n = h_ref.shape[1] * h_ref.shape[2]
    y = (y_ref[0, 0] + y_ref[1, 0] + dsk_ref[...] * u_ref[0]).reshape(n, d)
    z = jnp.dot(jax.nn.gelu(y).astype(BF16), w_ref[...], preferred_element_type=F32) + bg_ref[...]
    y = z[:, :d] * _sigmoid(z[:, d:])
    first = (pl.program_id(0) == 0) & (pl.program_id(1) == 0)
    _mixer_tail(first, h_ref[0].reshape(n, d), y, m_ref[0], g2_ref[...], rw_ref[...], rb_ref[...],
                run_ref, h1_ref, f_ref, tw_ref.at[0], meta_ref, cnt_ref)


def _s5_glu(y, a_lat, h_lat, mod, d_skip, w_glu, b_glu, g2, rw, rb):
    bsz, t, d = h_lat.shape
    n_rows = t // GRID_W
    n_e = rw.shape[-1]
    n_tok = n_rows * S5_COLS
    nt = GRID_W // S5_COLS
    lat = pl.BlockSpec((1, n_rows, S5_COLS, d), lambda b, i: (b, 0, i, 0))
    tail_specs, tail_shapes = _tail_outputs(bsz * nt, n_tok, d, n_e, lambda b, i: b * nt + i)
    const = lambda shape: pl.BlockSpec(shape, lambda b, i: (0,) * len(shape))
    return pl.pallas_call(
        _s5_glu_kernel,
        grid=(bsz, nt),
        in_specs=[pl.BlockSpec((2, 1, n_rows, S5_COLS, d), lambda b, i: (0, b, 0, i, 0)),
                  lat, lat, pl.BlockSpec((1, 6, d), lambda b, i: (2 * b + 1, 0, 0)),
                  const((1, d)), const((d, 2 * d)), const((1, 2 * d)), const((1, d)),
                  const((d, n_e)), const((1, n_e))],
        out_specs=[lat] + tail_specs,
        out_shape=[jax.ShapeDtypeStruct((bsz, n_rows, GRID_W, d), F32)] + tail_shapes,
        scratch_shapes=[pltpu.VMEM((1, n_e), F32)],
        compiler_params=_params("arbitrary", "arbitrary"),
        name="s5_glu",
    )(y, a_lat.reshape(bsz, n_rows, GRID_W, d), h_lat.reshape(bsz, n_rows, GRID_W, d), mod,
      d_skip.reshape(1, d), w_glu.astype(BF16), b_glu.reshape(1, 2 * d),
      g2.reshape(1, d), rw, rb.reshape(1, n_e))


def _combine1_kernel(dest_ref, y_hbm, tw_ref, h_ref, m_ref, g_ref, o_ref, ybuf, sem):
    d = h_ref.shape[-1]
    n = h_ref.shape[1] * h_ref.shape[2]
    h2 = h_ref[0].reshape(n, d) + m_ref[0][5:6] * _gather_combine(dest_ref, y_hbm, tw_ref, ybuf, sem, d)
    o_ref[0] = _rms(h2, g_ref[...]).reshape(o_ref.shape[1:])


def _combine1(y_sorted, dest, top_w, h1, mod, final_g):
    bsz, n_rows, _, d = h1.shape
    nt = GRID_W // S5_COLS
    n_tok = n_rows * S5_COLS
    lat = pl.BlockSpec((1, n_rows, S5_COLS, d), lambda b, i: (b, 0, i, 0))
    return pl.pallas_call(
        _combine1_kernel,
        grid=(bsz, nt),
        in_specs=[pl.BlockSpec((1, TOP_K, n_tok), lambda b, i: (b * nt + i, 0, 0),
                               memory_space=pltpu.SMEM),
                  pl.BlockSpec(memory_space=pl.ANY),
                  pl.BlockSpec((1, n_tok, TOP_K), lambda b, i: (b * nt + i, 0, 0)),
                  lat, pl.BlockSpec((1, 6, d), lambda b, i: (2 * b + 1, 0, 0)),
                  pl.BlockSpec((1, d), lambda b, i: (0, 0))],
        out_specs=lat,
        out_shape=jax.ShapeDtypeStruct(h1.shape, F32),
        scratch_shapes=[pltpu.VMEM((TOP_K, n_tok * d // LANES, LANES), F32), pltpu.SemaphoreType.DMA(())],
        compiler_params=_params("arbitrary", "arbitrary"),
        name="combine1",
    )(dest, y_sorted, top_w, h1, mod, final_g.reshape(1, d))


def kernel(x, c, ctx, c_ctx, mod_w, mod_b, norm1_g, norm2_g, hg_w_in, hg_lb_logits, hg_norm_g,
           hg_w_out, s5_a_re, s5_a_im, s5_log_dt, s5_b_re, s5_b_im, s5_c_re, s5_c_im, s5_d, s5_w_glu,
           s5_b_glu, router_w, router_b, moe_w1, moe_b1, moe_w2, moe_b2, final_g):
    bsz, seq, d = x.shape
    n_ctx = ctx.shape[1]
    assert mod_w.shape[0] == 2 and n_ctx % TOK_TILE == 0 and seq % TOK_TILE == 0
    n_ctx_tiles = n_ctx // TOK_TILE
    lb_all = jnp.cumsum(jax.nn.softmax(hg_lb_logits.astype(F32), axis=0), axis=0)
    mod = _modulation(c, c_ctx, mod_w, mod_b)
    h = jnp.concatenate([ctx, x], axis=1)

    qs, v, sg, kf, kb, lff, lfb = _hg_proj(h, mod[0], norm1_g[0], hg_w_in[0], lb_all[0], n_ctx_tiles)
    o_f, o_b = _gla(qs, v, kf, kb, lff, lfb, n_ctx // HG_CHUNK)
    h1, f, top_w, meta, counts = _hg_readout(o_f, o_b, sg, h, mod[0], hg_norm_g[0], hg_w_out[0],
                                             norm2_g[0], router_w[0], router_b[0], n_ctx_tiles)
    y_sorted, dest = _moe(f, meta, counts, 0, moe_w1, moe_b1, moe_w2, moe_b2)
    h_lat, a_lat, a_ctx = _combine0(y_sorted, dest, top_w, h1, mod[0], mod[1], norm1_g[1], n_ctx_tiles)

    mats = _s5_matrices(s5_a_re[0], s5_a_im[0], s5_log_dt[0], s5_b_re[0], s5_b_im[0],
                        s5_c_re[0], s5_c_im[0])
    y_s5 = _s5_scan(a_lat, a_ctx, mats)
    h1, f, top_w, meta, counts = _s5_glu(y_s5, a_lat, h_lat, mod[1], s5_d[0], s5_w_glu[0], s5_b_glu[0],
                                         norm2_g[1], router_w[1], router_b[1])
    y_sorted, dest = _moe(f, meta, counts, 1, moe_w1, moe_b1, moe_w2, moe_b2)
    out = _combine1(y_sorted, dest, top_w, h1, mod[1], final_g)
    return out.reshape(bsz, seq, d)
```

```python
import functools

import jax
import jax.numpy as jnp
from jax import lax
from jax.experimental import pallas as pl
from jax.experimental.pallas import tpu as pltpu

F32 = jnp.float32
BF16 = jnp.bfloat16
HIGHEST = lax.Precision.HIGHEST

RMS_EPS = 1e-6
GRID_W = 64
HEAD_DIM = 128
HG_CHUNK = 64
HG_BATCH = 2
S5_GROUP = 16
S5_STATE = 64
S5_EIG_MAX = -1e-4
S5_CHUNK = 8
S5_COLS = 8
LANES = 128
N_EXPERTS = 32
TOP_K = 4
SWIGLU_ALPHA = 1.702
SWIGLU_LIMIT = 7.0
TOK_TILE = 256
MOE_ROWS = 256
VMEM_LIMIT = 56 * 1024 * 1024


def _params(*sem):
    return pltpu.CompilerParams(dimension_semantics=sem, vmem_limit_bytes=VMEM_LIMIT)


def _rms(x, g):
    return x * lax.rsqrt(jnp.mean(x * x, axis=-1, keepdims=True) + RMS_EPS) * g


def _sigmoid(x):
    return 1.0 / (1.0 + jnp.exp(-x))


def _silu(x):
    return x * _sigmoid(x)


def _to_token_tiles(ref, x):
    n, d = x.shape
    for c in range(d // LANES):
        ref[pl.ds(c, n, stride=d // LANES), :] = x[:, c * LANES:(c + 1) * LANES]


def _from_token_tiles(ref, n, d):
    return jnp.concatenate([ref[pl.ds(c, n, stride=d // LANES), :] for c in range(d // LANES)], axis=-1)


def _mod_kernel(c_ref, w_ref, b_ref, o_ref):
    s = _silu(c_ref[...])
    o_ref[0] = jnp.dot(s, w_ref[0], precision=HIGHEST, preferred_element_type=F32) + b_ref[0]


def _modulation(c, c_ctx, mod_w, mod_b):
    bsz, d = c.shape
    depth = mod_w.shape[0]
    rows = jnp.concatenate([c, c_ctx[None], jnp.zeros((16 - bsz - 1, d), F32)], axis=0)
    bn = 6 * d // 4
    out = pl.pallas_call(
        _mod_kernel,
        grid=(depth, 4),
        in_specs=[pl.BlockSpec((16, d), lambda l, j: (0, 0)),
                  pl.BlockSpec((1, d, bn), lambda l, j: (l, 0, j)),
                  pl.BlockSpec((1, 1, bn), lambda l, j: (l, 0, j))],
        out_specs=pl.BlockSpec((1, 16, bn), lambda l, j: (l, 0, j)),
        out_shape=jax.ShapeDtypeStruct((depth, 16, 6 * d), F32),
        compiler_params=_params("parallel", "parallel"),
        name="modulation",
    )(rows, mod_w, mod_b.reshape(depth, 1, 6 * d))
    m_lat = out[:, :bsz].reshape(depth, bsz, 1, 6, d)
    m_ctx = jnp.broadcast_to(out[:, bsz].reshape(depth, 1, 1, 6, d), m_lat.shape)
    return jnp.concatenate([m_ctx, m_lat], axis=2).reshape(depth, 2 * bsz, 6, d)


def _hg_proj_kernel(h_ref, m_ref, g_ref, w_ref, lb_ref,
                    qs_ref, v_ref, sg_ref, kf_ref, kb_ref, lff_ref, lfb_ref):
    d = h_ref.shape[-1]
    m = m_ref[0]
    a = _rms(h_ref[0], g_ref[...]) * (1.0 + m[1:2]) + m[0:1]
    p = jnp.dot(a.astype(BF16), w_ref[...], preferred_element_type=F32)
    qs_ref[0] = _silu(p[:, 0:d]).astype(BF16)
    v_ref[0] = p[:, d:2 * d].astype(BF16)
    sg_ref[0] = _silu(p[:, 4 * d:5 * d]).astype(BF16)
    for di, (k_ref, lf_ref) in enumerate(((kf_ref, lff_ref), (kb_ref, lfb_ref))):
        lb = lb_ref[di:di + 1]
        f = lb + (1.0 - lb) * _sigmoid(p[:, (2 + di) * d:(3 + di) * d])
        k_ref[0] = (1.0 - f).astype(BF16)
        lf_ref[0] = jnp.log(f)


def _mod_spec(d, n_ctx_tiles):
    return pl.BlockSpec((1, 6, d), lambda b, i: (2 * b + (i >= n_ctx_tiles).astype(jnp.int32), 0, 0))


def _hg_proj(h, mod, g1, w_in, lb, n_ctx_tiles):
    bsz, t, d = h.shape
    tok = pl.BlockSpec((1, TOK_TILE, d), lambda b, i: (b, i, 0))
    bf = jax.ShapeDtypeStruct((bsz, t, d), BF16)
    ff = jax.ShapeDtypeStruct((bsz, t, d), F32)
    return pl.pallas_call(
        _hg_proj_kernel,
        grid=(bsz, t // TOK_TILE),
        in_specs=[tok, _mod_spec(d, n_ctx_tiles),
                  pl.BlockSpec((1, d), lambda b, i: (0, 0)),
                  pl.BlockSpec((d, 5 * d), lambda b, i: (0, 0)),
                  pl.BlockSpec((2, d), lambda b, i: (0, 0))],
        out_specs=[tok] * 7,
        out_shape=[bf, bf, bf, bf, bf, ff, ff],
        compiler_params=_params("parallel", "parallel"),
        name="hg_proj",
    )(h, mod, g1.reshape(1, d), w_in.astype(BF16), lb)


def _split3(x):
    hi = x.astype(BF16)
    r = x - hi.astype(F32)
    mid = r.astype(BF16)
    lo = (r - mid.astype(F32)).astype(BF16)
    return hi, mid, lo


def _gla_kernel(qf_ref, vf_ref, kf_ref, lf_ref, qb_ref, vb_ref, kb_ref, lb_ref,
                of_ref, ob_ref, sf_ref, sb_ref):
    c = HG_CHUNK
    n_b = qf_ref.shape[0]
    n_heads = sf_ref.shape[0] // n_b

    @pl.when(pl.program_id(1) == 0)
    def _():
        sf_ref[...] = jnp.zeros_like(sf_ref)
        sb_ref[...] = jnp.zeros_like(sb_ref)

    row = lax.broadcasted_iota(jnp.int32, (c, c), 0)
    col = lax.broadcasted_iota(jnp.int32, (c, c), 1)
    dirs = ((qf_ref, vf_ref, kf_ref, lf_ref, of_ref, sf_ref, col <= row, c // 2 - 1, c - 1),
            (qb_ref, vb_ref, kb_ref, lb_ref, ob_ref, sb_ref, col >= row, c // 2, 0))
    for bb, (q_ref, v_ref, k_ref, l_ref, o_ref, s_ref, keep, r_ref, r_last) in (
            (bb, dr) for bb in range(n_b) for dr in dirs):
        tri = keep.astype(BF16)
        b = sum(jnp.dot(tri, part, preferred_element_type=F32) for part in _split3(l_ref[bb]))
        b_ref = b[r_ref:r_ref + 1]
        b_last = b[r_last:r_last + 1]
        a_in = q_ref[bb].astype(F32) * jnp.exp(b - b_ref)
        k_in = k_ref[bb].astype(F32) * jnp.exp(b_ref - b)
        q_st = (a_in * jnp.exp(b_ref)).astype(BF16)
        k_st = (k_in * jnp.exp(b_last - b_ref)).astype(BF16)
        a_in = a_in.astype(BF16)
        k_in = k_in.astype(BF16)
        decay = jnp.exp(b_last)
        v = v_ref[bb]
        for h in range(n_heads):
            sl = slice(h * HEAD_DIM, (h + 1) * HEAD_DIM)
            sc = lax.dot_general(a_in[:, sl], k_in[:, sl], (((1,), (1,)), ((), ())),
                                 preferred_element_type=F32)
            sc = jnp.where(keep, sc, 0.0).astype(BF16)
            o = jnp.dot(sc, v[:, sl], preferred_element_type=F32)
            st = s_ref[bb * n_heads + h]
            o = o + lax.dot_general(q_st[:, sl], st.astype(BF16), (((1,), (1,)), ((), ())),
                                    preferred_element_type=F32)
            s_ref[bb * n_heads + h] = st * decay[:, sl] + lax.dot_general(
                v[:, sl], k_st[:, sl], (((0,), (0,)), ((), ())), preferred_element_type=F32)
            o_ref[bb, :, sl] = o


def _gla(qs, v, kf, kb, lff, lfb, n_ctx_chunks):
    bsz, t, d = qs.shape
    n = t // HG_CHUNK
    n_heads = d // HEAD_DIM

    def rev(j):
        return jnp.where(j < n_ctx_chunks, n_ctx_chunks - 1 - j, n + n_ctx_chunks - 1 - j)

    fwd = pl.BlockSpec((HG_BATCH, HG_CHUNK, d), lambda b, j: (b, j, 0))
    bwd = pl.BlockSpec((HG_BATCH, HG_CHUNK, d), lambda b, j: (b, rev(j), 0))
    out = jax.ShapeDtypeStruct((bsz, t, d), F32)
    state = pltpu.VMEM((HG_BATCH * n_heads, HEAD_DIM, HEAD_DIM), F32)
    return pl.pallas_call(
        _gla_kernel,
        grid=(bsz // HG_BATCH, n),
        in_specs=[fwd, fwd, fwd, fwd, bwd, bwd, bwd, bwd],
        out_specs=[fwd, bwd],
        out_shape=[out, out],
        scratch_shapes=[state, state],
        compiler_params=_params("parallel", "arbitrary"),
        name="gla",
    )(qs, v, kf, lff, qs, v, kb, lfb)


def _mixer_tail(first, h, y, m, g2, rw, rb, run_ref, h1_ref, f_ref, tw_ref, meta_ref, cnt_ref):
    h1 = h + m[2:3] * y
    f = _rms(h1, g2) * (1.0 + m[4:5]) + m[3:4]
    h1_ref[...] = h1.reshape(h1_ref.shape)
    _to_token_tiles(f_ref, f)
    logits = jnp.dot(f, rw, precision=HIGHEST, preferred_element_type=F32) + rb
    n, n_e = logits.shape
    lane = lax.broadcasted_iota(jnp.int32, logits.shape, 1).astype(F32)
    vals, idxs, hots = [], [], []
    for _ in range(TOP_K):
        mx = jnp.max(logits, axis=-1, keepdims=True)
        ix = jnp.min(jnp.where(logits == mx, lane, float(n_e)), axis=-1, keepdims=True)
        hot = lane == ix
        vals.append(mx)
        idxs.append(ix)
        hots.append(hot)
        logits = jnp.where(hot, -jnp.inf, logits)
    es = [jnp.exp(x - vals[0]) for x in vals]
    tot = sum(es)
    for k in range(TOP_K):
        tw_ref[:, k:k + 1] = es[k] / tot

    @pl.when(first)
    def _():
        run_ref[...] = jnp.zeros_like(run_ref)

    picked = sum(hot.astype(F32) for hot in hots)
    r_i = lax.broadcasted_iota(jnp.int32, (n, n), 0)
    c_i = lax.broadcasted_iota(jnp.int32, (n, n), 1)
    earlier = jnp.dot((c_i < r_i).astype(BF16), picked.astype(BF16), preferred_element_type=F32)
    rank = earlier + run_ref[...]
    total = run_ref[...] + jnp.sum(picked, axis=0, keepdims=True)
    run_ref[...] = total
    cnt_ref[...] = total
    col = lax.broadcasted_iota(jnp.int32, (n, LANES), 1)
    z = jnp.zeros((n, LANES), F32)
    for k in range(TOP_K):
        pos = jnp.sum(jnp.where(hots[k], rank, 0.0), axis=-1, keepdims=True)
        z = jnp.where(col == k, idxs[k], z)
        z = jnp.where(col == TOP_K + k, pos, z)
    meta_ref[0] = z.T[0:2 * TOP_K].astype(jnp.int32)


def _tail_outputs(n_tiles, tile, d, n_e, index):
    rt = d // LANES
    specs = [pl.BlockSpec((tile * rt, LANES), lambda b, i: (index(b, i), 0)),
             pl.BlockSpec((1, tile, TOP_K), lambda b, i: (index(b, i), 0, 0)),
             pl.BlockSpec((1, 2 * TOP_K, tile), lambda b, i: (index(b, i), 0, 0)),
             pl.BlockSpec((1, n_e), lambda b, i: (0, 0))]
    shapes = [jax.ShapeDtypeStruct((n_tiles * tile * rt, LANES), F32),
              jax.ShapeDtypeStruct((n_tiles, tile, TOP_K), F32),
              jax.ShapeDtypeStruct((n_tiles, 2 * TOP_K, tile), jnp.int32),
              jax.ShapeDtypeStruct((1, n_e), F32)]
    return specs, shapes


def _hg_readout_kernel(of_ref, ob_ref, sg_ref, h_ref, m_ref, ng_ref, w_ref, g2_ref, rw_ref, rb_ref,
                       h1_ref, f_ref, tw_ref, meta_ref, cnt_ref, run_ref):
    d = h_ref.shape[-1]
    o = of_ref[0] + ob_ref[0]
    parts = []
    for h in range(d // HEAD_DIM):
        oh = o[:, h * HEAD_DIM:(h + 1) * HEAD_DIM]
        parts.append(oh * lax.rsqrt(jnp.mean(oh * oh, axis=-1, keepdims=True) + RMS_EPS))
    o = jnp.concatenate(parts, axis=-1) * ng_ref[...]
    y = jnp.dot((o * sg_ref[0].astype(F32)).astype(BF16), w_ref[...], preferred_element_type=F32)
    first = (pl.program_id(0) == 0) & (pl.program_id(1) == 0)
    _mixer_tail(first, h_ref[0], y, m_ref[0], g2_ref[...], rw_ref[...], rb_ref[...], run_ref,
                h1_ref, f_ref, tw_ref.at[0], meta_ref, cnt_ref)


def _hg_readout(o_f, o_b, sg, h, mod, norm_g, w_out, g2, rw, rb, n_ctx_tiles):
    bsz, t, d = h.shape
    n_e = rw.shape[-1]
    nt = t // TOK_TILE
    tok = pl.BlockSpec((1, TOK_TILE, d), lambda b, i: (b, i, 0))
    const = lambda shape: pl.BlockSpec(shape, lambda b, i: (0,) * len(shape))
    tail_specs, tail_shapes = _tail_outputs(bsz * nt, TOK_TILE, d, n_e, lambda b, i: b * nt + i)
    return pl.pallas_call(
        _hg_readout_kernel,
        grid=(bsz, nt),
        in_specs=[tok, tok, tok, tok, _mod_spec(d, n_ctx_tiles),
                  const((1, d)), const((d, d)), const((1, d)), const((d, n_e)), const((1, n_e))],
        out_specs=[tok] + tail_specs,
        out_shape=[jax.ShapeDtypeStruct((bsz, t, d), F32)] + tail_shapes,
        scratch_shapes=[pltpu.VMEM((1, n_e), F32)],
        compiler_params=_params("arbitrary", "arbitrary"),
        name="hg_readout",
    )(o_f, o_b, sg, h, mod, norm_g.reshape(1, d), w_out.astype(BF16), g2.reshape(1, d),
      rw, rb.reshape(1, n_e))


def _moe_plan(counts, n_asg, rows):
    counts = counts.reshape(-1).astype(jnp.int32)
    padded = (counts + rows - 1) // rows * rows
    pad_end = jnp.cumsum(padded)
    n_blocks = -(-n_asg // rows) + N_EXPERTS
    block_pos = jnp.arange(n_blocks, dtype=jnp.int32) * rows
    block_e = jnp.minimum(jnp.sum((block_pos[:, None] >= pad_end[None, :]).astype(jnp.int32), axis=1),
                          N_EXPERTS - 1)
    n_used = (pad_end[-1:] // rows).astype(jnp.int32)
    return pad_end - padded, block_e, n_used, n_blocks


def _invert_kernel(start_ref, meta_ref, inv_ref):
    i = pl.program_id(0)
    tile = meta_ref.shape[2]

    @pl.when(i == 0)
    def _():
        def init(s, carry):
            inv_ref[s] = -1
            return carry
        lax.fori_loop(0, inv_ref.shape[0], init, 0, unroll=8)

    base = i * (tile * TOP_K)
    for r in range(tile):
        for k in range(TOP_K):
            slot = start_ref[meta_ref[0, k, r]] + meta_ref[0, TOP_K + k, r]
            inv_ref[slot] = base + (r * TOP_K + k)


def _invert(meta, slot_start, n_slots):
    n_tiles, _, tile = meta.shape
    grid_spec = pltpu.PrefetchScalarGridSpec(
        num_scalar_prefetch=1,
        grid=(n_tiles,),
        in_specs=[pl.BlockSpec((1, 2 * TOP_K, tile), lambda i, st: (i, 0, 0), memory_space=pltpu.SMEM)],
        out_specs=pl.BlockSpec(memory_space=pltpu.SMEM))
    return pl.pallas_call(
        _invert_kernel,
        grid_spec=grid_spec,
        out_shape=jax.ShapeDtypeStruct((n_slots,), jnp.int32),
        compiler_params=_params("arbitrary"),
        name="moe_invert",
    )(slot_start, meta)


def _ffn_kernel(be_ref, nu_ref, tok_ref, tok_next_ref, dst_prev_ref, dst_ref, x_hbm,
                w1_ref, b1_ref, w2_ref, b2_ref, y_hbm, xbuf, ybuf, zbuf, w1c, w2c, gsem, ssem, zsem):
    b = pl.program_id(0)
    nu = nu_ref[0]
    d, f2 = w1c.shape
    rt = d // LANES
    rows = xbuf.shape[1] // rt

    def tile_of(ref, idx):
        return ref.at[pl.ds(pl.multiple_of(idx * rt, rt), rt)]

    def gather(idx_ref, s):
        for r in range(rows):
            pltpu.make_async_copy(tile_of(x_hbm, idx_ref[0, 0, r]), xbuf.at[s, pl.ds(r * rt, rt)],
                                  gsem.at[s]).start()

    def scatter(idx_ref, s):
        for r in range(rows):
            pltpu.make_async_copy(ybuf.at[s, pl.ds(r * rt, rt)], tile_of(y_hbm, idx_ref[0, 0, r]),
                                  ssem.at[s]).start()

    def wait_gather(s):
        pltpu.make_async_copy(x_hbm.at[pl.ds(0, rows * rt)], xbuf.at[s], gsem.at[s]).wait()

    def wait_scatter(s):
        pltpu.make_async_copy(ybuf.at[s], y_hbm.at[pl.ds(0, rows * rt)], ssem.at[s]).wait()

    @pl.when(b == 0)
    def _():
        zbuf[...] = jnp.zeros_like(zbuf)
        ybuf[1] = jnp.zeros(ybuf.shape[1:], F32)
        gather(tok_ref, 0)

    @pl.when(b >= nu)
    def _():
        dst = y_hbm.at[pl.ds(pl.multiple_of(b * (rows * rt), rows * rt), rows * rt)]
        cp = pltpu.make_async_copy(zbuf, dst, zsem)
        cp.start()
        cp.wait()

    def used_block(s):
        @pl.when((b == 0) | (be_ref[b] != be_ref[jnp.maximum(b - 1, 0)]))
        def _():
            w1c[...] = w1_ref[0, 0].astype(BF16)
            w2c[...] = w2_ref[0, 0].astype(BF16)

        wait_gather(s)

        @pl.when(b >= 1)
        def _():
            wait_scatter(s)

        gather(tok_next_ref, 1 - s)
        scatter(dst_prev_ref, 1 - s)
        x = _from_token_tiles(xbuf.at[s], rows, d).astype(BF16)
        z = jnp.dot(x, w1c[...], preferred_element_type=F32) + b1_ref[0, 0]
        z_glu = jnp.minimum(z[:, :f2 // 2], SWIGLU_LIMIT)
        z_lin = jnp.clip(z[:, f2 // 2:], -SWIGLU_LIMIT, SWIGLU_LIMIT)
        act = z_glu * _sigmoid(SWIGLU_ALPHA * z_glu) * (z_lin + 1.0)
        y = jnp.dot(act.astype(BF16), w2c[...], preferred_element_type=F32) + b2_ref[0, 0]
        _to_token_tiles(ybuf.at[s], y)

        @pl.when(b == nu - 1)
        def _():
            scatter(dst_ref, s)
            wait_gather(1 - s)
            wait_scatter(1 - s)
            wait_scatter(s)

    for s in range(2):
        pl.when((b < nu) & (b % 2 == s))(functools.partial(used_block, s))


def _moe(f_tiles, meta, counts, layer, w1, b1, w2, b2):
    n_tiles, _, tile = meta.shape
    n_asg = n_tiles * tile * TOP_K
    _, n_e, d, f2 = w1.shape
    rt = d // LANES
    slot_start, block_e, n_used, n_blocks = _moe_plan(counts, n_asg, MOE_ROWS)
    n_slots = n_blocks * MOE_ROWS
    inv = _invert(meta, slot_start, n_slots)
    is_pad = inv < 0
    slot_tok = jnp.where(is_pad, 0, inv // TOP_K).reshape(n_blocks, 1, MOE_ROWS)
    spill = n_asg + jnp.cumsum(is_pad.astype(jnp.int32)) - 1
    slot_dst = jnp.where(is_pad, spill, inv)
    slot_dst = jnp.concatenate([slot_dst, jnp.arange(n_slots - MOE_ROWS, n_slots, dtype=jnp.int32)])
    slot_dst = slot_dst.reshape(n_blocks + 1, 1, MOE_ROWS)

    live = lambda b, nu: jnp.minimum(b, nu[0] - 1)
    smem = lambda imap: pl.BlockSpec((1, 1, MOE_ROWS), imap, memory_space=pltpu.SMEM)
    per_e = lambda shape: pl.BlockSpec((1, 1) + shape, lambda b, be, nu: (layer, be[live(b, nu)], 0, 0))
    buf = pltpu.VMEM((2, MOE_ROWS * rt, LANES), F32)
    grid_spec = pltpu.PrefetchScalarGridSpec(
        num_scalar_prefetch=2,
        grid=(n_blocks,),
        in_specs=[smem(lambda b, be, nu: (live(b, nu), 0, 0)),
                  smem(lambda b, be, nu: (live(b + 1, nu), 0, 0)),
                  smem(lambda b, be, nu: (jnp.where(b == 0, n_blocks, live(b - 1, nu)), 0, 0)),
                  smem(lambda b, be, nu: (live(b, nu), 0, 0)),
                  pl.BlockSpec(memory_space=pl.ANY),
                  per_e((d, f2)), per_e((1, f2)), per_e((f2 // 2, d)), per_e((1, d))],
        out_specs=pl.BlockSpec(memory_space=pl.ANY),
        scratch_shapes=[buf, buf, pltpu.VMEM((MOE_ROWS * rt, LANES), F32),
                        pltpu.VMEM((d, f2), BF16), pltpu.VMEM((f2 // 2, d), BF16),
                        pltpu.SemaphoreType.DMA((2,)), pltpu.SemaphoreType.DMA((2,)),
                        pltpu.SemaphoreType.DMA(())])
    return pl.pallas_call(
        _ffn_kernel,
        grid_spec=grid_spec,
        out_shape=jax.ShapeDtypeStruct((n_slots * rt, LANES), F32),
        compiler_params=_params("arbitrary"),
        name="moe_ffn",
    )(block_e, n_used, slot_tok, slot_tok, slot_dst, slot_dst, f_tiles, w1, b1.reshape(b1.shape[0], n_e, 1, f2),
      w2, b2.reshape(b2.shape[0], n_e, 1, d))


def _combine_experts(y_ref, tw_ref, tile, d):
    rt = d // LANES
    tw = tw_ref[0]
    out = 0.0
    for k in range(TOP_K):
        yk = jnp.concatenate([y_ref[pl.ds(k * rt + c, tile, stride=TOP_K * rt), :] for c in range(rt)],
                             axis=-1)
        out = out + tw[:, k:k + 1] * yk
    return out


def _combine0_kernel(y_ref, tw_ref, h_ref, m0_ref, m1_ref, g_ref, hl_ref, al_ref, ac_ref, *, n_ctx_tiles):
    tile, d = h_ref.shape[1:]
    h2 = h_ref[0] + m0_ref[0][5:6] * _combine_experts(y_ref, tw_ref, tile, d)
    m1 = m1_ref[0]
    a = _rms(h2, g_ref[...]) * (1.0 + m1[1:2]) + m1[0:1]
    is_ctx = pl.program_id(1) < n_ctx_tiles

    @pl.when(is_ctx)
    def _():
        ac_ref[0] = a

    @pl.when(jnp.logical_not(is_ctx))
    def _():
        hl_ref[0] = h2
        al_ref[0] = a


def _combine0(y_asg, top_w, h1, mod0, mod1, g1_next, n_ctx_tiles):
    bsz, t, d = h1.shape
    nt = t // TOK_TILE
    n_ctx = n_ctx_tiles * TOK_TILE
    tok = pl.BlockSpec((1, TOK_TILE, d), lambda b, i: (b, i, 0))
    lat = pl.BlockSpec((1, TOK_TILE, d), lambda b, i: (b, jnp.maximum(i - n_ctx_tiles, 0), 0))
    ctx = pl.BlockSpec((1, TOK_TILE, d), lambda b, i: (b, jnp.minimum(i, n_ctx_tiles - 1), 0))
    return pl.pallas_call(
        functools.partial(_combine0_kernel, n_ctx_tiles=n_ctx_tiles),
        grid=(bsz, nt),
        in_specs=[pl.BlockSpec((TOK_TILE * TOP_K * d // LANES, LANES), lambda b, i: (b * nt + i, 0)),
                  pl.BlockSpec((1, TOK_TILE, TOP_K), lambda b, i: (b * nt + i, 0, 0)),
                  tok, _mod_spec(d, n_ctx_tiles), _mod_spec(d, n_ctx_tiles),
                  pl.BlockSpec((1, d), lambda b, i: (0, 0))],
        out_specs=[lat, lat, ctx],
        out_shape=[jax.ShapeDtypeStruct((bsz, t - n_ctx, d), F32),
                   jax.ShapeDtypeStruct((bsz, t - n_ctx, d), F32),
                   jax.ShapeDtypeStruct((bsz, n_ctx, d), F32)],
        compiler_params=_params("parallel", "parallel"),
        name="combine0",
    )(y_asg, top_w, h1, mod0, mod1, g1_next.reshape(1, d))


def _s5_matrices(a_re, a_im, log_dt, b_re, b_im, c_re, c_im):
    ng, p = a_re.shape[1:]
    gc = b_re.shape[-1]
    ll = S5_CHUNK
    gpb = LANES // gc
    nblk = ng // gpb
    lam_re = jnp.minimum(a_re, S5_EIG_MAX)
    lam_im = a_im
    dt = jnp.exp(log_dt)[..., None]
    j = jnp.arange(ll + 1, dtype=F32).reshape(-1, 1, 1, 1)
    mag = jnp.exp(j * (lam_re * dt))
    pw_re = mag * jnp.cos(j * (lam_im * dt))
    pw_im = mag * jnp.sin(j * (lam_im * dt))
    ab_re, ab_im = pw_re[1], pw_im[1]
    den = lam_re * lam_re + lam_im * lam_im
    coef_re = ((ab_re - 1.0) * lam_re + ab_im * lam_im) / den
    coef_im = (ab_im * lam_re - (ab_re - 1.0) * lam_im) / den
    bb_re = coef_re[..., None] * b_re - coef_im[..., None] * b_im
    bb_im = coef_re[..., None] * b_im + coef_im[..., None] * b_re
    drv_re = pw_re[..., None] * bb_re - pw_im[..., None] * bb_im
    drv_im = pw_re[..., None] * bb_im + pw_im[..., None] * bb_re
    rd_re = c_re * pw_re[:, :, :, None, :] - c_im * pw_im[:, :, :, None, :]
    rd_im = -(c_re * pw_im[:, :, :, None, :] + c_im * pw_re[:, :, :, None, :])
    taps = (jnp.einsum('dgop,jdgpi->jdgoi', c_re, drv_re[:ll], precision=HIGHEST)
            - jnp.einsum('dgop,jdgpi->jdgoi', c_im, drv_im[:ll], precision=HIGHEST))
    r = jnp.arange(ll)
    c_in, c_drv, c_rd = [], [], []
    for d in range(2):
        lag = (r[None, :] - r[:, None]) if d == 0 else (r[:, None] - r[None, :])
        tp = jnp.where((lag >= 0)[:, :, None, None, None],
                       taps[:, d][jnp.clip(lag, 0, ll - 1)], 0.0)
        tp = tp.reshape(ll, ll, nblk, gpb, gc, gc).transpose(2, 0, 3, 5, 1, 4)
        c_in.append(tp.reshape(nblk, ll * LANES, ll * gc))
        steps = (ll - 1 - r) if d == 0 else r
        dr = jnp.stack([drv_re[:, d][steps], drv_im[:, d][steps]], axis=1)
        dr = dr.reshape(ll, 2, nblk, gpb, p, gc).transpose(2, 0, 3, 5, 1, 4)
        c_drv.append(dr.reshape(nblk, ll * LANES, 2 * p))
        steps = (r + 1) if d == 0 else (ll - r)
        rd = jnp.stack([rd_re[:, d][steps], rd_im[:, d][steps]], axis=1)
        rd = rd.reshape(ll, 2, nblk, gpb, gc, p).transpose(2, 1, 3, 5, 0, 4)
        c_rd.append(rd.reshape(nblk, 2 * gpb * p, ll * gc))

    def expand(compact, row_unit, col_unit):
        compact = jnp.stack(compact).astype(BF16)
        n_r, n_c = compact.shape[2], compact.shape[3] * gpb
        col = jnp.arange(n_c)
        src = (col // (col_unit * gpb)) * col_unit + col % col_unit
        spread = (jnp.arange(n_c // gpb)[:, None] == src[None, :]).astype(BF16)
        full = jnp.einsum('dbrk,kc->dbrc', compact, spread, preferred_element_type=F32)
        same = ((jnp.arange(n_r) // row_unit) % gpb)[:, None] == ((col // col_unit) % gpb)[None, :]
        return jnp.where(same, full, 0.0).astype(BF16)

    a8 = jnp.stack([pw_re[ll], pw_im[ll]], axis=1).reshape(2, 2, nblk, gpb * p).transpose(0, 2, 1, 3)
    return expand(c_in, gc, gc), expand(c_drv, gc, p), expand(c_rd, p, gc), a8


def _s5_kernel(uc_ref, ul_ref, min_ref, mdrv_ref, mrd_ref, a8_ref, y_ref, v_ref, st_ref, *, n_ctx_cols):
    bsz, n_oct, ll, n_col, lanes = ul_ref.shape
    n_rows = bsz * n_oct * n_col
    half = st_ref.shape[-1] // 2
    d = pl.program_id(0)
    k = pl.program_id(2)

    @pl.when(k == 0)
    def _():
        st_ref[...] = jnp.zeros_like(st_ref)

    def chunks(u_ref):
        return jnp.concatenate([u_ref[:, :, r].reshape(n_rows, lanes) for r in range(ll)], axis=-1)

    x = jnp.where(k == 0, chunks(uc_ref), chunks(ul_ref)).astype(BF16)
    inj_all = jnp.dot(x, mdrv_ref[0, 0], preferred_element_type=F32)
    n_pl = v_ref.shape[0]
    for c in range(n_pl):
        v_ref[c] = inj_all[:, c * lanes:(c + 1) * lanes]

    a_re = jnp.broadcast_to(a8_ref[0, 0, 0:1], (bsz, half))
    a_im = jnp.broadcast_to(a8_ref[0, 0, 1:2], (bsz, half))
    def scan(n_steps):
        def step(i, carry):
            s_re, s_im = carry
            i = jnp.where(d == 0, i, n_steps - 1 - i)
            row = (i % n_oct) * n_col + i // n_oct
            rows = pl.ds(row, bsz, stride=n_oct * n_col)
            inj = jnp.concatenate([v_ref[c, rows, :] for c in range(n_pl)], axis=-1)
            for c in range(n_pl // 2):
                v_ref[c, rows, :] = s_re[:, c * lanes:(c + 1) * lanes]
                v_ref[n_pl // 2 + c, rows, :] = s_im[:, c * lanes:(c + 1) * lanes]
            return (a_re * s_re - a_im * s_im + inj[:, :half],
                    a_re * s_im + a_im * s_re + inj[:, half:])

        s_re, s_im = lax.fori_loop(0, n_steps, step, (st_ref[:, :half], st_ref[:, half:]), unroll=2)
        st_ref[:, :half] = s_re
        st_ref[:, half:] = s_im

    @pl.when(k == 0)
    def _():
        scan(n_ctx_cols * n_oct)

    @pl.when(k > 0)
    def _():
        scan(n_col * n_oct)

    s_start = jnp.concatenate([v_ref[c] for c in range(n_pl)], axis=-1).astype(BF16)
    y = (jnp.dot(x, min_ref[0, 0], preferred_element_type=F32)
         + jnp.dot(s_start, mrd_ref[0, 0], preferred_element_type=F32))
    for r in range(ll):
        y_ref[0, :, :, r] = y[:, r * lanes:(r + 1) * lanes].reshape(bsz, n_oct, n_col, lanes)


def _s5_scan(a_lat, a_ctx, mats):
    bsz, t, d = a_lat.shape
    n_ctx = a_ctx.shape[1]
    m_in, m_drv, m_rd, a8 = mats
    n_rows = t // GRID_W
    n_oct = n_rows // S5_CHUNK
    n_ctx_cols = n_ctx // n_rows
    u_lat = a_lat.reshape(bsz, n_oct, S5_CHUNK, GRID_W, d)
    u_ctx = a_ctx.reshape(bsz, n_ctx_cols, n_rows, d).transpose(0, 2, 1, 3)
    u_ctx = jnp.pad(u_ctx, ((0, 0), (0, 0), (0, S5_COLS - n_ctx_cols), (0, 0)))
    u_ctx = u_ctx.reshape(bsz, n_oct, S5_CHUNK, S5_COLS, d)
    n_lat = GRID_W // S5_COLS
    nblk = d // LANES

    def lat_tile(dd, k):
        kk = jnp.maximum(k, 1) - 1
        return jnp.where(dd == 0, kk, n_lat - 1 - kk)

    blk = (bsz, n_oct, S5_CHUNK, S5_COLS, LANES)
    wspec = lambda shape: pl.BlockSpec((1, 1) + shape, lambda dd, j, k: (dd, j, 0, 0))
    kl = S5_CHUNK * LANES
    ns = m_drv.shape[-1]
    y = pl.pallas_call(
        functools.partial(_s5_kernel, n_ctx_cols=n_ctx_cols),
        grid=(2, nblk, n_lat + 1),
        in_specs=[pl.BlockSpec(blk, lambda dd, j, k: (0, 0, 0, 0, j)),
                  pl.BlockSpec(blk, lambda dd, j, k: (0, 0, 0, lat_tile(dd, k), j)),
                  wspec((kl, kl)), wspec((kl, ns)), wspec((ns, kl)),
                  pl.BlockSpec((1, 1, 2, ns // 2), lambda dd, j, k: (dd, j, 0, 0))],
        out_specs=pl.BlockSpec((1,) + blk, lambda dd, j, k: (dd, 0, 0, 0, lat_tile(dd, k), j)),
        out_shape=jax.ShapeDtypeStruct((2,) + u_lat.shape, F32),
        scratch_shapes=[pltpu.VMEM((ns // LANES, bsz * n_oct * S5_COLS, LANES), F32),
                        pltpu.VMEM((bsz, ns), F32)],
        compiler_params=_params("parallel", "parallel", "arbitrary"),
        name="s5_scan",
    )(u_ctx, u_lat, m_in, m_drv, m_rd, a8)
    return y.reshape(2, bsz, n_rows, GRID_W, d)


def _s5_glu_kernel(y_ref, u_ref, h_ref, m_ref, dsk_ref, w_ref, bg_ref, g2_ref, rw_ref, rb_ref,
                   h1_ref, f_ref, tw_ref, meta_ref, cnt_ref, run_ref):
    d = h_ref.shape[-1]
    n = h_ref.shape[1] * h_ref.shape[2]
    y = (y_ref[0, 0] + y_ref[1, 0] + dsk_ref[...] * u_ref[0]).reshape(n, d)
    z = jnp.dot(jax.nn.gelu(y).astype(BF16), w_ref[...], preferred_element_type=F32) + bg_ref[...]
    y = z[:, :d] * _sigmoid(z[:, d:])
    first = (pl.program_id(0) == 0) & (pl.program_id(1) == 0)
    _mixer_tail(first, h_ref[0].reshape(n, d), y, m_ref[0], g2_ref[...], rw_ref[...], rb_ref[...],
                run_ref, h1_ref, f_ref, tw_ref.at[0], meta_ref, cnt_ref)


def _s5_glu(y, a_lat, h_lat, mod, d_skip, w_glu, b_glu, g2, rw, rb):
    bsz, t, d = h_lat.shape
    n_rows = t // GRID_W
    n_e = rw.shape[-1]
    n_tok = n_rows * S5_COLS
    nt = GRID_W // S5_COLS
    lat = pl.BlockSpec((1, n_rows, S5_COLS, d), lambda b, i: (b, 0, i, 0))
    tail_specs, tail_shapes = _tail_outputs(bsz * nt, n_tok, d, n_e, lambda b, i: b * nt + i)
    const = lambda shape: pl.BlockSpec(shape, lambda b, i: (0,) * len(shape))
    return pl.pallas_call(
        _s5_glu_kernel,
        grid=(bsz, nt),
        in_specs=[pl.BlockSpec((2, 1, n_rows, S5_COLS, d), lambda b, i: (0, b, 0, i, 0)),
                  lat, lat, pl.BlockSpec((1, 6, d), lambda b, i: (2 * b + 1, 0, 0)),
                  const((1, d)), const((d, 2 * d)), const((1, 2 * d)), const((1, d)),
                  const((d, n_e)), const((1, n_e))],
        out_specs=[lat] + tail_specs,
        out_shape=[jax.ShapeDtypeStruct((bsz, n_rows, GRID_W, d), F32)] + tail_shapes,
        scratch_shapes=[pltpu.VMEM((1, n_e), F32)],
        compiler_params=_params("arbitrary", "arbitrary"),
        name="s5_glu",
    )(y, a_lat.reshape(bsz, n_rows, GRID_W, d), h_lat.reshape(bsz, n_rows, GRID_W, d), mod,
      d_skip.reshape(1, d), w_glu.astype(BF16), b_glu.reshape(1, 2 * d),
      g2.reshape(1, d), rw, rb.reshape(1, n_e))


def _combine1_kernel(y_ref, tw_ref, h_ref, m_ref, g_ref, o_ref):
    d = h_ref.shape[-1]
    n = h_ref.shape[1] * h_ref.shape[2]
    h2 = h_ref[0].reshape(n, d) + m_ref[0][5:6] * _combine_experts(y_ref, tw_ref, n, d)
    o_ref[0] = _rms(h2, g_ref[...]).reshape(o_ref.shape[1:])


def _combine1(y_asg, top_w, h1, mod, final_g):
    bsz, n_rows, _, d = h1.shape
    nt = GRID_W // S5_COLS
    n_tok = n_rows * S5_COLS
    lat = pl.BlockSpec((1, n_rows, S5_COLS, d), lambda b, i: (b, 0, i, 0))
    return pl.pallas_call(
        _combine1_kernel,
        grid=(bsz, nt),
        in_specs=[pl.BlockSpec((n_tok * TOP_K * d // LANES, LANES), lambda b, i: (b * nt + i, 0)),
                  pl.BlockSpec((1, n_tok, TOP_K), lambda b, i: (b * nt + i, 0, 0)),
                  lat, pl.BlockSpec((1, 6, d), lambda b, i: (2 * b + 1, 0, 0)),
                  pl.BlockSpec((1, d), lambda b, i: (0, 0))],
        out_specs=lat,
        out_shape=jax.ShapeDtypeStruct(h1.shape, F32),
        compiler_params=_params("parallel", "parallel"),
        name="combine1",
    )(y_asg, top_w, h1, mod, final_g.reshape(1, d))


def kernel(x, c, ctx, c_ctx, mod_w, mod_b, norm1_g, norm2_g, hg_w_in, hg_lb_logits, hg_norm_g,
           hg_w_out, s5_a_re, s5_a_im, s5_log_dt, s5_b_re, s5_b_im, s5_c_re, s5_c_im, s5_d, s5_w_glu,
           s5_b_glu, router_w, router_b, moe_w1, moe_b1, moe_w2, moe_b2, final_g):
    bsz, seq, d = x.shape
    n_ctx = ctx.shape[1]
    assert mod_w.shape[0] == 2 and n_ctx % TOK_TILE == 0 and seq % TOK_TILE == 0
    n_ctx_tiles = n_ctx // TOK_TILE
    lb_all = jnp.cumsum(jax.nn.softmax(hg_lb_logits.astype(F32), axis=0), axis=0)
    mod = _modulation(c, c_ctx, mod_w, mod_b)
    h = jnp.concatenate([ctx, x], axis=1)

    qs, v, sg, kf, kb, lff, lfb = _hg_proj(h, mod[0], norm1_g[0], hg_w_in[0], lb_all[0], n_ctx_tiles)
    o_f, o_b = _gla(qs, v, kf, kb, lff, lfb, n_ctx // HG_CHUNK)
    h1, f, top_w, meta, counts = _hg_readout(o_f, o_b, sg, h, mod[0], hg_norm_g[0], hg_w_out[0],
                                             norm2_g[0], router_w[0], router_b[0], n_ctx_tiles)
    y_asg = _moe(f, meta, counts, 0, moe_w1, moe_b1, moe_w2, moe_b2)
    h_lat, a_lat, a_ctx = _combine0(y_asg, top_w, h1, mod[0], mod[1], norm1_g[1], n_ctx_tiles)

    mats = _s5_matrices(s5_a_re[0], s5_a_im[0], s5_log_dt[0], s5_b_re[0], s5_b_im[0],
                        s5_c_re[0], s5_c_im[0])
    y_s5 = _s5_scan(a_lat, a_ctx, mats)
    h1, f, top_w, meta, counts = _s5_glu(y_s5, a_lat, h_lat, mod[1], s5_d[0], s5_w_glu[0], s5_b_glu[0],
                                         norm2_g[1], router_w[1], router_b[1])
    y_asg = _moe(f, meta, counts, 1, moe_w1, moe_b1, moe_w2, moe_b2)
    out = _combine1(y_asg, top_w, h1, mod[1], final_g)
    return out.reshape(bsz, seq, d)
```

```python
import functools

import jax
import jax.numpy as jnp
from jax import lax
from jax.experimental import pallas as pl
from jax.experimental.pallas import tpu as pltpu

F32 = jnp.float32
BF16 = jnp.bfloat16
HIGHEST = lax.Precision.HIGHEST

RMS_EPS = 1e-6
GRID_W = 64
HEAD_DIM = 128
HG_CHUNK = 64
HG_BATCH = 2
S5_GROUP = 16
S5_STATE = 64
S5_EIG_MAX = -1e-4
S5_CHUNK = 8
S5_COLS = 8
LANES = 128
N_EXPERTS = 32
TOP_K = 4
SWIGLU_ALPHA = 1.702
SWIGLU_LIMIT = 7.0
TOK_TILE = 256
MOE_ROWS = 256
VMEM_LIMIT = 56 * 1024 * 1024


def _params(*sem):
    return pltpu.CompilerParams(dimension_semantics=sem, vmem_limit_bytes=VMEM_LIMIT)


def _rms(x, g):
    return x * lax.rsqrt(jnp.mean(x * x, axis=-1, keepdims=True) + RMS_EPS) * g


def _sigmoid(x):
    return 1.0 / (1.0 + jnp.exp(-x))


def _silu(x):
    return x * _sigmoid(x)


def _to_token_tiles(ref, x):
    n, d = x.shape
    for c in range(d // LANES):
        ref[pl.ds(c, n, stride=d // LANES), :] = x[:, c * LANES:(c + 1) * LANES]


def _from_token_tiles(ref, n, d):
    return jnp.concatenate([ref[pl.ds(c, n, stride=d // LANES), :] for c in range(d // LANES)], axis=-1)


def _mod_kernel(c_ref, w_ref, b_ref, o_ref):
    s = _silu(c_ref[...])
    o_ref[0] = jnp.dot(s, w_ref[0], precision=HIGHEST, preferred_element_type=F32) + b_ref[0]


def _modulation(c, c_ctx, mod_w, mod_b):
    bsz, d = c.shape
    depth = mod_w.shape[0]
    rows = jnp.concatenate([c, c_ctx[None], jnp.zeros((16 - bsz - 1, d), F32)], axis=0)
    bn = 6 * d // 4
    out = pl.pallas_call(
        _mod_kernel,
        grid=(depth, 4),
        in_specs=[pl.BlockSpec((16, d), lambda l, j: (0, 0)),
                  pl.BlockSpec((1, d, bn), lambda l, j: (l, 0, j)),
                  pl.BlockSpec((1, 1, bn), lambda l, j: (l, 0, j))],
        out_specs=pl.BlockSpec((1, 16, bn), lambda l, j: (l, 0, j)),
        out_shape=jax.ShapeDtypeStruct((depth, 16, 6 * d), F32),
        compiler_params=_params("parallel", "parallel"),
        name="modulation",
    )(rows, mod_w, mod_b.reshape(depth, 1, 6 * d))
    m_lat = out[:, :bsz].reshape(depth, bsz, 1, 6, d)
    m_ctx = jnp.broadcast_to(out[:, bsz].reshape(depth, 1, 1, 6, d), m_lat.shape)
    return jnp.concatenate([m_ctx, m_lat], axis=2).reshape(depth, 2 * bsz, 6, d)


def _hg_proj_kernel(h_ref, m_ref, g_ref, w_ref, lb_ref,
                    qs_ref, v_ref, sg_ref, kf_ref, kb_ref, lff_ref, lfb_ref):
    d = h_ref.shape[-1]
    m = m_ref[0]
    a = _rms(h_ref[0], g_ref[...]) * (1.0 + m[1:2]) + m[0:1]
    p = jnp.dot(a.astype(BF16), w_ref[...], preferred_element_type=F32)
    qs_ref[0] = _silu(p[:, 0:d]).astype(BF16)
    v_ref[0] = p[:, d:2 * d].astype(BF16)
    sg_ref[0] = _silu(p[:, 4 * d:5 * d]).astype(BF16)
    for di, (k_ref, lf_ref) in enumerate(((kf_ref, lff_ref), (kb_ref, lfb_ref))):
        lb = lb_ref[di:di + 1]
        f = lb + (1.0 - lb) * _sigmoid(p[:, (2 + di) * d:(3 + di) * d])
        k_ref[0] = (1.0 - f).astype(BF16)
        lf_ref[0] = jnp.log(f)


def _mod_spec(d, n_ctx_tiles):
    return pl.BlockSpec((1, 6, d), lambda b, i: (2 * b + (i >= n_ctx_tiles).astype(jnp.int32), 0, 0))


def _hg_proj(h, mod, g1, w_in, lb, n_ctx_tiles):
    bsz, t, d = h.shape
    tok = pl.BlockSpec((1, TOK_TILE, d), lambda b, i: (b, i, 0))
    bf = jax.ShapeDtypeStruct((bsz, t, d), BF16)
    ff = jax.ShapeDtypeStruct((bsz, t, d), F32)
    return pl.pallas_call(
        _hg_proj_kernel,
        grid=(bsz, t // TOK_TILE),
        in_specs=[tok, _mod_spec(d, n_ctx_tiles),
                  pl.BlockSpec((1, d), lambda b, i: (0, 0)),
                  pl.BlockSpec((d, 5 * d), lambda b, i: (0, 0)),
                  pl.BlockSpec((2, d), lambda b, i: (0, 0))],
        out_specs=[tok] * 7,
        out_shape=[bf, bf, bf, bf, bf, ff, ff],
        compiler_params=_params("parallel", "parallel"),
        name="hg_proj",
    )(h, mod, g1.reshape(1, d), w_in.astype(BF16), lb)


def _split3(x):
    hi = x.astype(BF16)
    r = x - hi.astype(F32)
    mid = r.astype(BF16)
    lo = (r - mid.astype(F32)).astype(BF16)
    return hi, mid, lo


def _gla_kernel(qf_ref, vf_ref, kf_ref, lf_ref, qb_ref, vb_ref, kb_ref, lb_ref,
                of_ref, ob_ref, sf_ref, sb_ref):
    c = HG_CHUNK
    n_b = qf_ref.shape[0]
    n_heads = sf_ref.shape[0] // n_b

    @pl.when(pl.program_id(1) == 0)
    def _():
        sf_ref[...] = jnp.zeros_like(sf_ref)
        sb_ref[...] = jnp.zeros_like(sb_ref)

    row = lax.broadcasted_iota(jnp.int32, (c, c), 0)
    col = lax.broadcasted_iota(jnp.int32, (c, c), 1)
    dirs = ((qf_ref, vf_ref, kf_ref, lf_ref, of_ref, sf_ref, col <= row, c // 2 - 1, c - 1),
            (qb_ref, vb_ref, kb_ref, lb_ref, ob_ref, sb_ref, col >= row, c // 2, 0))
    for bb, (q_ref, v_ref, k_ref, l_ref, o_ref, s_ref, keep, r_ref, r_last) in (
            (bb, dr) for bb in range(n_b) for dr in dirs):
        tri = keep.astype(BF16)
        b = sum(jnp.dot(tri, part, preferred_element_type=F32) for part in _split3(l_ref[bb]))
        b_ref = b[r_ref:r_ref + 1]
        b_last = b[r_last:r_last + 1]
        a_in = q_ref[bb].astype(F32) * jnp.exp(b - b_ref)
        k_in = k_ref[bb].astype(F32) * jnp.exp(b_ref - b)
        q_st = (a_in * jnp.exp(b_ref)).astype(BF16)
        k_st = (k_in * jnp.exp(b_last - b_ref)).astype(BF16)
        a_in = a_in.astype(BF16)
        k_in = k_in.astype(BF16)
        decay = jnp.exp(b_last)
        v = v_ref[bb]
        for h in range(n_heads):
            sl = slice(h * HEAD_DIM, (h + 1) * HEAD_DIM)
            sc = lax.dot_general(a_in[:, sl], k_in[:, sl], (((1,), (1,)), ((), ())),
                                 preferred_element_type=F32)
            sc = jnp.where(keep, sc, 0.0).astype(BF16)
            o = jnp.dot(sc, v[:, sl], preferred_element_type=F32)
            st = s_ref[bb * n_heads + h]
            o = o + lax.dot_general(q_st[:, sl], st.astype(BF16), (((1,), (1,)), ((), ())),
                                    preferred_element_type=F32)
            s_ref[bb * n_heads + h] = st * decay[:, sl] + lax.dot_general(
                v[:, sl], k_st[:, sl], (((0,), (0,)), ((), ())), preferred_element_type=F32)
            o_ref[bb, :, sl] = o


def _gla(qs, v, kf, kb, lff, lfb, n_ctx_chunks):
    bsz, t, d = qs.shape
    n = t // HG_CHUNK
    n_heads = d // HEAD_DIM

    def rev(j):
        return jnp.where(j < n_ctx_chunks, n_ctx_chunks - 1 - j, n + n_ctx_chunks - 1 - j)

    fwd = pl.BlockSpec((HG_BATCH, HG_CHUNK, d), lambda b, j: (b, j, 0))
    bwd = pl.BlockSpec((HG_BATCH, HG_CHUNK, d), lambda b, j: (b, rev(j), 0))
    out = jax.ShapeDtypeStruct((bsz, t, d), F32)
    state = pltpu.VMEM((HG_BATCH * n_heads, HEAD_DIM, HEAD_DIM), F32)
    return pl.pallas_call(
        _gla_kernel,
        grid=(bsz // HG_BATCH, n),
        in_specs=[fwd, fwd, fwd, fwd, bwd, bwd, bwd, bwd],
        out_specs=[fwd, bwd],
        out_shape=[out, out],
        scratch_shapes=[state, state],
        compiler_params=_params("parallel", "arbitrary"),
        name="gla",
    )(qs, v, kf, lff, qs, v, kb, lfb)


def _mixer_tail(first, h, y, m, g2, rw, rb, run_ref, h1_ref, f_ref, tw_ref, meta_ref, cnt_ref):
    h1 = h + m[2:3] * y
    f = _rms(h1, g2) * (1.0 + m[4:5]) + m[3:4]
    h1_ref[...] = h1.reshape(h1_ref.shape)
    _to_token_tiles(f_ref, f)
    logits = jnp.dot(f, rw, precision=HIGHEST, preferred_element_type=F32) + rb
    n, n_e = logits.shape
    lane = lax.broadcasted_iota(jnp.int32, logits.shape, 1).astype(F32)
    vals, idxs, hots = [], [], []
    for _ in range(TOP_K):
        mx = jnp.max(logits, axis=-1, keepdims=True)
        ix = jnp.min(jnp.where(logits == mx, lane, float(n_e)), axis=-1, keepdims=True)
        hot = lane == ix
        vals.append(mx)
        idxs.append(ix)
        hots.append(hot)
        logits = jnp.where(hot, -jnp.inf, logits)
    es = [jnp.exp(x - vals[0]) for x in vals]
    tot = sum(es)
    for k in range(TOP_K):
        tw_ref[:, k:k + 1] = es[k] / tot

    @pl.when(first)
    def _():
        run_ref[...] = jnp.zeros_like(run_ref)

    picked = sum(hot.astype(F32) for hot in hots)
    r_i = lax.broadcasted_iota(jnp.int32, (n, n), 0)
    c_i = lax.broadcasted_iota(jnp.int32, (n, n), 1)
    earlier = jnp.dot((c_i < r_i).astype(BF16), picked.astype(BF16), preferred_element_type=F32)
    rank = earlier + run_ref[...]
    total = run_ref[...] + jnp.sum(picked, axis=0, keepdims=True)
    run_ref[...] = total
    cnt_ref[...] = total
    col = lax.broadcasted_iota(jnp.int32, (n, LANES), 1)
    z = jnp.zeros((n, LANES), F32)
    for k in range(TOP_K):
        pos = jnp.sum(jnp.where(hots[k], rank, 0.0), axis=-1, keepdims=True)
        z = jnp.where(col == k, idxs[k], z)
        z = jnp.where(col == TOP_K + k, pos, z)
    meta_ref[0] = z.T[0:2 * TOP_K].astype(jnp.int32)


def _tail_outputs(n_tiles, tile, d, n_e, index):
    rt = d // LANES
    specs = [pl.BlockSpec((tile * rt, LANES), lambda b, i: (index(b, i), 0)),
             pl.BlockSpec((1, tile, TOP_K), lambda b, i: (index(b, i), 0, 0)),
             pl.BlockSpec((1, 2 * TOP_K, tile), lambda b, i: (index(b, i), 0, 0)),
             pl.BlockSpec((1, n_e), lambda b, i: (0, 0))]
    shapes = [jax.ShapeDtypeStruct((n_tiles * tile * rt, LANES), F32),
              jax.ShapeDtypeStruct((n_tiles, tile, TOP_K), F32),
              jax.ShapeDtypeStruct((n_tiles, 2 * TOP_K, tile), jnp.int32),
              jax.ShapeDtypeStruct((1, n_e), F32)]
    return specs, shapes


def _hg_readout_kernel(of_ref, ob_ref, sg_ref, h_ref, m_ref, ng_ref, w_ref, g2_ref, rw_ref, rb_ref,
                       h1_ref, f_ref, tw_ref, meta_ref, cnt_ref, run_ref):
    d = h_ref.shape[-1]
    o = of_ref[0] + ob_ref[0]
    parts = []
    for h in range(d // HEAD_DIM):
        oh = o[:, h * HEAD_DIM:(h + 1) * HEAD_DIM]
        parts.append(oh * lax.rsqrt(jnp.mean(oh * oh, axis=-1, keepdims=True) + RMS_EPS))
    o = jnp.concatenate(parts, axis=-1) * ng_ref[...]
    y = jnp.dot((o * sg_ref[0].astype(F32)).astype(BF16), w_ref[...], preferred_element_type=F32)
    first = (pl.program_id(0) == 0) & (pl.program_id(1) == 0)
    _mixer_tail(first, h_ref[0], y, m_ref[0], g2_ref[...], rw_ref[...], rb_ref[...], run_ref,
                h1_ref, f_ref, tw_ref.at[0], meta_ref, cnt_ref)


def _hg_readout(o_f, o_b, sg, h, mod, norm_g, w_out, g2, rw, rb, n_ctx_tiles):
    bsz, t, d = h.shape
    n_e = rw.shape[-1]
    nt = t // TOK_TILE
    tok = pl.BlockSpec((1, TOK_TILE, d), lambda b, i: (b, i, 0))
    const = lambda shape: pl.BlockSpec(shape, lambda b, i: (0,) * len(shape))
    tail_specs, tail_shapes = _tail_outputs(bsz * nt, TOK_TILE, d, n_e, lambda b, i: b * nt + i)
    return pl.pallas_call(
        _hg_readout_kernel,
        grid=(bsz, nt),
        in_specs=[tok, tok, tok, tok, _mod_spec(d, n_ctx_tiles),
                  const((1, d)), const((d, d)), const((1, d)), const((d, n_e)), const((1, n_e))],
        out_specs=[tok] + tail_specs,
        out_shape=[jax.ShapeDtypeStruct((bsz, t, d), F32)] + tail_shapes,
        scratch_shapes=[pltpu.VMEM((1, n_e), F32)],
        compiler_params=_params("arbitrary", "arbitrary"),
        name="hg_readout",
    )(o_f, o_b, sg, h, mod, norm_g.reshape(1, d), w_out.astype(BF16), g2.reshape(1, d),
      rw, rb.reshape(1, n_e))


def _moe_plan(counts, n_asg, rows):
    counts = counts.reshape(-1).astype(jnp.int32)
    padded = (counts + rows - 1) // rows * rows
    pad_end = jnp.cumsum(padded)
    n_blocks = -(-n_asg // rows) + N_EXPERTS
    block_pos = jnp.arange(n_blocks, dtype=jnp.int32) * rows
    block_e = jnp.minimum(jnp.sum((block_pos[:, None] >= pad_end[None, :]).astype(jnp.int32), axis=1),
                          N_EXPERTS - 1)
    n_used = (pad_end[-1:] // rows).astype(jnp.int32)
    return pad_end - padded, block_e, n_used, n_blocks


def _invert_kernel(dest_ref, inv_ref):
    i = pl.program_id(0)
    tile = dest_ref.shape[2]

    @pl.when(i == 0)
    def _():
        def init(s, carry):
            inv_ref[s] = -1
            return carry
        lax.fori_loop(0, inv_ref.shape[0], init, 0, unroll=8)

    base = i * (tile * TOP_K)
    for r in range(tile):
        for k in range(TOP_K):
            inv_ref[dest_ref[0, k, r]] = base + (r * TOP_K + k)


def _invert(meta, slot_start, n_slots):
    n_tiles, _, tile = meta.shape
    hot = meta[:, :TOP_K, :, None] == jnp.arange(N_EXPERTS, dtype=jnp.int32)
    dest = meta[:, TOP_K:] + jnp.sum(jnp.where(hot, slot_start, 0), axis=-1)
    return pl.pallas_call(
        _invert_kernel,
        grid=(n_tiles,),
        in_specs=[pl.BlockSpec((1, TOP_K, tile), lambda i: (i, 0, 0), memory_space=pltpu.SMEM)],
        out_specs=pl.BlockSpec(memory_space=pltpu.SMEM),
        out_shape=jax.ShapeDtypeStruct((n_slots,), jnp.int32),
        compiler_params=_params("arbitrary"),
        name="moe_invert",
    )(dest.astype(jnp.int32))


def _ffn_kernel(be_ref, nu_ref, tok_ref, tok_next_ref, dst_prev_ref, dst_ref, x_hbm,
                w1_ref, b1_ref, w2_ref, b2_ref, y_hbm, xbuf, ybuf, zbuf, w1c, w2c, gsem, ssem, zsem):
    b = pl.program_id(0)
    nu = nu_ref[0]
    d, f2 = w1c.shape
    rt = d // LANES
    rows = xbuf.shape[1] // rt

    def tile_of(ref, idx):
        return ref.at[pl.ds(pl.multiple_of(idx * rt, rt), rt)]

    def gather(idx_ref, s):
        for r in range(rows):
            pltpu.make_async_copy(tile_of(x_hbm, idx_ref[0, 0, r]), xbuf.at[s, pl.ds(r * rt, rt)],
                                  gsem.at[s]).start(priority=r % 2)

    def scatter(idx_ref, s):
        for r in range(rows):
            pltpu.make_async_copy(ybuf.at[s, pl.ds(r * rt, rt)], tile_of(y_hbm, idx_ref[0, 0, r]),
                                  ssem.at[s]).start(priority=r % 2)

    def wait_gather(s):
        pltpu.make_async_copy(x_hbm.at[pl.ds(0, rows * rt)], xbuf.at[s], gsem.at[s]).wait()

    def wait_scatter(s):
        pltpu.make_async_copy(ybuf.at[s], y_hbm.at[pl.ds(0, rows * rt)], ssem.at[s]).wait()

    @pl.when(b == 0)
    def _():
        zbuf[...] = jnp.zeros_like(zbuf)
        ybuf[1] = jnp.zeros(ybuf.shape[1:], F32)
        gather(tok_ref, 0)

    @pl.when(b >= nu)
    def _():
        dst = y_hbm.at[pl.ds(pl.multiple_of(b * (rows * rt), rows * rt), rows * rt)]
        cp = pltpu.make_async_copy(zbuf, dst, zsem)
        cp.start()
        cp.wait()

    def used_block(s):
        @pl.when((b == 0) | (be_ref[b] != be_ref[jnp.maximum(b - 1, 0)]))
        def _():
            w1c[...] = w1_ref[0, 0].astype(BF16)
            w2c[...] = w2_ref[0, 0].astype(BF16)

        wait_gather(s)

        @pl.when(b >= 1)
        def _():
            wait_scatter(s)

        gather(tok_next_ref, 1 - s)
        scatter(dst_prev_ref, 1 - s)
        x = _from_token_tiles(xbuf.at[s], rows, d).astype(BF16)
        z = jnp.dot(x, w1c[...], preferred_element_type=F32) + b1_ref[0, 0]
        z_glu = jnp.minimum(z[:, :f2 // 2], SWIGLU_LIMIT)
        z_lin = jnp.clip(z[:, f2 // 2:], -SWIGLU_LIMIT, SWIGLU_LIMIT)
        act = z_glu * _sigmoid(SWIGLU_ALPHA * z_glu) * (z_lin + 1.0)
        y = jnp.dot(act.astype(BF16), w2c[...], preferred_element_type=F32) + b2_ref[0, 0]
        _to_token_tiles(ybuf.at[s], y)

        @pl.when(b == nu - 1)
        def _():
            scatter(dst_ref, s)
            wait_gather(1 - s)
            wait_scatter(1 - s)
            wait_scatter(s)

    for s in range(2):
        pl.when((b < nu) & (b % 2 == s))(functools.partial(used_block, s))


def _moe(f_tiles, meta, counts, layer, w1, b1, w2, b2):
    n_tiles, _, tile = meta.shape
    n_asg = n_tiles * tile * TOP_K
    _, n_e, d, f2 = w1.shape
    rt = d // LANES
    slot_start, block_e, n_used, n_blocks = _moe_plan(counts, n_asg, MOE_ROWS)
    n_slots = n_blocks * MOE_ROWS
    inv = _invert(meta, slot_start, n_slots)
    is_pad = inv < 0
    slot_tok = jnp.where(is_pad, 0, inv // TOP_K).reshape(n_blocks, 1, MOE_ROWS)
    spill = n_asg + jnp.cumsum(is_pad.astype(jnp.int32)) - 1
    slot_dst = jnp.where(is_pad, spill, inv)
    slot_dst = jnp.concatenate([slot_dst, jnp.arange(n_slots - MOE_ROWS, n_slots, dtype=jnp.int32)])
    slot_dst = slot_dst.reshape(n_blocks + 1, 1, MOE_ROWS)

    live = lambda b, nu: jnp.minimum(b, nu[0] - 1)
    smem = lambda imap: pl.BlockSpec((1, 1, MOE_ROWS), imap, memory_space=pltpu.SMEM)
    per_e = lambda shape: pl.BlockSpec((1, 1) + shape, lambda b, be, nu: (layer, be[live(b, nu)], 0, 0))
    buf = pltpu.VMEM((2, MOE_ROWS * rt, LANES), F32)
    grid_spec = pltpu.PrefetchScalarGridSpec(
        num_scalar_prefetch=2,
        grid=(n_blocks,),
        in_specs=[smem(lambda b, be, nu: (live(b, nu), 0, 0)),
                  smem(lambda b, be, nu: (live(b + 1, nu), 0, 0)),
                  smem(lambda b, be, nu: (jnp.where(b == 0, n_blocks, live(b - 1, nu)), 0, 0)),
                  smem(lambda b, be, nu: (live(b, nu), 0, 0)),
                  pl.BlockSpec(memory_space=pl.ANY),
                  per_e((d, f2)), per_e((1, f2)), per_e((f2 // 2, d)), per_e((1, d))],
        out_specs=pl.BlockSpec(memory_space=pl.ANY),
        scratch_shapes=[buf, buf, pltpu.VMEM((MOE_ROWS * rt, LANES), F32),
                        pltpu.VMEM((d, f2), BF16), pltpu.VMEM((f2 // 2, d), BF16),
                        pltpu.SemaphoreType.DMA((2,)), pltpu.SemaphoreType.DMA((2,)),
                        pltpu.SemaphoreType.DMA(())])
    return pl.pallas_call(
        _ffn_kernel,
        grid_spec=grid_spec,
        out_shape=jax.ShapeDtypeStruct((n_slots * rt, LANES), F32),
        compiler_params=_params("arbitrary"),
        name="moe_ffn",
    )(block_e, n_used, slot_tok, slot_tok, slot_dst, slot_dst, f_tiles, w1, b1.reshape(b1.shape[0], n_e, 1, f2),
      w2, b2.reshape(b2.shape[0], n_e, 1, d))


def _combine_experts(y_ref, tw_ref, tile, d):
    rt = d // LANES
    tw = tw_ref[0]
    out = 0.0
    for k in range(TOP_K):
        yk = jnp.concatenate([y_ref[pl.ds(k * rt + c, tile, stride=TOP_K * rt), :] for c in range(rt)],
                             axis=-1)
        out = out + tw[:, k:k + 1] * yk
    return out


def _combine0_kernel(y_ref, tw_ref, h_ref, m0_ref, m1_ref, g_ref, hl_ref, al_ref, ac_ref, *, n_ctx_tiles):
    tile, d = h_ref.shape[1:]
    h2 = h_ref[0] + m0_ref[0][5:6] * _combine_experts(y_ref, tw_ref, tile, d)
    m1 = m1_ref[0]
    a = _rms(h2, g_ref[...]) * (1.0 + m1[1:2]) + m1[0:1]
    is_ctx = pl.program_id(1) < n_ctx_tiles

    @pl.when(is_ctx)
    def _():
        ac_ref[0] = a

    @pl.when(jnp.logical_not(is_ctx))
    def _():
        hl_ref[0] = h2
        al_ref[0] = a


def _combine0(y_asg, top_w, h1, mod0, mod1, g1_next, n_ctx_tiles):
    bsz, t, d = h1.shape
    nt = t // TOK_TILE
    n_ctx = n_ctx_tiles * TOK_TILE
    tok = pl.BlockSpec((1, TOK_TILE, d), lambda b, i: (b, i, 0))
    lat = pl.BlockSpec((1, TOK_TILE, d), lambda b, i: (b, jnp.maximum(i - n_ctx_tiles, 0), 0))
    ctx = pl.BlockSpec((1, TOK_TILE, d), lambda b, i: (b, jnp.minimum(i, n_ctx_tiles - 1), 0))
    return pl.pallas_call(
        functools.partial(_combine0_kernel, n_ctx_tiles=n_ctx_tiles),
        grid=(bsz, nt),
        in_specs=[pl.BlockSpec((TOK_TILE * TOP_K * d // LANES, LANES), lambda b, i: (b * nt + i, 0)),
                  pl.BlockSpec((1, TOK_TILE, TOP_K), lambda b, i: (b * nt + i, 0, 0)),
                  tok, _mod_spec(d, n_ctx_tiles), _mod_spec(d, n_ctx_tiles),
                  pl.BlockSpec((1, d), lambda b, i: (0, 0))],
        out_specs=[lat, lat, ctx],
        out_shape=[jax.ShapeDtypeStruct((bsz, t - n_ctx, d), F32),
                   jax.ShapeDtypeStruct((bsz, t - n_ctx, d), F32),
                   jax.ShapeDtypeStruct((bsz, n_ctx, d), F32)],
        compiler_params=_params("parallel", "parallel"),
        name="combine0",
    )(y_asg, top_w, h1, mod0, mod1, g1_next.reshape(1, d))


def _s5_matrices(a_re, a_im, log_dt, b_re, b_im, c_re, c_im):
    ng, p = a_re.shape[1:]
    gc = b_re.shape[-1]
    ll = S5_CHUNK
    gpb = LANES // gc
    nblk = ng // gpb
    lam_re = jnp.minimum(a_re, S5_EIG_MAX)
    lam_im = a_im
    dt = jnp.exp(log_dt)[..., None]
    j = jnp.arange(ll + 1, dtype=F32).reshape(-1, 1, 1, 1)
    mag = jnp.exp(j * (lam_re * dt))
    pw_re = mag * jnp.cos(j * (lam_im * dt))
    pw_im = mag * jnp.sin(j * (lam_im * dt))
    ab_re, ab_im = pw_re[1], pw_im[1]
    den = lam_re * lam_re + lam_im * lam_im
    coef_re = ((ab_re - 1.0) * lam_re + ab_im * lam_im) / den
    coef_im = (ab_im * lam_re - (ab_re - 1.0) * lam_im) / den
    bb_re = coef_re[..., None] * b_re - coef_im[..., None] * b_im
    bb_im = coef_re[..., None] * b_im + coef_im[..., None] * b_re
    drv_re = pw_re[..., None] * bb_re - pw_im[..., None] * bb_im
    drv_im = pw_re[..., None] * bb_im + pw_im[..., None] * bb_re
    rd_re = c_re * pw_re[:, :, :, None, :] - c_im * pw_im[:, :, :, None, :]
    rd_im = -(c_re * pw_im[:, :, :, None, :] + c_im * pw_re[:, :, :, None, :])
    taps = (jnp.einsum('dgop,jdgpi->jdgoi', c_re, drv_re[:ll], precision=HIGHEST)
            - jnp.einsum('dgop,jdgpi->jdgoi', c_im, drv_im[:ll], precision=HIGHEST))
    r = jnp.arange(ll)
    c_in, c_drv, c_rd = [], [], []
    for d in range(2):
        lag = (r[None, :] - r[:, None]) if d == 0 else (r[:, None] - r[None, :])
        tp = jnp.where((lag >= 0)[:, :, None, None, None],
                       taps[:, d][jnp.clip(lag, 0, ll - 1)], 0.0)
        tp = tp.reshape(ll, ll, nblk, gpb, gc, gc).transpose(2, 0, 3, 5, 1, 4)
        c_in.append(tp.reshape(nblk, ll * LANES, ll * gc))
        steps = (ll - 1 - r) if d == 0 else r
        dr = jnp.stack([drv_re[:, d][steps], drv_im[:, d][steps]], axis=1)
        dr = dr.reshape(ll, 2, nblk, gpb, p, gc).transpose(2, 0, 3, 5, 1, 4)
        c_drv.append(dr.reshape(nblk, ll * LANES, 2 * p))
        steps = (r + 1) if d == 0 else (ll - r)
        rd = jnp.stack([rd_re[:, d][steps], rd_im[:, d][steps]], axis=1)
        rd = rd.reshape(ll, 2, nblk, gpb, gc, p).transpose(2, 1, 3, 5, 0, 4)
        c_rd.append(rd.reshape(nblk, 2 * gpb * p, ll * gc))

    def expand(compact, row_unit, col_unit):
        compact = jnp.stack(compact).astype(BF16)
        n_r, n_c = compact.shape[2], compact.shape[3] * gpb
        col = jnp.arange(n_c)
        src = (col // (col_unit * gpb)) * col_unit + col % col_unit
        spread = (jnp.arange(n_c // gpb)[:, None] == src[None, :]).astype(BF16)
        full = jnp.einsum('dbrk,kc->dbrc', compact, spread, preferred_element_type=F32)
        same = ((jnp.arange(n_r) // row_unit) % gpb)[:, None] == ((col // col_unit) % gpb)[None, :]
        return jnp.where(same, full, 0.0).astype(BF16)

    a8 = jnp.stack([pw_re[ll], pw_im[ll]], axis=1).reshape(2, 2, nblk, gpb * p).transpose(0, 2, 1, 3)
    return expand(c_in, gc, gc), expand(c_drv, gc, p), expand(c_rd, p, gc), a8


def _s5_kernel(uc_ref, ul_ref, min_ref, mdrv_ref, mrd_ref, a8_ref, y_ref, v_ref, st_ref, *, n_ctx_cols):
    bsz, n_oct, ll, n_col, lanes = ul_ref.shape
    n_rows = bsz * n_oct * n_col
    half = st_ref.shape[-1] // 2
    d = pl.program_id(0)
    k = pl.program_id(2)

    @pl.when(k == 0)
    def _():
        st_ref[...] = jnp.zeros_like(st_ref)

    def chunks(u_ref):
        return jnp.concatenate([u_ref[:, :, r].reshape(n_rows, lanes) for r in range(ll)], axis=-1)

    x = jnp.where(k == 0, chunks(uc_ref), chunks(ul_ref)).astype(BF16)
    inj_all = jnp.dot(x, mdrv_ref[0, 0], preferred_element_type=F32)
    n_pl = v_ref.shape[0]
    for c in range(n_pl):
        v_ref[c] = inj_all[:, c * lanes:(c + 1) * lanes]

    a_re = jnp.broadcast_to(a8_ref[0, 0, 0:1], (bsz, half))
    a_im = jnp.broadcast_to(a8_ref[0, 0, 1:2], (bsz, half))
    def scan(n_steps):
        def step(i, carry):
            s_re, s_im = carry
            i = jnp.where(d == 0, i, n_steps - 1 - i)
            row = (i % n_oct) * n_col + i // n_oct
            rows = pl.ds(row, bsz, stride=n_oct * n_col)
            inj = jnp.concatenate([v_ref[c, rows, :] for c in range(n_pl)], axis=-1)
            for c in range(n_pl // 2):
                v_ref[c, rows, :] = s_re[:, c * lanes:(c + 1) * lanes]
                v_ref[n_pl // 2 + c, rows, :] = s_im[:, c * lanes:(c + 1) * lanes]
            return (a_re * s_re - a_im * s_im + inj[:, :half],
                    a_re * s_im + a_im * s_re + inj[:, half:])

        s_re, s_im = lax.fori_loop(0, n_steps, step, (st_ref[:, :half], st_ref[:, half:]), unroll=2)
        st_ref[:, :half] = s_re
        st_ref[:, half:] = s_im

    @pl.when(k == 0)
    def _():
        scan(n_ctx_cols * n_oct)

    @pl.when(k > 0)
    def _():
        scan(n_col * n_oct)

    s_start = jnp.concatenate([v_ref[c] for c in range(n_pl)], axis=-1).astype(BF16)
    y = (jnp.dot(x, min_ref[0, 0], preferred_element_type=F32)
         + jnp.dot(s_start, mrd_ref[0, 0], preferred_element_type=F32))
    for r in range(ll):
        y_ref[0, :, :, r] = y[:, r * lanes:(r + 1) * lanes].reshape(bsz, n_oct, n_col, lanes)


def _s5_scan(a_lat, a_ctx, mats):
    bsz, t, d = a_lat.shape
    n_ctx = a_ctx.shape[1]
    m_in, m_drv, m_rd, a8 = mats
    n_rows = t // GRID_W
    n_oct = n_rows // S5_CHUNK
    n_ctx_cols = n_ctx // n_rows
    u_lat = a_lat.reshape(bsz, n_oct, S5_CHUNK, GRID_W, d)
    u_ctx = a_ctx.reshape(bsz, n_ctx_cols, n_rows, d).transpose(0, 2, 1, 3)
    u_ctx = jnp.pad(u_ctx, ((0, 0), (0, 0), (0, S5_COLS - n_ctx_cols), (0, 0)))
    u_ctx = u_ctx.reshape(bsz, n_oct, S5_CHUNK, S5_COLS, d)
    n_lat = GRID_W // S5_COLS
    nblk = d // LANES

    def lat_tile(dd, k):
        kk = jnp.maximum(k, 1) - 1
        return jnp.where(dd == 0, kk, n_lat - 1 - kk)

    blk = (bsz, n_oct, S5_CHUNK, S5_COLS, LANES)
    wspec = lambda shape: pl.BlockSpec((1, 1) + shape, lambda dd, j, k: (dd, j, 0, 0))
    kl = S5_CHUNK * LANES
    ns = m_drv.shape[-1]
    y = pl.pallas_call(
        functools.partial(_s5_kernel, n_ctx_cols=n_ctx_cols),
        grid=(2, nblk, n_lat + 1),
        in_specs=[pl.BlockSpec(blk, lambda dd, j, k: (0, 0, 0, 0, j)),
                  pl.BlockSpec(blk, lambda dd, j, k: (0, 0, 0, lat_tile(dd, k), j)),
                  wspec((kl, kl)), wspec((kl, ns)), wspec((ns, kl)),
                  pl.BlockSpec((1, 1, 2, ns // 2), lambda dd, j, k: (dd, j, 0, 0))],
        out_specs=pl.BlockSpec((1,) + blk, lambda dd, j, k: (dd, 0, 0, 0, lat_tile(dd, k), j)),
        out_shape=jax.ShapeDtypeStruct((2,) + u_lat.shape, F32),
        scratch_shapes=[pltpu.VMEM((ns // LANES, bsz * n_oct * S5_COLS, LANES), F32),
                        pltpu.VMEM((bsz, ns), F32)],
        compiler_params=_params("parallel", "parallel", "arbitrary"),
        name="s5_scan",
    )(u_ctx, u_lat, m_in, m_drv, m_rd, a8)
    return y.reshape(2, bsz, n_rows, GRID_W, d)


def _s5_glu_kernel(y_ref, u_ref, h_ref, m_ref, dsk_ref, w_ref, bg_ref, g2_ref, rw_ref, rb_ref,
                   h1_ref, f_ref, tw_ref, meta_ref, cnt_ref, run_ref):
    d = h_ref.shape[-1]
    n = h_ref.shape[1] * h_ref.shape[2]
    y = (y_ref[0, 0] + y_ref[1, 0] + dsk_ref[...] * u_ref[0]).reshape(n, d)
    z = jnp.dot(jax.nn.gelu(y).astype(BF16), w_ref[...], preferred_element_type=F32) + bg_ref[...]
    y = z[:, :d] * _sigmoid(z[:, d:])
    first = (pl.program_id(0) == 0) & (pl.program_id(1) == 0)
    _mixer_tail(first, h_ref[0].reshape(n, d), y, m_ref[0], g2_ref[...], rw_ref[...], rb_ref[...],
                run_ref, h1_ref, f_ref, tw_ref.at[0], meta_ref, cnt_ref)


def _s5_glu(y, a_lat, h_lat, mod, d_skip, w_glu, b_glu, g2, rw, rb):
    bsz, t, d = h_lat.shape
    n_rows = t // GRID_W
    n_e = rw.shape[-1]
    n_tok = n_rows * S5_COLS
    nt = GRID_W // S5_COLS
    lat = pl.BlockSpec((1, n_rows, S5_COLS, d), lambda b, i: (b, 0, i, 0))
    tail_specs, tail_shapes = _tail_outputs(bsz * nt, n_tok, d, n_e, lambda b, i: b * nt + i)
    const = lambda shape: pl.BlockSpec(shape, lambda b, i: (0,) * len(shape))
    return pl.pallas_call(
        _s5_glu_kernel,
        grid=(bsz, nt),
        in_specs=[pl.BlockSpec((2, 1, n_rows, S5_COLS, d), lambda b, i: (0, b, 0, i, 0)),
                  lat, lat, pl.BlockSpec((1, 6, d), lambda b, i: (2 * b + 1, 0, 0)),
                  const((1, d)), const((d, 2 * d)), const((1, 2 * d)), const((1, d)),
                  const((d, n_e)), const((1, n_e))],
        out_specs=[lat] + tail_specs,
        out_shape=[jax.ShapeDtypeStruct((bsz, n_rows, GRID_W, d), F32)] + tail_shapes,
        scratch_shapes=[pltpu.VMEM((1, n_e), F32)],
        compiler_params=_params("arbitrary", "arbitrary"),
        name="s5_glu",
    )(y, a_lat.reshape(bsz, n_rows, GRID_W, d), h_lat.reshape(bsz, n_rows, GRID_W, d), mod,
      d_skip.reshape(1, d), w_glu.astype(BF16), b_glu.reshape(1, 2 * d),
      g2.reshape(1, d), rw, rb.reshape(1, n_e))


def _combine1_kernel(y_ref, tw_ref, h_ref, m_ref, g_ref, o_ref):
    d = h_ref.shape[-1]
    n = h_ref.shape[1] * h_ref.shape[2]
    h2 = h_ref[0].reshape(n, d) + m_ref[0][5:6] * _combine_experts(y_ref, tw_ref, n, d)
    o_ref[0] = _rms(h2, g_ref[...]).reshape(o_ref.shape[1:])


def _combine1(y_asg, top_w, h1, mod, final_g):
    bsz, n_rows, _, d = h1.shape
    nt = GRID_W // S5_COLS
    n_tok = n_rows * S5_COLS
    lat = pl.BlockSpec((1, n_rows, S5_COLS, d), lambda b, i: (b, 0, i, 0))
    return pl.pallas_call(
        _combine1_kernel,
        grid=(bsz, nt),
        in_specs=[pl.BlockSpec((n_tok * TOP_K * d // LANES, LANES), lambda b, i: (b * nt + i, 0)),
                  pl.BlockSpec((1, n_tok, TOP_K), lambda b, i: (b * nt + i, 0, 0)),
                  lat, pl.BlockSpec((1, 6, d), lambda b, i: (2 * b + 1, 0, 0)),
                  pl.BlockSpec((1, d), lambda b, i: (0, 0))],
        out_specs=lat,
        out_shape=jax.ShapeDtypeStruct(h1.shape, F32),
        compiler_params=_params("parallel", "parallel"),
        name="combine1",
    )(y_asg, top_w, h1, mod, final_g.reshape(1, d))


def kernel(x, c, ctx, c_ctx, mod_w, mod_b, norm1_g, norm2_g, hg_w_in, hg_lb_logits, hg_norm_g,
           hg_w_out, s5_a_re, s5_a_im, s5_log_dt, s5_b_re, s5_b_im, s5_c_re, s5_c_im, s5_d, s5_w_glu,
           s5_b_glu, router_w, router_b, moe_w1, moe_b1, moe_w2, moe_b2, final_g):
    bsz, seq, d = x.shape
    n_ctx = ctx.shape[1]
    assert mod_w.shape[0] == 2 and n_ctx % TOK_TILE == 0 and seq % TOK_TILE == 0
    n_ctx_tiles = n_ctx // TOK_TILE
    lb_all = jnp.cumsum(jax.nn.softmax(hg_lb_logits.astype(F32), axis=0), axis=0)
    mod = _modulation(c, c_ctx, mod_w, mod_b)
    h = jnp.concatenate([ctx, x], axis=1)

    qs, v, sg, kf, kb, lff, lfb = _hg_proj(h, mod[0], norm1_g[0], hg_w_in[0], lb_all[0], n_ctx_tiles)
    o_f, o_b = _gla(qs, v, kf, kb, lff, lfb, n_ctx // HG_CHUNK)
    h1, f, top_w, meta, counts = _hg_readout(o_f, o_b, sg, h, mod[0], hg_norm_g[0], hg_w_out[0],
                                             norm2_g[0], router_w[0], router_b[0], n_ctx_tiles)
    y_asg = _moe(f, meta, counts, 0, moe_w1, moe_b1, moe_w2, moe_b2)
    h_lat, a_lat, a_ctx = _combine0(y_asg, top_w, h1, mod[0], mod[1], norm1_g[1], n_ctx_tiles)

    mats = _s5_matrices(s5_a_re[0], s5_a_im[0], s5_log_dt[0], s5_b_re[0], s5_b_im[0],
                        s5_c_re[0], s5_c_im[0])
    y_s5 = _s5_scan(a_lat, a_ctx, mats)
    h1, f, top_w, meta, counts = _s5_glu(y_s5, a_lat, h_lat, mod[1], s5_d[0], s5_w_glu[0], s5_b_glu[0],
                                         norm2_g[1], router_w[1], router_b[1])
    y_asg = _moe(f, meta, counts, 1, moe_w1, moe_b1, moe_w2, moe_b2)
    out = _combine1(y_asg, top_w, h1, mod[1], final_g)
    return out.reshape(bsz, seq, d)
```

```python
import functools

import jax
import jax.numpy as jnp
from jax import lax
from jax.experimental import pallas as pl
from jax.experimental.pallas import tpu as pltpu

F32 = jnp.float32
BF16 = jnp.bfloat16
HIGHEST = lax.Precision.HIGHEST

RMS_EPS = 1e-6
GRID_W = 64
HEAD_DIM = 128
HG_CHUNK = 64
HG_BATCH = 2
S5_GROUP = 16
S5_STATE = 64
S5_EIG_MAX = -1e-4
S5_CHUNK = 8
S5_COLS = 8
LANES = 128
N_EXPERTS = 32
TOP_K = 4
SWIGLU_ALPHA = 1.702
SWIGLU_LIMIT = 7.0
TOK_TILE = 256
MOE_ROWS = 256
VMEM_LIMIT = 56 * 1024 * 1024


def _params(*sem):
    return pltpu.CompilerParams(dimension_semantics=sem, vmem_limit_bytes=VMEM_LIMIT)


def _rms(x, g):
    return x * lax.rsqrt(jnp.mean(x * x, axis=-1, keepdims=True) + RMS_EPS) * g


def _sigmoid(x):
    return 1.0 / (1.0 + jnp.exp(-x))


def _silu(x):
    return x * _sigmoid(x)


def _to_token_tiles(ref, x):
    n, d = x.shape
    for c in range(d // LANES):
        ref[pl.ds(c, n, stride=d // LANES), :] = x[:, c * LANES:(c + 1) * LANES]


def _from_token_tiles(ref, n, d):
    return jnp.concatenate([ref[pl.ds(c, n, stride=d // LANES), :] for c in range(d // LANES)], axis=-1)


def _mod_kernel(c_ref, w_ref, b_ref, o_ref):
    s = _silu(c_ref[...])
    o_ref[0] = jnp.dot(s, w_ref[0], precision=HIGHEST, preferred_element_type=F32) + b_ref[0]


def _modulation(c, c_ctx, mod_w, mod_b):
    bsz, d = c.shape
    depth = mod_w.shape[0]
    rows = jnp.concatenate([c, c_ctx[None], jnp.zeros((16 - bsz - 1, d), F32)], axis=0)
    bn = 6 * d // 4
    out = pl.pallas_call(
        _mod_kernel,
        grid=(depth, 4),
        in_specs=[pl.BlockSpec((16, d), lambda l, j: (0, 0)),
                  pl.BlockSpec((1, d, bn), lambda l, j: (l, 0, j)),
                  pl.BlockSpec((1, 1, bn), lambda l, j: (l, 0, j))],
        out_specs=pl.BlockSpec((1, 16, bn), lambda l, j: (l, 0, j)),
        out_shape=jax.ShapeDtypeStruct((depth, 16, 6 * d), F32),
        compiler_params=_params("parallel", "parallel"),
        name="modulation",
    )(rows, mod_w, mod_b.reshape(depth, 1, 6 * d))
    m_lat = out[:, :bsz].reshape(depth, bsz, 1, 6, d)
    m_ctx = jnp.broadcast_to(out[:, bsz].reshape(depth, 1, 1, 6, d), m_lat.shape)
    return jnp.concatenate([m_ctx, m_lat], axis=2).reshape(depth, 2 * bsz, 6, d)


def _hg_proj_kernel(h_ref, m_ref, g_ref, w_ref, lb_ref,
                    qs_ref, v_ref, sg_ref, kf_ref, kb_ref, lff_ref, lfb_ref):
    d = h_ref.shape[-1]
    m = m_ref[0]
    a = _rms(h_ref[0], g_ref[...]) * (1.0 + m[1:2]) + m[0:1]
    p = jnp.dot(a.astype(BF16), w_ref[...], preferred_element_type=F32)
    qs_ref[0] = _silu(p[:, 0:d]).astype(BF16)
    v_ref[0] = p[:, d:2 * d].astype(BF16)
    sg_ref[0] = _silu(p[:, 4 * d:5 * d]).astype(BF16)
    for di, (k_ref, lf_ref) in enumerate(((kf_ref, lff_ref), (kb_ref, lfb_ref))):
        lb = lb_ref[di:di + 1]
        f = lb + (1.0 - lb) * _sigmoid(p[:, (2 + di) * d:(3 + di) * d])
        k_ref[0] = (1.0 - f).astype(BF16)
        lf_ref[0] = jnp.log(f)


def _mod_spec(d, n_ctx_tiles):
    return pl.BlockSpec((1, 6, d), lambda b, i: (2 * b + (i >= n_ctx_tiles).astype(jnp.int32), 0, 0))


def _hg_proj(h, mod, g1, w_in, lb, n_ctx_tiles):
    bsz, t, d = h.shape
    tok = pl.BlockSpec((1, TOK_TILE, d), lambda b, i: (b, i, 0))
    bf = jax.ShapeDtypeStruct((bsz, t, d), BF16)
    ff = jax.ShapeDtypeStruct((bsz, t, d), F32)
    return pl.pallas_call(
        _hg_proj_kernel,
        grid=(bsz, t // TOK_TILE),
        in_specs=[tok, _mod_spec(d, n_ctx_tiles),
                  pl.BlockSpec((1, d), lambda b, i: (0, 0)),
                  pl.BlockSpec((d, 5 * d), lambda b, i: (0, 0)),
                  pl.BlockSpec((2, d), lambda b, i: (0, 0))],
        out_specs=[tok] * 7,
        out_shape=[bf, bf, bf, bf, bf, ff, ff],
        compiler_params=_params("parallel", "parallel"),
        name="hg_proj",
    )(h, mod, g1.reshape(1, d), w_in.astype(BF16), lb)


def _split3(x):
    hi = x.astype(BF16)
    r = x - hi.astype(F32)
    mid = r.astype(BF16)
    lo = (r - mid.astype(F32)).astype(BF16)
    return hi, mid, lo


def _gla_kernel(qf_ref, vf_ref, kf_ref, lf_ref, qb_ref, vb_ref, kb_ref, lb_ref,
                of_ref, ob_ref, sf_ref, sb_ref):
    c = HG_CHUNK
    n_b = qf_ref.shape[0]
    n_heads = sf_ref.shape[0] // n_b

    @pl.when(pl.program_id(1) == 0)
    def _():
        sf_ref[...] = jnp.zeros_like(sf_ref)
        sb_ref[...] = jnp.zeros_like(sb_ref)

    row = lax.broadcasted_iota(jnp.int32, (c, c), 0)
    col = lax.broadcasted_iota(jnp.int32, (c, c), 1)
    dirs = ((qf_ref, vf_ref, kf_ref, lf_ref, of_ref, sf_ref, col <= row, c // 2 - 1, c - 1),
            (qb_ref, vb_ref, kb_ref, lb_ref, ob_ref, sb_ref, col >= row, c // 2, 0))
    for bb, (q_ref, v_ref, k_ref, l_ref, o_ref, s_ref, keep, r_ref, r_last) in (
            (bb, dr) for bb in range(n_b) for dr in dirs):
        tri = keep.astype(BF16)
        b = sum(jnp.dot(tri, part, preferred_element_type=F32) for part in _split3(l_ref[bb]))
        b_ref = b[r_ref:r_ref + 1]
        b_last = b[r_last:r_last + 1]
        a_in = q_ref[bb].astype(F32) * jnp.exp(b - b_ref)
        k_in = k_ref[bb].astype(F32) * jnp.exp(b_ref - b)
        q_st = (a_in * jnp.exp(b_ref)).astype(BF16)
        k_st = (k_in * jnp.exp(b_last - b_ref)).astype(BF16)
        a_in = a_in.astype(BF16)
        k_in = k_in.astype(BF16)
        decay = jnp.exp(b_last)
        v = v_ref[bb]
        for h in range(n_heads):
            sl = slice(h * HEAD_DIM, (h + 1) * HEAD_DIM)
            sc = lax.dot_general(a_in[:, sl], k_in[:, sl], (((1,), (1,)), ((), ())),
                                 preferred_element_type=F32)
            sc = jnp.where(keep, sc, 0.0).astype(BF16)
            o = jnp.dot(sc, v[:, sl], preferred_element_type=F32)
            st = s_ref[bb * n_heads + h]
            o = o + lax.dot_general(q_st[:, sl], st.astype(BF16), (((1,), (1,)), ((), ())),
                                    preferred_element_type=F32)
            s_ref[bb * n_heads + h] = st * decay[:, sl] + lax.dot_general(
                v[:, sl], k_st[:, sl], (((0,), (0,)), ((), ())), preferred_element_type=F32)
            o_ref[bb, :, sl] = o


def _gla(qs, v, kf, kb, lff, lfb, n_ctx_chunks):
    bsz, t, d = qs.shape
    n = t // HG_CHUNK
    n_heads = d // HEAD_DIM

    def rev(j):
        return jnp.where(j < n_ctx_chunks, n_ctx_chunks - 1 - j, n + n_ctx_chunks - 1 - j)

    fwd = pl.BlockSpec((HG_BATCH, HG_CHUNK, d), lambda b, j: (b, j, 0))
    bwd = pl.BlockSpec((HG_BATCH, HG_CHUNK, d), lambda b, j: (b, rev(j), 0))
    out = jax.ShapeDtypeStruct((bsz, t, d), F32)
    state = pltpu.VMEM((HG_BATCH * n_heads, HEAD_DIM, HEAD_DIM), F32)
    return pl.pallas_call(
        _gla_kernel,
        grid=(bsz // HG_BATCH, n),
        in_specs=[fwd, fwd, fwd, fwd, bwd, bwd, bwd, bwd],
        out_specs=[fwd, bwd],
        out_shape=[out, out],
        scratch_shapes=[state, state],
        compiler_params=_params("parallel", "arbitrary"),
        name="gla",
    )(qs, v, kf, lff, qs, v, kb, lfb)


def _mixer_tail(first, h, y, m, g2, rw, rb, run_ref, h1_ref, f_ref, tw_ref, meta_ref, cnt_ref):
    h1 = h + m[2:3] * y
    f = _rms(h1, g2) * (1.0 + m[4:5]) + m[3:4]
    h1_ref[...] = h1.reshape(h1_ref.shape)
    _to_token_tiles(f_ref, f)
    f_hi, f_lo, _ = _split3(f)
    w_hi, w_lo, _ = _split3(rw)
    logits = (jnp.dot(f_hi, w_hi, preferred_element_type=F32) + jnp.dot(f_hi, w_lo, preferred_element_type=F32)
              + jnp.dot(f_lo, w_hi, preferred_element_type=F32)) + rb
    n, n_e = logits.shape
    lane = lax.broadcasted_iota(jnp.int32, logits.shape, 1).astype(F32)
    vals, idxs, hots = [], [], []
    for _ in range(TOP_K):
        mx = jnp.max(logits, axis=-1, keepdims=True)
        ix = jnp.min(jnp.where(logits == mx, lane, float(n_e)), axis=-1, keepdims=True)
        hot = lane == ix
        vals.append(mx)
        idxs.append(ix)
        hots.append(hot)
        logits = jnp.where(hot, -jnp.inf, logits)
    es = [jnp.exp(x - vals[0]) for x in vals]
    tot = sum(es)
    for k in range(TOP_K):
        tw_ref[:, k:k + 1] = es[k] / tot

    @pl.when(first)
    def _():
        run_ref[...] = jnp.zeros_like(run_ref)

    picked = sum(hot.astype(F32) for hot in hots)
    r_i = lax.broadcasted_iota(jnp.int32, (n, n), 0)
    c_i = lax.broadcasted_iota(jnp.int32, (n, n), 1)
    earlier = jnp.dot((c_i < r_i).astype(BF16), picked.astype(BF16), preferred_element_type=F32)
    rank = earlier + run_ref[...]
    total = run_ref[...] + jnp.sum(picked, axis=0, keepdims=True)
    run_ref[...] = total
    cnt_ref[...] = total
    col = lax.broadcasted_iota(jnp.int32, (n, LANES), 1)
    z = jnp.zeros((n, LANES), F32)
    for k in range(TOP_K):
        pos = jnp.sum(jnp.where(hots[k], rank, 0.0), axis=-1, keepdims=True)
        z = jnp.where(col == k, idxs[k], z)
        z = jnp.where(col == TOP_K + k, pos, z)
    meta_ref[0] = z.T[0:2 * TOP_K].astype(jnp.int32)


def _tail_outputs(n_tiles, tile, d, n_e, index):
    rt = d // LANES
    specs = [pl.BlockSpec((tile * rt, LANES), lambda b, i: (index(b, i), 0)),
             pl.BlockSpec((1, tile, TOP_K), lambda b, i: (index(b, i), 0, 0)),
             pl.BlockSpec((1, 2 * TOP_K, tile), lambda b, i: (index(b, i), 0, 0)),
             pl.BlockSpec((1, n_e), lambda b, i: (0, 0))]
    shapes = [jax.ShapeDtypeStruct((n_tiles * tile * rt, LANES), F32),
              jax.ShapeDtypeStruct((n_tiles, tile, TOP_K), F32),
              jax.ShapeDtypeStruct((n_tiles, 2 * TOP_K, tile), jnp.int32),
              jax.ShapeDtypeStruct((1, n_e), F32)]
    return specs, shapes


def _hg_readout_kernel(of_ref, ob_ref, sg_ref, h_ref, m_ref, ng_ref, w_ref, g2_ref, rw_ref, rb_ref,
                       h1_ref, f_ref, tw_ref, meta_ref, cnt_ref, run_ref):
    d = h_ref.shape[-1]
    o = of_ref[0] + ob_ref[0]
    parts = []
    for h in range(d // HEAD_DIM):
        oh = o[:, h * HEAD_DIM:(h + 1) * HEAD_DIM]
        parts.append(oh * lax.rsqrt(jnp.mean(oh * oh, axis=-1, keepdims=True) + RMS_EPS))
    o = jnp.concatenate(parts, axis=-1) * ng_ref[...]
    y = jnp.dot((o * sg_ref[0].astype(F32)).astype(BF16), w_ref[...], preferred_element_type=F32)
    first = (pl.program_id(0) == 0) & (pl.program_id(1) == 0)
    _mixer_tail(first, h_ref[0], y, m_ref[0], g2_ref[...], rw_ref[...], rb_ref[...], run_ref,
                h1_ref, f_ref, tw_ref.at[0], meta_ref, cnt_ref)


def _hg_readout(o_f, o_b, sg, h, mod, norm_g, w_out, g2, rw, rb, n_ctx_tiles):
    bsz, t, d = h.shape
    n_e = rw.shape[-1]
    nt = t // TOK_TILE
    tok = pl.BlockSpec((1, TOK_TILE, d), lambda b, i: (b, i, 0))
    const = lambda shape: pl.BlockSpec(shape, lambda b, i: (0,) * len(shape))
    tail_specs, tail_shapes = _tail_outputs(bsz * nt, TOK_TILE, d, n_e, lambda b, i: b * nt + i)
    return pl.pallas_call(
        _hg_readout_kernel,
        grid=(bsz, nt),
        in_specs=[tok, tok, tok, tok, _mod_spec(d, n_ctx_tiles),
                  const((1, d)), const((d, d)), const((1, d)), const((d, n_e)), const((1, n_e))],
        out_specs=[tok] + tail_specs,
        out_shape=[jax.ShapeDtypeStruct((bsz, t, d), F32)] + tail_shapes,
        scratch_shapes=[pltpu.VMEM((1, n_e), F32)],
        compiler_params=_params("arbitrary", "arbitrary"),
        name="hg_readout",
    )(o_f, o_b, sg, h, mod, norm_g.reshape(1, d), w_out.astype(BF16), g2.reshape(1, d),
      rw, rb.reshape(1, n_e))


def _moe_plan(counts, n_asg, rows):
    counts = counts.reshape(-1).astype(jnp.int32)
    padded = (counts + rows - 1) // rows * rows
    pad_end = jnp.cumsum(padded)
    n_blocks = -(-n_asg // rows) + N_EXPERTS
    block_pos = jnp.arange(n_blocks, dtype=jnp.int32) * rows
    block_e = jnp.minimum(jnp.sum((block_pos[:, None] >= pad_end[None, :]).astype(jnp.int32), axis=1),
                          N_EXPERTS - 1)
    n_used = (pad_end[-1:] // rows).astype(jnp.int32)
    return pad_end - padded, block_e, n_used, n_blocks


def _invert_kernel(dest_ref, inv_ref):
    i = pl.program_id(0)
    tile = dest_ref.shape[2]

    @pl.when(i == 0)
    def _():
        def init(s, carry):
            inv_ref[s] = -1
            return carry
        lax.fori_loop(0, inv_ref.shape[0], init, 0, unroll=8)

    base = i * (tile * TOP_K)
    for r in range(tile):
        for k in range(TOP_K):
            inv_ref[dest_ref[0, k, r]] = base + (r * TOP_K + k)


def _invert(meta, slot_start, n_slots):
    n_tiles, _, tile = meta.shape
    hot = meta[:, :TOP_K, :, None] == jnp.arange(N_EXPERTS, dtype=jnp.int32)
    dest = meta[:, TOP_K:] + jnp.sum(jnp.where(hot, slot_start, 0), axis=-1)
    return pl.pallas_call(
        _invert_kernel,
        grid=(n_tiles,),
        in_specs=[pl.BlockSpec((1, TOP_K, tile), lambda i: (i, 0, 0), memory_space=pltpu.SMEM)],
        out_specs=pl.BlockSpec(memory_space=pltpu.SMEM),
        out_shape=jax.ShapeDtypeStruct((n_slots,), jnp.int32),
        compiler_params=_params("arbitrary"),
        name="moe_invert",
    )(dest.astype(jnp.int32))


def _ffn_kernel(be_ref, nu_ref, tok_ref, tok_next_ref, dst_prev_ref, dst_ref, x_hbm,
                w1_ref, b1_ref, w2_ref, b2_ref, y_hbm, xbuf, ybuf, zbuf, w1c, w2c, gsem, ssem, zsem):
    b = pl.program_id(0)
    nu = nu_ref[0]
    d, f2 = w1c.shape
    rt = d // LANES
    rows = xbuf.shape[1] // rt

    def tile_of(ref, idx):
        return ref.at[pl.ds(pl.multiple_of(idx * rt, rt), rt)]

    def gather(idx_ref, s):
        for r in range(rows):
            pltpu.make_async_copy(tile_of(x_hbm, idx_ref[0, 0, r]), xbuf.at[s, pl.ds(r * rt, rt)],
                                  gsem.at[s]).start(priority=r % 2)

    def scatter(idx_ref, s):
        for r in range(rows):
            pltpu.make_async_copy(ybuf.at[s, pl.ds(r * rt, rt)], tile_of(y_hbm, idx_ref[0, 0, r]),
                                  ssem.at[s]).start(priority=r % 2)

    def wait_gather(s):
        pltpu.make_async_copy(x_hbm.at[pl.ds(0, rows * rt)], xbuf.at[s], gsem.at[s]).wait()

    def wait_scatter(s):
        pltpu.make_async_copy(ybuf.at[s], y_hbm.at[pl.ds(0, rows * rt)], ssem.at[s]).wait()

    @pl.when(b == 0)
    def _():
        zbuf[...] = jnp.zeros_like(zbuf)
        ybuf[1] = jnp.zeros(ybuf.shape[1:], F32)
        gather(tok_ref, 0)

    @pl.when(b >= nu)
    def _():
        dst = y_hbm.at[pl.ds(pl.multiple_of(b * (rows * rt), rows * rt), rows * rt)]
        cp = pltpu.make_async_copy(zbuf, dst, zsem)
        cp.start()
        cp.wait()

    def used_block(s):
        @pl.when((b == 0) | (be_ref[b] != be_ref[jnp.maximum(b - 1, 0)]))
        def _():
            w1c[...] = w1_ref[0, 0].astype(BF16)
            w2c[...] = w2_ref[0, 0].astype(BF16)

        wait_gather(s)

        @pl.when(b >= 1)
        def _():
            wait_scatter(s)

        gather(tok_next_ref, 1 - s)
        scatter(dst_prev_ref, 1 - s)
        x = _from_token_tiles(xbuf.at[s], rows, d).astype(BF16)
        z = jnp.dot(x, w1c[...], preferred_element_type=F32) + b1_ref[0, 0]
        z_glu = jnp.minimum(z[:, :f2 // 2], SWIGLU_LIMIT)
        z_lin = jnp.clip(z[:, f2 // 2:], -SWIGLU_LIMIT, SWIGLU_LIMIT)
        act = z_glu * _sigmoid(SWIGLU_ALPHA * z_glu) * (z_lin + 1.0)
        y = jnp.dot(act.astype(BF16), w2c[...], preferred_element_type=F32) + b2_ref[0, 0]
        _to_token_tiles(ybuf.at[s], y)

        @pl.when(b == nu - 1)
        def _():
            scatter(dst_ref, s)
            wait_gather(1 - s)
            wait_scatter(1 - s)
            wait_scatter(s)

    for s in range(2):
        pl.when((b < nu) & (b % 2 == s))(functools.partial(used_block, s))


def _moe(f_tiles, meta, counts, layer, w1, b1, w2, b2):
    n_tiles, _, tile = meta.shape
    n_asg = n_tiles * tile * TOP_K
    _, n_e, d, f2 = w1.shape
    rt = d // LANES
    slot_start, block_e, n_used, n_blocks = _moe_plan(counts, n_asg, MOE_ROWS)
    n_slots = n_blocks * MOE_ROWS
    inv = _invert(meta, slot_start, n_slots)
    is_pad = inv < 0
    slot_tok = jnp.where(is_pad, 0, inv // TOP_K).reshape(n_blocks, 1, MOE_ROWS)
    spill = n_asg + jnp.cumsum(is_pad.astype(jnp.int32)) - 1
    slot_dst = jnp.where(is_pad, spill, inv)
    slot_dst = jnp.concatenate([slot_dst, jnp.arange(n_slots - MOE_ROWS, n_slots, dtype=jnp.int32)])
    slot_dst = slot_dst.reshape(n_blocks + 1, 1, MOE_ROWS)

    live = lambda b, nu: jnp.minimum(b, nu[0] - 1)
    smem = lambda imap: pl.BlockSpec((1, 1, MOE_ROWS), imap, memory_space=pltpu.SMEM)
    per_e = lambda shape: pl.BlockSpec((1, 1) + shape, lambda b, be, nu: (layer, be[live(b, nu)], 0, 0))
    buf = pltpu.VMEM((2, MOE_ROWS * rt, LANES), F32)
    grid_spec = pltpu.PrefetchScalarGridSpec(
        num_scalar_prefetch=2,
        grid=(n_blocks,),
        in_specs=[smem(lambda b, be, nu: (live(b, nu), 0, 0)),
                  smem(lambda b, be, nu: (live(b + 1, nu), 0, 0)),
                  smem(lambda b, be, nu: (jnp.where(b == 0, n_blocks, live(b - 1, nu)), 0, 0)),
                  smem(lambda b, be, nu: (live(b, nu), 0, 0)),
                  pl.BlockSpec(memory_space=pl.ANY),
                  per_e((d, f2)), per_e((1, f2)), per_e((f2 // 2, d)), per_e((1, d))],
        out_specs=pl.BlockSpec(memory_space=pl.ANY),
        scratch_shapes=[buf, buf, pltpu.VMEM((MOE_ROWS * rt, LANES), F32),
                        pltpu.VMEM((d, f2), BF16), pltpu.VMEM((f2 // 2, d), BF16),
                        pltpu.SemaphoreType.DMA((2,)), pltpu.SemaphoreType.DMA((2,)),
                        pltpu.SemaphoreType.DMA(())])
    return pl.pallas_call(
        _ffn_kernel,
        grid_spec=grid_spec,
        out_shape=jax.ShapeDtypeStruct((n_slots * rt, LANES), F32),
        compiler_params=_params("arbitrary"),
        name="moe_ffn",
    )(block_e, n_used, slot_tok, slot_tok, slot_dst, slot_dst, f_tiles, w1, b1.reshape(b1.shape[0], n_e, 1, f2),
      w2, b2.reshape(b2.shape[0], n_e, 1, d))


def _combine_experts(y_ref, tw_ref, tile, d):
    rt = d // LANES
    tw = tw_ref[0]
    out = 0.0
    for k in range(TOP_K):
        yk = jnp.concatenate([y_ref[pl.ds(k * rt + c, tile, stride=TOP_K * rt), :] for c in range(rt)],
                             axis=-1)
        out = out + tw[:, k:k + 1] * yk
    return out


def _combine0_kernel(y_ref, tw_ref, h_ref, m0_ref, m1_ref, g_ref, hl_ref, al_ref, ac_ref, *, n_ctx_tiles):
    tile, d = h_ref.shape[1:]
    h2 = h_ref[0] + m0_ref[0][5:6] * _combine_experts(y_ref, tw_ref, tile, d)
    m1 = m1_ref[0]
    a = _rms(h2, g_ref[...]) * (1.0 + m1[1:2]) + m1[0:1]
    is_ctx = pl.program_id(1) < n_ctx_tiles

    @pl.when(is_ctx)
    def _():
        ac_ref[0] = a

    @pl.when(jnp.logical_not(is_ctx))
    def _():
        hl_ref[0] = h2
        al_ref[0] = a


def _combine0(y_asg, top_w, h1, mod0, mod1, g1_next, n_ctx_tiles):
    bsz, t, d = h1.shape
    nt = t // TOK_TILE
    n_ctx = n_ctx_tiles * TOK_TILE
    tok = pl.BlockSpec((1, TOK_TILE, d), lambda b, i: (b, i, 0))
    lat = pl.BlockSpec((1, TOK_TILE, d), lambda b, i: (b, jnp.maximum(i - n_ctx_tiles, 0), 0))
    ctx = pl.BlockSpec((1, TOK_TILE, d), lambda b, i: (b, jnp.minimum(i, n_ctx_tiles - 1), 0))
    return pl.pallas_call(
        functools.partial(_combine0_kernel, n_ctx_tiles=n_ctx_tiles),
        grid=(bsz, nt),
        in_specs=[pl.BlockSpec((TOK_TILE * TOP_K * d // LANES, LANES), lambda b, i: (b * nt + i, 0)),
                  pl.BlockSpec((1, TOK_TILE, TOP_K), lambda b, i: (b * nt + i, 0, 0)),
                  tok, _mod_spec(d, n_ctx_tiles), _mod_spec(d, n_ctx_tiles),
                  pl.BlockSpec((1, d), lambda b, i: (0, 0))],
        out_specs=[lat, lat, ctx],
        out_shape=[jax.ShapeDtypeStruct((bsz, t - n_ctx, d), F32),
                   jax.ShapeDtypeStruct((bsz, t - n_ctx, d), F32),
                   jax.ShapeDtypeStruct((bsz, n_ctx, d), F32)],
        compiler_params=_params("parallel", "parallel"),
        name="combine0",
    )(y_asg, top_w, h1, mod0, mod1, g1_next.reshape(1, d))


def _s5_matrices(a_re, a_im, log_dt, b_re, b_im, c_re, c_im):
    ng, p = a_re.shape[1:]
    gc = b_re.shape[-1]
    ll = S5_CHUNK
    gpb = LANES // gc
    nblk = ng // gpb
    lam_re = jnp.minimum(a_re, S5_EIG_MAX)
    lam_im = a_im
    dt = jnp.exp(log_dt)[..., None]
    j = jnp.arange(ll + 1, dtype=F32).reshape(-1, 1, 1, 1)
    mag = jnp.exp(j * (lam_re * dt))
    pw_re = mag * jnp.cos(j * (lam_im * dt))
    pw_im = mag * jnp.sin(j * (lam_im * dt))
    ab_re, ab_im = pw_re[1], pw_im[1]
    den = lam_re * lam_re + lam_im * lam_im
    coef_re = ((ab_re - 1.0) * lam_re + ab_im * lam_im) / den
    coef_im = (ab_im * lam_re - (ab_re - 1.0) * lam_im) / den
    bb_re = coef_re[..., None] * b_re - coef_im[..., None] * b_im
    bb_im = coef_re[..., None] * b_im + coef_im[..., None] * b_re
    drv_re = pw_re[..., None] * bb_re - pw_im[..., None] * bb_im
    drv_im = pw_re[..., None] * bb_im + pw_im[..., None] * bb_re
    rd_re = c_re * pw_re[:, :, :, None, :] - c_im * pw_im[:, :, :, None, :]
    rd_im = -(c_re * pw_im[:, :, :, None, :] + c_im * pw_re[:, :, :, None, :])
    taps = (jnp.einsum('dgop,jdgpi->jdgoi', c_re, drv_re[:ll], precision=HIGHEST)
            - jnp.einsum('dgop,jdgpi->jdgoi', c_im, drv_im[:ll], precision=HIGHEST))
    r = jnp.arange(ll)
    c_in, c_drv, c_rd = [], [], []
    for d in range(2):
        lag = (r[None, :] - r[:, None]) if d == 0 else (r[:, None] - r[None, :])
        tp = jnp.where((lag >= 0)[:, :, None, None, None],
                       taps[:, d][jnp.clip(lag, 0, ll - 1)], 0.0)
        tp = tp.reshape(ll, ll, nblk, gpb, gc, gc).transpose(2, 0, 3, 5, 1, 4)
        c_in.append(tp.reshape(nblk, ll * LANES, ll * gc))
        steps = (ll - 1 - r) if d == 0 else r
        dr = jnp.stack([drv_re[:, d][steps], drv_im[:, d][steps]], axis=1)
        dr = dr.reshape(ll, 2, nblk, gpb, p, gc).transpose(2, 0, 3, 5, 1, 4)
        c_drv.append(dr.reshape(nblk, ll * LANES, 2 * p))
        steps = (r + 1) if d == 0 else (ll - r)
        rd = jnp.stack([rd_re[:, d][steps], rd_im[:, d][steps]], axis=1)
        rd = rd.reshape(ll, 2, nblk, gpb, gc, p).transpose(2, 1, 3, 5, 0, 4)
        c_rd.append(rd.reshape(nblk, 2 * gpb * p, ll * gc))

    def expand(compact, row_unit, col_unit):
        compact = jnp.stack(compact).astype(BF16)
        n_r, n_c = compact.shape[2], compact.shape[3] * gpb
        col = jnp.arange(n_c)
        src = (col // (col_unit * gpb)) * col_unit + col % col_unit
        spread = (jnp.arange(n_c // gpb)[:, None] == src[None, :]).astype(BF16)
        full = jnp.einsum('dbrk,kc->dbrc', compact, spread, preferred_element_type=F32)
        same = ((jnp.arange(n_r) // row_unit) % gpb)[:, None] == ((col // col_unit) % gpb)[None, :]
        return jnp.where(same, full, 0.0).astype(BF16)

    a8 = jnp.stack([pw_re[ll], pw_im[ll]], axis=1).reshape(2, 2, nblk, gpb * p).transpose(0, 2, 1, 3)
    return expand(c_in, gc, gc), expand(c_drv, gc, p), expand(c_rd, p, gc), a8


def _s5_kernel(uc_ref, ul_ref, min_ref, mdrv_ref, mrd_ref, a8_ref, y_ref, v_ref, st_ref, *, n_ctx_cols):
    bsz, n_oct, ll, n_col, lanes = ul_ref.shape
    n_rows = bsz * n_oct * n_col
    half = st_ref.shape[-1] // 2
    d = pl.program_id(0)
    k = pl.program_id(2)

    @pl.when(k == 0)
    def _():
        st_ref[...] = jnp.zeros_like(st_ref)

    def chunks(u_ref):
        return jnp.concatenate([u_ref[:, :, r].reshape(n_rows, lanes) for r in range(ll)], axis=-1)

    x = jnp.where(k == 0, chunks(uc_ref), chunks(ul_ref)).astype(BF16)
    inj_all = jnp.dot(x, mdrv_ref[0, 0], preferred_element_type=F32)
    n_pl = v_ref.shape[0]
    for c in range(n_pl):
        v_ref[c] = inj_all[:, c * lanes:(c + 1) * lanes]

    a_re = jnp.broadcast_to(a8_ref[0, 0, 0:1], (bsz, half))
    a_im = jnp.broadcast_to(a8_ref[0, 0, 1:2], (bsz, half))
    def scan(n_steps):
        def step(i, carry):
            s_re, s_im = carry
            i = jnp.where(d == 0, i, n_steps - 1 - i)
            row = (i % n_oct) * n_col + i // n_oct
            rows = pl.ds(row, bsz, stride=n_oct * n_col)
            inj = jnp.concatenate([v_ref[c, rows, :] for c in range(n_pl)], axis=-1)
            for c in range(n_pl // 2):
                v_ref[c, rows, :] = s_re[:, c * lanes:(c + 1) * lanes]
                v_ref[n_pl // 2 + c, rows, :] = s_im[:, c * lanes:(c + 1) * lanes]
            return (a_re * s_re - a_im * s_im + inj[:, :half],
                    a_re * s_im + a_im * s_re + inj[:, half:])

        s_re, s_im = lax.fori_loop(0, n_steps, step, (st_ref[:, :half], st_ref[:, half:]), unroll=2)
        st_ref[:, :half] = s_re
        st_ref[:, half:] = s_im

    @pl.when(k == 0)
    def _():
        scan(n_ctx_cols * n_oct)

    @pl.when(k > 0)
    def _():
        scan(n_col * n_oct)

    s_start = jnp.concatenate([v_ref[c] for c in range(n_pl)], axis=-1).astype(BF16)
    y = (jnp.dot(x, min_ref[0, 0], preferred_element_type=F32)
         + jnp.dot(s_start, mrd_ref[0, 0], preferred_element_type=F32))
    for r in range(ll):
        y_ref[0, :, :, r] = y[:, r * lanes:(r + 1) * lanes].reshape(bsz, n_oct, n_col, lanes)


def _s5_scan(a_lat, a_ctx, mats):
    bsz, t, d = a_lat.shape
    n_ctx = a_ctx.shape[1]
    m_in, m_drv, m_rd, a8 = mats
    n_rows = t // GRID_W
    n_oct = n_rows // S5_CHUNK
    n_ctx_cols = n_ctx // n_rows
    u_lat = a_lat.reshape(bsz, n_oct, S5_CHUNK, GRID_W, d)
    u_ctx = a_ctx.reshape(bsz, n_ctx_cols, n_rows, d).transpose(0, 2, 1, 3)
    u_ctx = jnp.pad(u_ctx, ((0, 0), (0, 0), (0, S5_COLS - n_ctx_cols), (0, 0)))
    u_ctx = u_ctx.reshape(bsz, n_oct, S5_CHUNK, S5_COLS, d)
    n_lat = GRID_W // S5_COLS
    nblk = d // LANES

    def lat_tile(dd, k):
        kk = jnp.maximum(k, 1) - 1
        return jnp.where(dd == 0, kk, n_lat - 1 - kk)

    blk = (bsz, n_oct, S5_CHUNK, S5_COLS, LANES)
    wspec = lambda shape: pl.BlockSpec((1, 1) + shape, lambda dd, j, k: (dd, j, 0, 0))
    kl = S5_CHUNK * LANES
    ns = m_drv.shape[-1]
    y = pl.pallas_call(
        functools.partial(_s5_kernel, n_ctx_cols=n_ctx_cols),
        grid=(2, nblk, n_lat + 1),
        in_specs=[pl.BlockSpec(blk, lambda dd, j, k: (0, 0, 0, 0, j)),
                  pl.BlockSpec(blk, lambda dd, j, k: (0, 0, 0, lat_tile(dd, k), j)),
                  wspec((kl, kl)), wspec((kl, ns)), wspec((ns, kl)),
                  pl.BlockSpec((1, 1, 2, ns // 2), lambda dd, j, k: (dd, j, 0, 0))],
        out_specs=pl.BlockSpec((1,) + blk, lambda dd, j, k: (dd, 0, 0, 0, lat_tile(dd, k), j)),
        out_shape=jax.ShapeDtypeStruct((2,) + u_lat.shape, F32),
        scratch_shapes=[pltpu.VMEM((ns // LANES, bsz * n_oct * S5_COLS, LANES), F32),
                        pltpu.VMEM((bsz, ns), F32)],
        compiler_params=_params("parallel", "parallel", "arbitrary"),
        name="s5_scan",
    )(u_ctx, u_lat, m_in, m_drv, m_rd, a8)
    return y.reshape(2, bsz, n_rows, GRID_W, d)


def _s5_glu_kernel(y_ref, u_ref, h_ref, m_ref, dsk_ref, w_ref, bg_ref, g2_ref, rw_ref, rb_ref,
                   h1_ref, f_ref, tw_ref, meta_ref, cnt_ref, run_ref):
    d = h_ref.shape[-1]
    n = h_ref.shape[1] * h_ref.shape[2]
    y = (y_ref[0, 0] + y_ref[1, 0] + dsk_ref[...] * u_ref[0]).reshape(n, d)
    z = jnp.dot(jax.nn.gelu(y).astype(BF16), w_ref[...], preferred_element_type=F32) + bg_ref[...]
    y = z[:, :d] * _sigmoid(z[:, d:])
    first = (pl.program_id(0) == 0) & (pl.program_id(1) == 0)
    _mixer_tail(first, h_ref[0].reshape(n, d), y, m_ref[0], g2_ref[...], rw_ref[...], rb_ref[...],
                run_ref, h1_ref, f_ref, tw_ref.at[0], meta_ref, cnt_ref)


def _s5_glu(y, a_lat, h_lat, mod, d_skip, w_glu, b_glu, g2, rw, rb):
    bsz, t, d = h_lat.shape
    n_rows = t // GRID_W
    n_e = rw.shape[-1]
    n_tok = n_rows * S5_COLS
    nt = GRID_W // S5_COLS
    lat = pl.BlockSpec((1, n_rows, S5_COLS, d), lambda b, i: (b, 0, i, 0))
    tail_specs, tail_shapes = _tail_outputs(bsz * nt, n_tok, d, n_e, lambda b, i: b * nt + i)
    const = lambda shape: pl.BlockSpec(shape, lambda b, i: (0,) * len(shape))
    return pl.pallas_call(
        _s5_glu_kernel,
        grid=(bsz, nt),
        in_specs=[pl.BlockSpec((2, 1, n_rows, S5_COLS, d), lambda b, i: (0, b, 0, i, 0)),
                  lat, lat, pl.BlockSpec((1, 6, d), lambda b, i: (2 * b + 1, 0, 0)),
                  const((1, d)), const((d, 2 * d)), const((1, 2 * d)), const((1, d)),
                  const((d, n_e)), const((1, n_e))],
        out_specs=[lat] + tail_specs,
        out_shape=[jax.ShapeDtypeStruct((bsz, n_rows, GRID_W, d), F32)] + tail_shapes,
        scratch_shapes=[pltpu.VMEM((1, n_e), F32)],
        compiler_params=_params("arbitrary", "arbitrary"),
        name="s5_glu",
    )(y, a_lat.reshape(bsz, n_rows, GRID_W, d), h_lat.reshape(bsz, n_rows, GRID_W, d), mod,
      d_skip.reshape(1, d), w_glu.astype(BF16), b_glu.reshape(1, 2 * d),
      g2.reshape(1, d), rw, rb.reshape(1, n_e))


def _combine1_kernel(y_ref, tw_ref, h_ref, m_ref, g_ref, o_ref):
    d = h_ref.shape[-1]
    n = h_ref.shape[1] * h_ref.shape[2]
    h2 = h_ref[0].reshape(n, d) + m_ref[0][5:6] * _combine_experts(y_ref, tw_ref, n, d)
    o_ref[0] = _rms(h2, g_ref[...]).reshape(o_ref.shape[1:])


def _combine1(y_asg, top_w, h1, mod, final_g):
    bsz, n_rows, _, d = h1.shape
    nt = GRID_W // S5_COLS
    n_tok = n_rows * S5_COLS
    lat = pl.BlockSpec((1, n_rows, S5_COLS, d), lambda b, i: (b, 0, i, 0))
    return pl.pallas_call(
        _combine1_kernel,
        grid=(bsz, nt),
        in_specs=[pl.BlockSpec((n_tok * TOP_K * d // LANES, LANES), lambda b, i: (b * nt + i, 0)),
                  pl.BlockSpec((1, n_tok, TOP_K), lambda b, i: (b * nt + i, 0, 0)),
                  lat, pl.BlockSpec((1, 6, d), lambda b, i: (2 * b + 1, 0, 0)),
                  pl.BlockSpec((1, d), lambda b, i: (0, 0))],
        out_specs=lat,
        out_shape=jax.ShapeDtypeStruct(h1.shape, F32),
        compiler_params=_params("parallel", "parallel"),
        name="combine1",
    )(y_asg, top_w, h1, mod, final_g.reshape(1, d))


def kernel(x, c, ctx, c_ctx, mod_w, mod_b, norm1_g, norm2_g, hg_w_in, hg_lb_logits, hg_norm_g,
           hg_w_out, s5_a_re, s5_a_im, s5_log_dt, s5_b_re, s5_b_im, s5_c_re, s5_c_im, s5_d, s5_w_glu,
           s5_b_glu, router_w, router_b, moe_w1, moe_b1, moe_w2, moe_b2, final_g):
    bsz, seq, d = x.shape
    n_ctx = ctx.shape[1]
    assert mod_w.shape[0] == 2 and n_ctx % TOK_TILE == 0 and seq % TOK_TILE == 0
    n_ctx_tiles = n_ctx // TOK_TILE
    lb_all = jnp.cumsum(jax.nn.softmax(hg_lb_logits.astype(F32), axis=0), axis=0)
    mod = _modulation(c, c_ctx, mod_w, mod_b)
    h = jnp.concatenate([ctx, x], axis=1)

    qs, v, sg, kf, kb, lff, lfb = _hg_proj(h, mod[0], norm1_g[0], hg_w_in[0], lb_all[0], n_ctx_tiles)
    o_f, o_b = _gla(qs, v, kf, kb, lff, lfb, n_ctx // HG_CHUNK)
    h1, f, top_w, meta, counts = _hg_readout(o_f, o_b, sg, h, mod[0], hg_norm_g[0], hg_w_out[0],
                                             norm2_g[0], router_w[0], router_b[0], n_ctx_tiles)
    y_asg = _moe(f, meta, counts, 0, moe_w1, moe_b1, moe_w2, moe_b2)
    h_lat, a_lat, a_ctx = _combine0(y_asg, top_w, h1, mod[0], mod[1], norm1_g[1], n_ctx_tiles)

    mats = _s5_matrices(s5_a_re[0], s5_a_im[0], s5_log_dt[0], s5_b_re[0], s5_b_im[0],
                        s5_c_re[0], s5_c_im[0])
    y_s5 = _s5_scan(a_lat, a_ctx, mats)
    h1, f, top_w, meta, counts = _s5_glu(y_s5, a_lat, h_lat, mod[1], s5_d[0], s5_w_glu[0], s5_b_glu[0],
                                         norm2_g[1], router_w[1], router_b[1])
    y_asg = _moe(f, meta, counts, 1, moe_w1, moe_b1, moe_w2, moe_b2)
    out = _combine1(y_asg, top_w, h1, mod[1], final_g)
    return out.reshape(bsz, seq, d)
```

```python
import functools

import jax
import jax.numpy as jnp
from jax import lax
from jax.experimental import pallas as pl
from jax.experimental.pallas import tpu as pltpu

F32 = jnp.float32
BF16 = jnp.bfloat16
HIGHEST = lax.Precision.HIGHEST

RMS_EPS = 1e-6
GRID_W = 64
HEAD_DIM = 128
HG_CHUNK = 64
HG_BATCH = 2
S5_GROUP = 16
S5_STATE = 64
S5_EIG_MAX = -1e-4
S5_CHUNK = 8
S5_COLS = 8
LANES = 128
N_EXPERTS = 32
TOP_K = 4
SWIGLU_ALPHA = 1.702
SWIGLU_LIMIT = 7.0
TOK_TILE = 256
MOE_ROWS = 256
VMEM_LIMIT = 56 * 1024 * 1024


def _params(*sem):
    return pltpu.CompilerParams(dimension_semantics=sem, vmem_limit_bytes=VMEM_LIMIT)


def _rms(x, g):
    return x * lax.rsqrt(jnp.mean(x * x, axis=-1, keepdims=True) + RMS_EPS) * g


def _sigmoid(x):
    return 1.0 / (1.0 + jnp.exp(-x))


def _silu(x):
    return x * _sigmoid(x)


def _to_token_tiles(ref, x):
    n, d = x.shape
    for c in range(d // LANES):
        ref[pl.ds(c, n, stride=d // LANES), :] = x[:, c * LANES:(c + 1) * LANES]


def _from_token_tiles(ref, n, d):
    return jnp.concatenate([ref[pl.ds(c, n, stride=d // LANES), :] for c in range(d // LANES)], axis=-1)


def _mod_kernel(c_ref, w_ref, b_ref, o_ref):
    s = _silu(c_ref[...])
    o_ref[0] = jnp.dot(s, w_ref[0], precision=HIGHEST, preferred_element_type=F32) + b_ref[0]


def _modulation(c, c_ctx, mod_w, mod_b):
    bsz, d = c.shape
    depth = mod_w.shape[0]
    rows = jnp.concatenate([c, c_ctx[None], jnp.zeros((16 - bsz - 1, d), F32)], axis=0)
    bn = 6 * d // 4
    out = pl.pallas_call(
        _mod_kernel,
        grid=(depth, 4),
        in_specs=[pl.BlockSpec((16, d), lambda l, j: (0, 0)),
                  pl.BlockSpec((1, d, bn), lambda l, j: (l, 0, j)),
                  pl.BlockSpec((1, 1, bn), lambda l, j: (l, 0, j))],
        out_specs=pl.BlockSpec((1, 16, bn), lambda l, j: (l, 0, j)),
        out_shape=jax.ShapeDtypeStruct((depth, 16, 6 * d), F32),
        compiler_params=_params("parallel", "parallel"),
        name="modulation",
    )(rows, mod_w, mod_b.reshape(depth, 1, 6 * d))
    m_lat = out[:, :bsz].reshape(depth, bsz, 1, 6, d)
    m_ctx = jnp.broadcast_to(out[:, bsz].reshape(depth, 1, 1, 6, d), m_lat.shape)
    return jnp.concatenate([m_ctx, m_lat], axis=2).reshape(depth, 2 * bsz, 6, d)


def _residual_tile(hc_ref, hx_ref, n_ctx_tiles):
    return jnp.where(pl.program_id(1) < n_ctx_tiles, hc_ref[0], hx_ref[0])


def _residual_specs(d, n_ctx_tiles):
    return [pl.BlockSpec((1, TOK_TILE, d), lambda b, i: (b, jnp.minimum(i, n_ctx_tiles - 1), 0)),
            pl.BlockSpec((1, TOK_TILE, d), lambda b, i: (b, jnp.maximum(i - n_ctx_tiles, 0), 0))]


def _hg_proj_kernel(hc_ref, hx_ref, m_ref, g_ref, w_ref, lb_ref,
                    qs_ref, v_ref, sg_ref, kf_ref, kb_ref, lff_ref, lfb_ref, *, n_ctx_tiles):
    d = hx_ref.shape[-1]
    m = m_ref[0]
    a = _rms(_residual_tile(hc_ref, hx_ref, n_ctx_tiles), g_ref[...]) * (1.0 + m[1:2]) + m[0:1]
    p = jnp.dot(a.astype(BF16), w_ref[...], preferred_element_type=F32)
    qs_ref[0] = _silu(p[:, 0:d]).astype(BF16)
    v_ref[0] = p[:, d:2 * d].astype(BF16)
    sg_ref[0] = _silu(p[:, 4 * d:5 * d]).astype(BF16)
    for di, (k_ref, lf_ref) in enumerate(((kf_ref, lff_ref), (kb_ref, lfb_ref))):
        lb = lb_ref[di:di + 1]
        f = lb + (1.0 - lb) * _sigmoid(p[:, (2 + di) * d:(3 + di) * d])
        k_ref[0] = (1.0 - f).astype(BF16)
        lf_ref[0] = jnp.log(f)


def _mod_spec(d, n_ctx_tiles):
    return pl.BlockSpec((1, 6, d), lambda b, i: (2 * b + (i >= n_ctx_tiles).astype(jnp.int32), 0, 0))


def _hg_proj(ctx, x, mod, g1, w_in, lb, n_ctx_tiles):
    bsz, t, d = x.shape
    t += ctx.shape[1]
    tok = pl.BlockSpec((1, TOK_TILE, d), lambda b, i: (b, i, 0))
    bf = jax.ShapeDtypeStruct((bsz, t, d), BF16)
    ff = jax.ShapeDtypeStruct((bsz, t, d), F32)
    return pl.pallas_call(
        functools.partial(_hg_proj_kernel, n_ctx_tiles=n_ctx_tiles),
        grid=(bsz, t // TOK_TILE),
        in_specs=_residual_specs(d, n_ctx_tiles) + [_mod_spec(d, n_ctx_tiles),
                  pl.BlockSpec((1, d), lambda b, i: (0, 0)),
                  pl.BlockSpec((d, 5 * d), lambda b, i: (0, 0)),
                  pl.BlockSpec((2, d), lambda b, i: (0, 0))],
        out_specs=[tok] * 7,
        out_shape=[bf, bf, bf, bf, bf, ff, ff],
        compiler_params=_params("parallel", "parallel"),
        name="hg_proj",
    )(ctx, x, mod, g1.reshape(1, d), w_in.astype(BF16), lb)


def _split3(x):
    hi = x.astype(BF16)
    r = x - hi.astype(F32)
    mid = r.astype(BF16)
    lo = (r - mid.astype(F32)).astype(BF16)
    return hi, mid, lo


def _gla_kernel(qf_ref, vf_ref, kf_ref, lf_ref, qb_ref, vb_ref, kb_ref, lb_ref,
                of_ref, ob_ref, sf_ref, sb_ref):
    c = HG_CHUNK
    n_b = qf_ref.shape[0]
    n_heads = sf_ref.shape[0] // n_b

    @pl.when(pl.program_id(1) == 0)
    def _():
        sf_ref[...] = jnp.zeros_like(sf_ref)
        sb_ref[...] = jnp.zeros_like(sb_ref)

    row = lax.broadcasted_iota(jnp.int32, (c, c), 0)
    col = lax.broadcasted_iota(jnp.int32, (c, c), 1)
    dirs = ((qf_ref, vf_ref, kf_ref, lf_ref, of_ref, sf_ref, col <= row, c // 2 - 1, c - 1),
            (qb_ref, vb_ref, kb_ref, lb_ref, ob_ref, sb_ref, col >= row, c // 2, 0))
    for bb, (q_ref, v_ref, k_ref, l_ref, o_ref, s_ref, keep, r_ref, r_last) in (
            (bb, dr) for bb in range(n_b) for dr in dirs):
        tri = keep.astype(BF16)
        b = sum(jnp.dot(tri, part, preferred_element_type=F32) for part in _split3(l_ref[bb]))
        b_ref = b[r_ref:r_ref + 1]
        b_last = b[r_last:r_last + 1]
        a_in = q_ref[bb].astype(F32) * jnp.exp(b - b_ref)
        k_in = k_ref[bb].astype(F32) * jnp.exp(b_ref - b)
        q_st = (a_in * jnp.exp(b_ref)).astype(BF16)
        k_st = (k_in * jnp.exp(b_last - b_ref)).astype(BF16)
        a_in = a_in.astype(BF16)
        k_in = k_in.astype(BF16)
        decay = jnp.exp(b_last)
        v = v_ref[bb]
        for h in range(n_heads):
            sl = slice(h * HEAD_DIM, (h + 1) * HEAD_DIM)
            sc = lax.dot_general(a_in[:, sl], k_in[:, sl], (((1,), (1,)), ((), ())),
                                 preferred_element_type=F32)
            sc = jnp.where(keep, sc, 0.0).astype(BF16)
            o = jnp.dot(sc, v[:, sl], preferred_element_type=F32)
            st = s_ref[bb * n_heads + h]
            o = o + lax.dot_general(q_st[:, sl], st.astype(BF16), (((1,), (1,)), ((), ())),
                                    preferred_element_type=F32)
            s_ref[bb * n_heads + h] = st * decay[:, sl] + lax.dot_general(
                v[:, sl], k_st[:, sl], (((0,), (0,)), ((), ())), preferred_element_type=F32)
            o_ref[bb, :, sl] = o


def _gla(qs, v, kf, kb, lff, lfb, n_ctx_chunks):
    bsz, t, d = qs.shape
    n = t // HG_CHUNK
    n_heads = d // HEAD_DIM

    def rev(j):
        return jnp.where(j < n_ctx_chunks, n_ctx_chunks - 1 - j, n + n_ctx_chunks - 1 - j)

    fwd = pl.BlockSpec((HG_BATCH, HG_CHUNK, d), lambda b, j: (b, j, 0))
    bwd = pl.BlockSpec((HG_BATCH, HG_CHUNK, d), lambda b, j: (b, rev(j), 0))
    out = jax.ShapeDtypeStruct((bsz, t, d), F32)
    state = pltpu.VMEM((HG_BATCH * n_heads, HEAD_DIM, HEAD_DIM), F32)
    return pl.pallas_call(
        _gla_kernel,
        grid=(bsz // HG_BATCH, n),
        in_specs=[fwd, fwd, fwd, fwd, bwd, bwd, bwd, bwd],
        out_specs=[fwd, bwd],
        out_shape=[out, out],
        scratch_shapes=[state, state],
        compiler_params=_params("parallel", "arbitrary"),
        name="gla",
    )(qs, v, kf, lff, qs, v, kb, lfb)


def _mixer_tail(first, h, y, m, g2, rw, rb, run_ref, h1_ref, f_ref, tw_ref, meta_ref, cnt_ref):
    h1 = h + m[2:3] * y
    f = _rms(h1, g2) * (1.0 + m[4:5]) + m[3:4]
    h1_ref[...] = h1.reshape(h1_ref.shape)
    _to_token_tiles(f_ref, f)
    f_hi, f_lo, _ = _split3(f)
    w_hi, w_lo, _ = _split3(rw)
    logits = (jnp.dot(f_hi, w_hi, preferred_element_type=F32) + jnp.dot(f_hi, w_lo, preferred_element_type=F32)
              + jnp.dot(f_lo, w_hi, preferred_element_type=F32)) + rb
    n, n_e = logits.shape
    lane = lax.broadcasted_iota(jnp.int32, logits.shape, 1).astype(F32)
    vals, idxs, hots = [], [], []
    for _ in range(TOP_K):
        mx = jnp.max(logits, axis=-1, keepdims=True)
        ix = jnp.min(jnp.where(logits == mx, lane, float(n_e)), axis=-1, keepdims=True)
        hot = lane == ix
        vals.append(mx)
        idxs.append(ix)
        hots.append(hot)
        logits = jnp.where(hot, -jnp.inf, logits)
    es = [jnp.exp(x - vals[0]) for x in vals]
    tot = sum(es)
    for k in range(TOP_K):
        tw_ref[:, k:k + 1] = es[k] / tot

    @pl.when(first)
    def _():
        run_ref[...] = jnp.zeros_like(run_ref)

    picked = sum(hot.astype(F32) for hot in hots)
    r_i = lax.broadcasted_iota(jnp.int32, (n, n), 0)
    c_i = lax.broadcasted_iota(jnp.int32, (n, n), 1)
    earlier = jnp.dot((c_i < r_i).astype(BF16), picked.astype(BF16), preferred_element_type=F32)
    rank = earlier + run_ref[...]
    total = run_ref[...] + jnp.sum(picked, axis=0, keepdims=True)
    run_ref[...] = total
    cnt_ref[...] = total
    col = lax.broadcasted_iota(jnp.int32, (n, LANES), 1)
    z = jnp.zeros((n, LANES), F32)
    for k in range(TOP_K):
        pos = jnp.sum(jnp.where(hots[k], rank, 0.0), axis=-1, keepdims=True)
        z = jnp.where(col == k, idxs[k], z)
        z = jnp.where(col == TOP_K + k, pos, z)
    meta_ref[0] = z.T[0:2 * TOP_K].astype(jnp.int32)


def _tail_outputs(n_tiles, tile, d, n_e, index):
    rt = d // LANES
    specs = [pl.BlockSpec((tile * rt, LANES), lambda b, i: (index(b, i), 0)),
             pl.BlockSpec((1, tile, TOP_K), lambda b, i: (index(b, i), 0, 0)),
             pl.BlockSpec((1, 2 * TOP_K, tile), lambda b, i: (index(b, i), 0, 0)),
             pl.BlockSpec((1, n_e), lambda b, i: (0, 0))]
    shapes = [jax.ShapeDtypeStruct((n_tiles * tile * rt, LANES), F32),
              jax.ShapeDtypeStruct((n_tiles, tile, TOP_K), F32),
              jax.ShapeDtypeStruct((n_tiles, 2 * TOP_K, tile), jnp.int32),
              jax.ShapeDtypeStruct((1, n_e), F32)]
    return specs, shapes


def _hg_readout_kernel(of_ref, ob_ref, sg_ref, hc_ref, hx_ref, m_ref, ng_ref, w_ref, g2_ref, rw_ref, rb_ref,
                       h1_ref, f_ref, tw_ref, meta_ref, cnt_ref, run_ref, *, n_ctx_tiles):
    d = hx_ref.shape[-1]
    o = of_ref[0] + ob_ref[0]
    parts = []
    for h in range(d // HEAD_DIM):
        oh = o[:, h * HEAD_DIM:(h + 1) * HEAD_DIM]
        parts.append(oh * lax.rsqrt(jnp.mean(oh * oh, axis=-1, keepdims=True) + RMS_EPS))
    o = jnp.concatenate(parts, axis=-1) * ng_ref[...]
    y = jnp.dot((o * sg_ref[0].astype(F32)).astype(BF16), w_ref[...], preferred_element_type=F32)
    first = (pl.program_id(0) == 0) & (pl.program_id(1) == 0)
    _mixer_tail(first, _residual_tile(hc_ref, hx_ref, n_ctx_tiles), y, m_ref[0], g2_ref[...], rw_ref[...],
                rb_ref[...], run_ref, h1_ref, f_ref, tw_ref.at[0], meta_ref, cnt_ref)


def _hg_readout(o_f, o_b, sg, ctx, x, mod, norm_g, w_out, g2, rw, rb, n_ctx_tiles):
    bsz, t, d = o_f.shape
    n_e = rw.shape[-1]
    nt = t // TOK_TILE
    tok = pl.BlockSpec((1, TOK_TILE, d), lambda b, i: (b, i, 0))
    const = lambda shape: pl.BlockSpec(shape, lambda b, i: (0,) * len(shape))
    tail_specs, tail_shapes = _tail_outputs(bsz * nt, TOK_TILE, d, n_e, lambda b, i: b * nt + i)
    return pl.pallas_call(
        functools.partial(_hg_readout_kernel, n_ctx_tiles=n_ctx_tiles),
        grid=(bsz, nt),
        in_specs=[tok, tok, tok] + _residual_specs(d, n_ctx_tiles) + [_mod_spec(d, n_ctx_tiles),
                  const((1, d)), const((d, d)), const((1, d)), const((d, n_e)), const((1, n_e))],
        out_specs=[tok] + tail_specs,
        out_shape=[jax.ShapeDtypeStruct((bsz, t, d), F32)] + tail_shapes,
        scratch_shapes=[pltpu.VMEM((1, n_e), F32)],
        compiler_params=_params("arbitrary", "arbitrary"),
        name="hg_readout",
    )(o_f, o_b, sg, ctx, x, mod, norm_g.reshape(1, d), w_out.astype(BF16), g2.reshape(1, d),
      rw, rb.reshape(1, n_e))


def _moe_plan(counts, n_asg, rows):
    counts = counts.reshape(-1).astype(jnp.int32)
    padded = (counts + rows - 1) // rows * rows
    pad_end = jnp.cumsum(padded)
    n_blocks = -(-n_asg // rows) + N_EXPERTS
    block_pos = jnp.arange(n_blocks, dtype=jnp.int32) * rows
    block_e = jnp.minimum(jnp.sum((block_pos[:, None] >= pad_end[None, :]).astype(jnp.int32), axis=1),
                          N_EXPERTS - 1)
    n_used = (pad_end[-1:] // rows).astype(jnp.int32)
    return pad_end - padded, block_e, n_used, n_blocks


def _invert_kernel(dest_ref, init_hbm, inv_ref, sem):
    i = pl.program_id(0)
    tile = dest_ref.shape[2]

    @pl.when(i == 0)
    def _():
        cp = pltpu.make_async_copy(init_hbm, inv_ref, sem)
        cp.start()
        cp.wait()

    base = i * (tile * TOP_K)
    for r in range(tile):
        for k in range(TOP_K):
            inv_ref[dest_ref[0, k, r]] = base + (r * TOP_K + k)


def _invert(meta, slot_start, n_slots):
    n_tiles, _, tile = meta.shape
    hot = meta[:, :TOP_K, :, None] == jnp.arange(N_EXPERTS, dtype=jnp.int32)
    dest = meta[:, TOP_K:] + jnp.sum(jnp.where(hot, slot_start, 0), axis=-1)
    return pl.pallas_call(
        _invert_kernel,
        grid=(n_tiles,),
        in_specs=[pl.BlockSpec((1, TOP_K, tile), lambda i: (i, 0, 0), memory_space=pltpu.SMEM),
                  pl.BlockSpec(memory_space=pl.ANY)],
        out_specs=pl.BlockSpec(memory_space=pltpu.SMEM),
        out_shape=jax.ShapeDtypeStruct((n_slots,), jnp.int32),
        scratch_shapes=[pltpu.SemaphoreType.DMA(())],
        compiler_params=_params("arbitrary"),
        name="moe_invert",
    )(dest.astype(jnp.int32), jnp.full((n_slots,), -1, jnp.int32))


def _ffn_kernel(be_ref, nu_ref, tok_ref, tok_next_ref, dst_prev_ref, dst_ref, x_hbm,
                w1_ref, b1_ref, w2_ref, b2_ref, y_hbm, xbuf, ybuf, zbuf, w1c, w2c, gsem, ssem, zsem):
    b = pl.program_id(0)
    nu = nu_ref[0]
    d, f2 = w1c.shape
    rt = d // LANES
    rows = xbuf.shape[1] // rt

    def tile_of(ref, idx):
        return ref.at[pl.ds(pl.multiple_of(idx * rt, rt), rt)]

    def gather(idx_ref, s):
        for r in range(rows):
            pltpu.make_async_copy(tile_of(x_hbm, idx_ref[0, 0, r]), xbuf.at[s, pl.ds(r * rt, rt)],
                                  gsem.at[s]).start(priority=r % 2)

    def scatter(idx_ref, s):
        for r in range(rows):
            pltpu.make_async_copy(ybuf.at[s, pl.ds(r * rt, rt)], tile_of(y_hbm, idx_ref[0, 0, r]),
                                  ssem.at[s]).start(priority=r % 2)

    def wait_gather(s):
        pltpu.make_async_copy(x_hbm.at[pl.ds(0, rows * rt)], xbuf.at[s], gsem.at[s]).wait()

    def wait_scatter(s):
        pltpu.make_async_copy(ybuf.at[s], y_hbm.at[pl.ds(0, rows * rt)], ssem.at[s]).wait()

    @pl.when(b == 0)
    def _():
        zbuf[...] = jnp.zeros_like(zbuf)
        ybuf[1] = jnp.zeros(ybuf.shape[1:], F32)
        gather(tok_ref, 0)

    @pl.when(b >= nu)
    def _():
        dst = y_hbm.at[pl.ds(pl.multiple_of(b * (rows * rt), rows * rt), rows * rt)]
        cp = pltpu.make_async_copy(zbuf, dst, zsem)
        cp.start()
        cp.wait()

    def used_block(s):
        @pl.when((b == 0) | (be_ref[b] != be_ref[jnp.maximum(b - 1, 0)]))
        def _():
            w1c[...] = w1_ref[0, 0].astype(BF16)
            w2c[...] = w2_ref[0, 0].astype(BF16)

        wait_gather(s)

        @pl.when(b >= 1)
        def _():
            wait_scatter(s)

        gather(tok_next_ref, 1 - s)
        scatter(dst_prev_ref, 1 - s)
        x = _from_token_tiles(xbuf.at[s], rows, d).astype(BF16)
        z = jnp.dot(x, w1c[...], preferred_element_type=F32) + b1_ref[0, 0]
        z_glu = jnp.minimum(z[:, :f2 // 2], SWIGLU_LIMIT)
        z_lin = jnp.clip(z[:, f2 // 2:], -SWIGLU_LIMIT, SWIGLU_LIMIT)
        act = z_glu * _sigmoid(SWIGLU_ALPHA * z_glu) * (z_lin + 1.0)
        y = jnp.dot(act.astype(BF16), w2c[...], preferred_element_type=F32) + b2_ref[0, 0]
        _to_token_tiles(ybuf.at[s], y)

        @pl.when(b == nu - 1)
        def _():
            scatter(dst_ref, s)
            wait_gather(1 - s)
            wait_scatter(1 - s)
            wait_scatter(s)

    for s in range(2):
        pl.when((b < nu) & (b % 2 == s))(functools.partial(used_block, s))


def _moe(f_tiles, meta, counts, layer, w1, b1, w2, b2):
    n_tiles, _, tile = meta.shape
    n_asg = n_tiles * tile * TOP_K
    _, n_e, d, f2 = w1.shape
    rt = d // LANES
    slot_start, block_e, n_used, n_blocks = _moe_plan(counts, n_asg, MOE_ROWS)
    n_slots = n_blocks * MOE_ROWS
    inv = _invert(meta, slot_start, n_slots)
    is_pad = inv < 0
    slot_tok = jnp.where(is_pad, 0, inv // TOP_K).reshape(n_blocks, 1, MOE_ROWS)
    spill = n_asg + jnp.cumsum(is_pad.astype(jnp.int32)) - 1
    slot_dst = jnp.where(is_pad, spill, inv)
    slot_dst = jnp.concatenate([slot_dst, jnp.arange(n_slots - MOE_ROWS, n_slots, dtype=jnp.int32)])
    slot_dst = slot_dst.reshape(n_blocks + 1, 1, MOE_ROWS)

    live = lambda b, nu: jnp.minimum(b, nu[0] - 1)
    smem = lambda imap: pl.BlockSpec((1, 1, MOE_ROWS), imap, memory_space=pltpu.SMEM)
    per_e = lambda shape: pl.BlockSpec((1, 1) + shape, lambda b, be, nu: (layer, be[live(b, nu)], 0, 0))
    buf = pltpu.VMEM((2, MOE_ROWS * rt, LANES), F32)
    grid_spec = pltpu.PrefetchScalarGridSpec(
        num_scalar_prefetch=2,
        grid=(n_blocks,),
        in_specs=[smem(lambda b, be, nu: (live(b, nu), 0, 0)),
                  smem(lambda b, be, nu: (live(b + 1, nu), 0, 0)),
                  smem(lambda b, be, nu: (jnp.where(b == 0, n_blocks, live(b - 1, nu)), 0, 0)),
                  smem(lambda b, be, nu: (live(b, nu), 0, 0)),
                  pl.BlockSpec(memory_space=pl.ANY),
                  per_e((d, f2)), per_e((1, f2)), per_e((f2 // 2, d)), per_e((1, d))],
        out_specs=pl.BlockSpec(memory_space=pl.ANY),
        scratch_shapes=[buf, buf, pltpu.VMEM((MOE_ROWS * rt, LANES), F32),
                        pltpu.VMEM((d, f2), BF16), pltpu.VMEM((f2 // 2, d), BF16),
                        pltpu.SemaphoreType.DMA((2,)), pltpu.SemaphoreType.DMA((2,)),
                        pltpu.SemaphoreType.DMA(())])
    return pl.pallas_call(
        _ffn_kernel,
        grid_spec=grid_spec,
        out_shape=jax.ShapeDtypeStruct((n_slots * rt, LANES), F32),
        compiler_params=_params("arbitrary"),
        name="moe_ffn",
    )(block_e, n_used, slot_tok, slot_tok, slot_dst, slot_dst, f_tiles, w1, b1.reshape(b1.shape[0], n_e, 1, f2),
      w2, b2.reshape(b2.shape[0], n_e, 1, d))


def _combine_experts(y_ref, tw_ref, tile, d):
    rt = d // LANES
    tw = tw_ref[0]
    out = 0.0
    for k in range(TOP_K):
        yk = jnp.concatenate([y_ref[pl.ds(k * rt + c, tile, stride=TOP_K * rt), :] for c in range(rt)],
                             axis=-1)
        out = out + tw[:, k:k + 1] * yk
    return out


def _combine0_kernel(y_ref, tw_ref, h_ref, m0_ref, m1_ref, g_ref, hl_ref, al_ref, ac_ref, *, n_ctx_tiles):
    tile, d = h_ref.shape[1:]
    h2 = h_ref[0] + m0_ref[0][5:6] * _combine_experts(y_ref, tw_ref, tile, d)
    m1 = m1_ref[0]
    a = _rms(h2, g_ref[...]) * (1.0 + m1[1:2]) + m1[0:1]
    is_ctx = pl.program_id(1) < n_ctx_tiles

    @pl.when(is_ctx)
    def _():
        ac_ref[0] = a

    @pl.when(jnp.logical_not(is_ctx))
    def _():
        hl_ref[0] = h2
        al_ref[0] = a


def _combine0(y_asg, top_w, h1, mod0, mod1, g1_next, n_ctx_tiles):
    bsz, t, d = h1.shape
    nt = t // TOK_TILE
    n_ctx = n_ctx_tiles * TOK_TILE
    tok = pl.BlockSpec((1, TOK_TILE, d), lambda b, i: (b, i, 0))
    lat = pl.BlockSpec((1, TOK_TILE, d), lambda b, i: (b, jnp.maximum(i - n_ctx_tiles, 0), 0))
    ctx = pl.BlockSpec((1, TOK_TILE, d), lambda b, i: (b, jnp.minimum(i, n_ctx_tiles - 1), 0))
    return pl.pallas_call(
        functools.partial(_combine0_kernel, n_ctx_tiles=n_ctx_tiles),
        grid=(bsz, nt),
        in_specs=[pl.BlockSpec((TOK_TILE * TOP_K * d // LANES, LANES), lambda b, i: (b * nt + i, 0)),
                  pl.BlockSpec((1, TOK_TILE, TOP_K), lambda b, i: (b * nt + i, 0, 0)),
                  tok, _mod_spec(d, n_ctx_tiles), _mod_spec(d, n_ctx_tiles),
                  pl.BlockSpec((1, d), lambda b, i: (0, 0))],
        out_specs=[lat, lat, ctx],
        out_shape=[jax.ShapeDtypeStruct((bsz, t - n_ctx, d), F32),
                   jax.ShapeDtypeStruct((bsz, t - n_ctx, d), F32),
                   jax.ShapeDtypeStruct((bsz, n_ctx, d), F32)],
        compiler_params=_params("parallel", "parallel"),
        name="combine0",
    )(y_asg, top_w, h1, mod0, mod1, g1_next.reshape(1, d))


def _s5_matrices(a_re, a_im, log_dt, b_re, b_im, c_re, c_im):
    ng, p = a_re.shape[1:]
    gc = b_re.shape[-1]
    ll = S5_CHUNK
    gpb = LANES // gc
    nblk = ng // gpb
    lam_re = jnp.minimum(a_re, S5_EIG_MAX)
    lam_im = a_im
    dt = jnp.exp(log_dt)[..., None]
    j = jnp.arange(ll + 1, dtype=F32).reshape(-1, 1, 1, 1)
    mag = jnp.exp(j * (lam_re * dt))
    pw_re = mag * jnp.cos(j * (lam_im * dt))
    pw_im = mag * jnp.sin(j * (lam_im * dt))
    ab_re, ab_im = pw_re[1], pw_im[1]
    den = lam_re * lam_re + lam_im * lam_im
    coef_re = ((ab_re - 1.0) * lam_re + ab_im * lam_im) / den
    coef_im = (ab_im * lam_re - (ab_re - 1.0) * lam_im) / den
    bb_re = coef_re[..., None] * b_re - coef_im[..., None] * b_im
    bb_im = coef_re[..., None] * b_im + coef_im[..., None] * b_re
    drv_re = pw_re[..., None] * bb_re - pw_im[..., None] * bb_im
    drv_im = pw_re[..., None] * bb_im + pw_im[..., None] * bb_re
    rd_re = c_re * pw_re[:, :, :, None, :] - c_im * pw_im[:, :, :, None, :]
    rd_im = -(c_re * pw_im[:, :, :, None, :] + c_im * pw_re[:, :, :, None, :])
    taps = (jnp.einsum('dgop,jdgpi->jdgoi', c_re, drv_re[:ll], precision=HIGHEST)
            - jnp.einsum('dgop,jdgpi->jdgoi', c_im, drv_im[:ll], precision=HIGHEST))
    r = jnp.arange(ll)
    c_in, c_drv, c_rd = [], [], []
    for d in range(2):
        lag = (r[None, :] - r[:, None]) if d == 0 else (r[:, None] - r[None, :])
        tp = jnp.where((lag >= 0)[:, :, None, None, None],
                       taps[:, d][jnp.clip(lag, 0, ll - 1)], 0.0)
        tp = tp.reshape(ll, ll, nblk, gpb, gc, gc).transpose(2, 0, 3, 5, 1, 4)
        c_in.append(tp.reshape(nblk, ll * LANES, ll * gc))
        steps = (ll - 1 - r) if d == 0 else r
        dr = jnp.stack([drv_re[:, d][steps], drv_im[:, d][steps]], axis=1)
        dr = dr.reshape(ll, 2, nblk, gpb, p, gc).transpose(2, 0, 3, 5, 1, 4)
        c_drv.append(dr.reshape(nblk, ll * LANES, 2 * p))
        steps = (r + 1) if d == 0 else (ll - r)
        rd = jnp.stack([rd_re[:, d][steps], rd_im[:, d][steps]], axis=1)
        rd = rd.reshape(ll, 2, nblk, gpb, gc, p).transpose(2, 1, 3, 5, 0, 4)
        c_rd.append(rd.reshape(nblk, 2 * gpb * p, ll * gc))

    def expand(compact, row_unit, col_unit):
        compact = jnp.stack(compact).astype(BF16)
        n_r, n_c = compact.shape[2], compact.shape[3] * gpb
        col = jnp.arange(n_c)
        src = (col // (col_unit * gpb)) * col_unit + col % col_unit
        spread = (jnp.arange(n_c // gpb)[:, None] == src[None, :]).astype(BF16)
        full = jnp.einsum('dbrk,kc->dbrc', compact, spread, preferred_element_type=F32)
        same = ((jnp.arange(n_r) // row_unit) % gpb)[:, None] == ((col // col_unit) % gpb)[None, :]
        return jnp.where(same, full, 0.0).astype(BF16)

    a8 = jnp.stack([pw_re[ll], pw_im[ll]], axis=1).reshape(2, 2, nblk, gpb * p).transpose(0, 2, 1, 3)
    return expand(c_in, gc, gc), expand(c_drv, gc, p), expand(c_rd, p, gc), a8


def _s5_kernel(uc_ref, ul_ref, min_ref, mdrv_ref, mrd_ref, a8_ref, y_ref, v_ref, st_ref, *, n_ctx_cols):
    bsz, n_oct, ll, n_col, lanes = ul_ref.shape
    n_rows = bsz * n_oct * n_col
    half = st_ref.shape[-1] // 2
    d = pl.program_id(0)
    k = pl.program_id(2)

    @pl.when(k == 0)
    def _():
        st_ref[...] = jnp.zeros_like(st_ref)

    def chunks(u_ref):
        return jnp.concatenate([u_ref[:, :, r].reshape(n_rows, lanes) for r in range(ll)], axis=-1)

    x = jnp.where(k == 0, chunks(uc_ref), chunks(ul_ref)).astype(BF16)
    inj_all = jnp.dot(x, mdrv_ref[0, 0], preferred_element_type=F32)
    n_pl = v_ref.shape[0]
    for c in range(n_pl):
        v_ref[c] = inj_all[:, c * lanes:(c + 1) * lanes]

    a_re = jnp.broadcast_to(a8_ref[0, 0, 0:1], (bsz, half))
    a_im = jnp.broadcast_to(a8_ref[0, 0, 1:2], (bsz, half))
    def scan(n_steps):
        def step(i, carry):
            s_re, s_im = carry
            i = jnp.where(d == 0, i, n_steps - 1 - i)
            row = (i % n_oct) * n_col + i // n_oct
            rows = pl.ds(row, bsz, stride=n_oct * n_col)
            inj = jnp.concatenate([v_ref[c, rows, :] for c in range(n_pl)], axis=-1)
            for c in range(n_pl // 2):
                v_ref[c, rows, :] = s_re[:, c * lanes:(c + 1) * lanes]
                v_ref[n_pl // 2 + c, rows, :] = s_im[:, c * lanes:(c + 1) * lanes]
            return (a_re * s_re - a_im * s_im + inj[:, :half],
                    a_re * s_im + a_im * s_re + inj[:, half:])

        s_re, s_im = lax.fori_loop(0, n_steps, step, (st_ref[:, :half], st_ref[:, half:]), unroll=2)
        st_ref[:, :half] = s_re
        st_ref[:, half:] = s_im

    @pl.when(k == 0)
    def _():
        scan(n_ctx_cols * n_oct)

    @pl.when(k > 0)
    def _():
        scan(n_col * n_oct)

    s_start = jnp.concatenate([v_ref[c] for c in range(n_pl)], axis=-1).astype(BF16)
    y = (jnp.dot(x, min_ref[0, 0], preferred_element_type=F32)
         + jnp.dot(s_start, mrd_ref[0, 0], preferred_element_type=F32))
    for r in range(ll):
        y_ref[0, :, :, r] = y[:, r * lanes:(r + 1) * lanes].reshape(bsz, n_oct, n_col, lanes)


def _s5_scan(a_lat, a_ctx, mats):
    bsz, t, d = a_lat.shape
    n_ctx = a_ctx.shape[1]
    m_in, m_drv, m_rd, a8 = mats
    n_rows = t // GRID_W
    n_oct = n_rows // S5_CHUNK
    n_ctx_cols = n_ctx // n_rows
    u_lat = a_lat.reshape(bsz, n_oct, S5_CHUNK, GRID_W, d)
    u_ctx = a_ctx.reshape(bsz, n_ctx_cols, n_rows, d).transpose(0, 2, 1, 3)
    u_ctx = jnp.pad(u_ctx, ((0, 0), (0, 0), (0, S5_COLS - n_ctx_cols), (0, 0)))
    u_ctx = u_ctx.reshape(bsz, n_oct, S5_CHUNK, S5_COLS, d)
    n_lat = GRID_W // S5_COLS
    nblk = d // LANES

    def lat_tile(dd, k):
        kk = jnp.maximum(k, 1) - 1
        return jnp.where(dd == 0, kk, n_lat - 1 - kk)

    blk = (bsz, n_oct, S5_CHUNK, S5_COLS, LANES)
    wspec = lambda shape: pl.BlockSpec((1, 1) + shape, lambda dd, j, k: (dd, j, 0, 0))
    kl = S5_CHUNK * LANES
    ns = m_drv.shape[-1]
    y = pl.pallas_call(
        functools.partial(_s5_kernel, n_ctx_cols=n_ctx_cols),
        grid=(2, nblk, n_lat + 1),
        in_specs=[pl.BlockSpec(blk, lambda dd, j, k: (0, 0, 0, 0, j)),
                  pl.BlockSpec(blk, lambda dd, j, k: (0, 0, 0, lat_tile(dd, k), j)),
                  wspec((kl, kl)), wspec((kl, ns)), wspec((ns, kl)),
                  pl.BlockSpec((1, 1, 2, ns // 2), lambda dd, j, k: (dd, j, 0, 0))],
        out_specs=pl.BlockSpec((1,) + blk, lambda dd, j, k: (dd, 0, 0, 0, lat_tile(dd, k), j)),
        out_shape=jax.ShapeDtypeStruct((2,) + u_lat.shape, F32),
        scratch_shapes=[pltpu.VMEM((ns // LANES, bsz * n_oct * S5_COLS, LANES), F32),
                        pltpu.VMEM((bsz, ns), F32)],
        compiler_params=_params("parallel", "parallel", "arbitrary"),
        name="s5_scan",
    )(u_ctx, u_lat, m_in, m_drv, m_rd, a8)
    return y.reshape(2, bsz, n_rows, GRID_W, d)


def _s5_glu_kernel(y_ref, u_ref, h_ref, m_ref, dsk_ref, w_ref, bg_ref, g2_ref, rw_ref, rb_ref,
                   h1_ref, f_ref, tw_ref, meta_ref, cnt_ref, run_ref):
    d = h_ref.shape[-1]
    n = h_ref.shape[1] * h_ref.shape[2]
    y = (y_ref[0, 0] + y_ref[1, 0] + dsk_ref[...] * u_ref[0]).reshape(n, d)
    z = jnp.dot(jax.nn.gelu(y).astype(BF16), w_ref[...], preferred_element_type=F32) + bg_ref[...]
    y = z[:, :d] * _sigmoid(z[:, d:])
    first = (pl.program_id(0) == 0) & (pl.program_id(1) == 0)
    _mixer_tail(first, h_ref[0].reshape(n, d), y, m_ref[0], g2_ref[...], rw_ref[...], rb_ref[...],
                run_ref, h1_ref, f_ref, tw_ref.at[0], meta_ref, cnt_ref)


def _s5_glu(y, a_lat, h_lat, mod, d_skip, w_glu, b_glu, g2, rw, rb):
    bsz, t, d = h_lat.shape
    n_rows = t // GRID_W
    n_e = rw.shape[-1]
    n_tok = n_rows * S5_COLS
    nt = GRID_W // S5_COLS
    lat = pl.BlockSpec((1, n_rows, S5_COLS, d), lambda b, i: (b, 0, i, 0))
    tail_specs, tail_shapes = _tail_outputs(bsz * nt, n_tok, d, n_e, lambda b, i: b * nt + i)
    const = lambda shape: pl.BlockSpec(shape, lambda b, i: (0,) * len(shape))
    return pl.pallas_call(
        _s5_glu_kernel,
        grid=(bsz, nt),
        in_specs=[pl.BlockSpec((2, 1, n_rows, S5_COLS, d), lambda b, i: (0, b, 0, i, 0)),
                  lat, lat, pl.BlockSpec((1, 6, d), lambda b, i: (2 * b + 1, 0, 0)),
                  const((1, d)), const((d, 2 * d)), const((1, 2 * d)), const((1, d)),
                  const((d, n_e)), const((1, n_e))],
        out_specs=[lat] + tail_specs,
        out_shape=[jax.ShapeDtypeStruct((bsz, n_rows, GRID_W, d), F32)] + tail_shapes,
        scratch_shapes=[pltpu.VMEM((1, n_e), F32)],
        compiler_params=_params("arbitrary", "arbitrary"),
        name="s5_glu",
    )(y, a_lat.reshape(bsz, n_rows, GRID_W, d), h_lat.reshape(bsz, n_rows, GRID_W, d), mod,
      d_skip.reshape(1, d), w_glu.astype(BF16), b_glu.reshape(1, 2 * d),
      g2.reshape(1, d), rw, rb.reshape(1, n_e))


def _combine1_kernel(y_ref, tw_ref, h_ref, m_ref, g_ref, o_ref):
    d = h_ref.shape[-1]
    n = h_ref.shape[1] * h_ref.shape[2]
    h2 = h_ref[0].reshape(n, d) + m_ref[0][5:6] * _combine_experts(y_ref, tw_ref, n, d)
    o_ref[0] = _rms(h2, g_ref[...]).reshape(o_ref.shape[1:])


def _combine1(y_asg, top_w, h1, mod, final_g):
    bsz, n_rows, _, d = h1.shape
    nt = GRID_W // S5_COLS
    n_tok = n_rows * S5_COLS
    lat = pl.BlockSpec((1, n_rows, S5_COLS, d), lambda b, i: (b, 0, i, 0))
    return pl.pallas_call(
        _combine1_kernel,
        grid=(bsz, nt),
        in_specs=[pl.BlockSpec((n_tok * TOP_K * d // LANES, LANES), lambda b, i: (b * nt + i, 0)),
                  pl.BlockSpec((1, n_tok, TOP_K), lambda b, i: (b * nt + i, 0, 0)),
                  lat, pl.BlockSpec((1, 6, d), lambda b, i: (2 * b + 1, 0, 0)),
                  pl.BlockSpec((1, d), lambda b, i: (0, 0))],
        out_specs=lat,
        out_shape=jax.ShapeDtypeStruct(h1.shape, F32),
        compiler_params=_params("parallel", "parallel"),
        name="combine1",
    )(y_asg, top_w, h1, mod, final_g.reshape(1, d))


def kernel(x, c, ctx, c_ctx, mod_w, mod_b, norm1_g, norm2_g, hg_w_in, hg_lb_logits, hg_norm_g,
           hg_w_out, s5_a_re, s5_a_im, s5_log_dt, s5_b_re, s5_b_im, s5_c_re, s5_c_im, s5_d, s5_w_glu,
           s5_b_glu, router_w, router_b, moe_w1, moe_b1, moe_w2, moe_b2, final_g):
    bsz, seq, d = x.shape
    n_ctx = ctx.shape[1]
    assert mod_w.shape[0] == 2 and n_ctx % TOK_TILE == 0 and seq % TOK_TILE == 0
    n_ctx_tiles = n_ctx // TOK_TILE
    lb_all = jnp.cumsum(jax.nn.softmax(hg_lb_logits.astype(F32), axis=0), axis=0)
    mod = _modulation(c, c_ctx, mod_w, mod_b)

    qs, v, sg, kf, kb, lff, lfb = _hg_proj(ctx, x, mod[0], norm1_g[0], hg_w_in[0], lb_all[0], n_ctx_tiles)
    o_f, o_b = _gla(qs, v, kf, kb, lff, lfb, n_ctx // HG_CHUNK)
    h1, f, top_w, meta, counts = _hg_readout(o_f, o_b, sg, ctx, x, mod[0], hg_norm_g[0], hg_w_out[0],
                                             norm2_g[0], router_w[0], router_b[0], n_ctx_tiles)
    y_asg = _moe(f, meta, counts, 0, moe_w1, moe_b1, moe_w2, moe_b2)
    h_lat, a_lat, a_ctx = _combine0(y_asg, top_w, h1, mod[0], mod[1], norm1_g[1], n_ctx_tiles)

    mats = _s5_matrices(s5_a_re[0], s5_a_im[0], s5_log_dt[0], s5_b_re[0], s5_b_im[0],
                        s5_c_re[0], s5_c_im[0])
    y_s5 = _s5_scan(a_lat, a_ctx, mats)
    h1, f, top_w, meta, counts = _s5_glu(y_s5, a_lat, h_lat, mod[1], s5_d[0], s5_w_glu[0], s5_b_glu[0],
                                         norm2_g[1], router_w[1], router_b[1])
    y_asg = _moe(f, meta, counts, 1, moe_w1, moe_b1, moe_w2, moe_b2)
    out = _combine1(y_asg, top_w, h1, mod[1], final_g)
    return out.reshape(bsz, seq, d)
```

```python
import functools

import jax
import jax.numpy as jnp
from jax import lax
from jax.experimental import pallas as pl
from jax.experimental.pallas import tpu as pltpu

F32 = jnp.float32
BF16 = jnp.bfloat16
HIGHEST = lax.Precision.HIGHEST

RMS_EPS = 1e-6
GRID_W = 64
HEAD_DIM = 128
HG_CHUNK = 64
HG_BATCH = 2
S5_GROUP = 16
S5_STATE = 64
S5_EIG_MAX = -1e-4
S5_CHUNK = 8
S5_COLS = 8
LANES = 128
N_EXPERTS = 32
TOP_K = 4
SWIGLU_ALPHA = 1.702
SWIGLU_LIMIT = 7.0
TOK_TILE = 256
MOE_ROWS = 256
VMEM_LIMIT = 56 * 1024 * 1024


def _params(*sem):
    return pltpu.CompilerParams(dimension_semantics=sem, vmem_limit_bytes=VMEM_LIMIT)


def _rms(x, g):
    return x * lax.rsqrt(jnp.mean(x * x, axis=-1, keepdims=True) + RMS_EPS) * g


def _sigmoid(x):
    return 1.0 / (1.0 + jnp.exp(-x))


def _silu(x):
    return x * _sigmoid(x)


def _to_token_tiles(ref, x):
    n, d = x.shape
    for c in range(d // LANES):
        ref[pl.ds(c, n, stride=d // LANES), :] = x[:, c * LANES:(c + 1) * LANES]


def _from_token_tiles(ref, n, d):
    return jnp.concatenate([ref[pl.ds(c, n, stride=d // LANES), :] for c in range(d // LANES)], axis=-1)


def _mod_kernel(c_ref, w_ref, b_ref, o_ref):
    s = _silu(c_ref[...])
    o_ref[0] = jnp.dot(s, w_ref[0], precision=HIGHEST, preferred_element_type=F32) + b_ref[0]


def _modulation(c, c_ctx, mod_w, mod_b):
    bsz, d = c.shape
    depth = mod_w.shape[0]
    rows = jnp.concatenate([c, c_ctx[None], jnp.zeros((16 - bsz - 1, d), F32)], axis=0)
    bn = 6 * d // 4
    out = pl.pallas_call(
        _mod_kernel,
        grid=(depth, 4),
        in_specs=[pl.BlockSpec((16, d), lambda l, j: (0, 0)),
                  pl.BlockSpec((1, d, bn), lambda l, j: (l, 0, j)),
                  pl.BlockSpec((1, 1, bn), lambda l, j: (l, 0, j))],
        out_specs=pl.BlockSpec((1, 16, bn), lambda l, j: (l, 0, j)),
        out_shape=jax.ShapeDtypeStruct((depth, 16, 6 * d), F32),
        compiler_params=_params("parallel", "parallel"),
        name="modulation",
    )(rows, mod_w, mod_b.reshape(depth, 1, 6 * d))
    m_lat = out[:, :bsz].reshape(depth, bsz, 1, 6, d)
    m_ctx = jnp.broadcast_to(out[:, bsz].reshape(depth, 1, 1, 6, d), m_lat.shape)
    return jnp.concatenate([m_ctx, m_lat], axis=2).reshape(depth, 2 * bsz, 6, d)


def _residual_tile(hc_ref, hx_ref, n_ctx_tiles):
    return jnp.where(pl.program_id(1) < n_ctx_tiles, hc_ref[0], hx_ref[0])


def _residual_specs(d, n_ctx_tiles):
    return [pl.BlockSpec((1, TOK_TILE, d), lambda b, i: (b, jnp.minimum(i, n_ctx_tiles - 1), 0)),
            pl.BlockSpec((1, TOK_TILE, d), lambda b, i: (b, jnp.maximum(i - n_ctx_tiles, 0), 0))]


def _hg_proj_kernel(hc_ref, hx_ref, m_ref, g_ref, w_ref, lb_ref,
                    qs_ref, v_ref, sg_ref, kf_ref, kb_ref, lff_ref, lfb_ref, *, n_ctx_tiles):
    d = hx_ref.shape[-1]
    m = m_ref[0]
    a = _rms(_residual_tile(hc_ref, hx_ref, n_ctx_tiles), g_ref[...]) * (1.0 + m[1:2]) + m[0:1]
    p = jnp.dot(a.astype(BF16), w_ref[...], preferred_element_type=F32)
    qs_ref[0] = _silu(p[:, 0:d]).astype(BF16)
    v_ref[0] = p[:, d:2 * d].astype(BF16)
    sg_ref[0] = _silu(p[:, 4 * d:5 * d]).astype(BF16)
    for di, (k_ref, lf_ref) in enumerate(((kf_ref, lff_ref), (kb_ref, lfb_ref))):
        lb = lb_ref[di:di + 1]
        f = lb + (1.0 - lb) * _sigmoid(p[:, (2 + di) * d:(3 + di) * d])
        k_ref[0] = (1.0 - f).astype(BF16)
        lf_ref[0] = jnp.log(f)


def _mod_spec(d, n_ctx_tiles):
    return pl.BlockSpec((1, 6, d), lambda b, i: (2 * b + (i >= n_ctx_tiles).astype(jnp.int32), 0, 0))


def _hg_proj(ctx, x, mod, g1, w_in, lb, n_ctx_tiles):
    bsz, t, d = x.shape
    t += ctx.shape[1]
    tok = pl.BlockSpec((1, TOK_TILE, d), lambda b, i: (b, i, 0))
    bf = jax.ShapeDtypeStruct((bsz, t, d), BF16)
    ff = jax.ShapeDtypeStruct((bsz, t, d), F32)
    return pl.pallas_call(
        functools.partial(_hg_proj_kernel, n_ctx_tiles=n_ctx_tiles),
        grid=(bsz, t // TOK_TILE),
        in_specs=_residual_specs(d, n_ctx_tiles) + [_mod_spec(d, n_ctx_tiles),
                  pl.BlockSpec((1, d), lambda b, i: (0, 0)),
                  pl.BlockSpec((d, 5 * d), lambda b, i: (0, 0)),
                  pl.BlockSpec((2, d), lambda b, i: (0, 0))],
        out_specs=[tok] * 7,
        out_shape=[bf, bf, bf, bf, bf, ff, ff],
        compiler_params=_params("parallel", "parallel"),
        name="hg_proj",
    )(ctx, x, mod, g1.reshape(1, d), w_in.astype(BF16), lb)


def _split3(x):
    hi = x.astype(BF16)
    r = x - hi.astype(F32)
    mid = r.astype(BF16)
    lo = (r - mid.astype(F32)).astype(BF16)
    return hi, mid, lo


def _gla_kernel(qf_ref, vf_ref, kf_ref, lf_ref, qb_ref, vb_ref, kb_ref, lb_ref,
                of_ref, ob_ref, sf_ref, sb_ref):
    c = HG_CHUNK
    n_b = qf_ref.shape[0]
    n_heads = sf_ref.shape[0] // n_b

    @pl.when(pl.program_id(1) == 0)
    def _():
        sf_ref[...] = jnp.zeros_like(sf_ref)
        sb_ref[...] = jnp.zeros_like(sb_ref)

    row = lax.broadcasted_iota(jnp.int32, (c, c), 0)
    col = lax.broadcasted_iota(jnp.int32, (c, c), 1)
    dirs = ((qf_ref, vf_ref, kf_ref, lf_ref, of_ref, sf_ref, col <= row, c // 2 - 1, c - 1),
            (qb_ref, vb_ref, kb_ref, lb_ref, ob_ref, sb_ref, col >= row, c // 2, 0))
    for bb, (q_ref, v_ref, k_ref, l_ref, o_ref, s_ref, keep, r_ref, r_last) in (
            (bb, dr) for bb in range(n_b) for dr in dirs):
        tri = keep.astype(BF16)
        b = sum(jnp.dot(tri, part, preferred_element_type=F32) for part in _split3(l_ref[bb]))
        b_ref = b[r_ref:r_ref + 1]
        b_last = b[r_last:r_last + 1]
        a_in = q_ref[bb].astype(F32) * jnp.exp(b - b_ref)
        k_in = k_ref[bb].astype(F32) * jnp.exp(b_ref - b)
        q_st = (a_in * jnp.exp(b_ref)).astype(BF16)
        k_st = (k_in * jnp.exp(b_last - b_ref)).astype(BF16)
        a_in = a_in.astype(BF16)
        k_in = k_in.astype(BF16)
        decay = jnp.exp(b_last)
        v = v_ref[bb]
        for h in range(n_heads):
            sl = slice(h * HEAD_DIM, (h + 1) * HEAD_DIM)
            sc = lax.dot_general(a_in[:, sl], k_in[:, sl], (((1,), (1,)), ((), ())),
                                 preferred_element_type=F32)
            sc = jnp.where(keep, sc, 0.0).astype(BF16)
            o = jnp.dot(sc, v[:, sl], preferred_element_type=F32)
            st = s_ref[bb * n_heads + h]
            o = o + lax.dot_general(q_st[:, sl], st.astype(BF16), (((1,), (1,)), ((), ())),
                                    preferred_element_type=F32)
            s_ref[bb * n_heads + h] = st * decay[:, sl] + lax.dot_general(
                v[:, sl], k_st[:, sl], (((0,), (0,)), ((), ())), preferred_element_type=F32)
            o_ref[bb, :, sl] = o


def _gla(qs, v, kf, kb, lff, lfb, n_ctx_chunks):
    bsz, t, d = qs.shape
    n = t // HG_CHUNK
    n_heads = d // HEAD_DIM

    def rev(j):
        return jnp.where(j < n_ctx_chunks, n_ctx_chunks - 1 - j, n + n_ctx_chunks - 1 - j)

    fwd = pl.BlockSpec((HG_BATCH, HG_CHUNK, d), lambda b, j: (b, j, 0))
    bwd = pl.BlockSpec((HG_BATCH, HG_CHUNK, d), lambda b, j: (b, rev(j), 0))
    out = jax.ShapeDtypeStruct((bsz, t, d), F32)
    state = pltpu.VMEM((HG_BATCH * n_heads, HEAD_DIM, HEAD_DIM), F32)
    return pl.pallas_call(
        _gla_kernel,
        grid=(bsz // HG_BATCH, n),
        in_specs=[fwd, fwd, fwd, fwd, bwd, bwd, bwd, bwd],
        out_specs=[fwd, bwd],
        out_shape=[out, out],
        scratch_shapes=[state, state],
        compiler_params=_params("parallel", "arbitrary"),
        name="gla",
    )(qs, v, kf, lff, qs, v, kb, lfb)


def _mixer_tail(first, h, y, m, g2, rw, rb, run_ref, h1_ref, f_ref, tw_ref, meta_ref, cnt_ref):
    h1 = h + m[2:3] * y
    f = _rms(h1, g2) * (1.0 + m[4:5]) + m[3:4]
    h1_ref[...] = h1.reshape(h1_ref.shape)
    _to_token_tiles(f_ref, f)
    f_hi, f_lo, _ = _split3(f)
    w_hi, w_lo, _ = _split3(rw)
    logits = (jnp.dot(f_hi, w_hi, preferred_element_type=F32) + jnp.dot(f_hi, w_lo, preferred_element_type=F32)
              + jnp.dot(f_lo, w_hi, preferred_element_type=F32)) + rb
    n, n_e = logits.shape
    lane = lax.broadcasted_iota(jnp.int32, logits.shape, 1).astype(F32)
    vals, idxs, hots = [], [], []
    for _ in range(TOP_K):
        mx = jnp.max(logits, axis=-1, keepdims=True)
        ix = jnp.min(jnp.where(logits == mx, lane, float(n_e)), axis=-1, keepdims=True)
        hot = lane == ix
        vals.append(mx)
        idxs.append(ix)
        hots.append(hot)
        logits = jnp.where(hot, -jnp.inf, logits)
    es = [jnp.exp(x - vals[0]) for x in vals]
    tot = sum(es)
    for k in range(TOP_K):
        tw_ref[:, k:k + 1] = es[k] / tot

    @pl.when(first)
    def _():
        run_ref[...] = jnp.zeros_like(run_ref)

    picked = sum(hot.astype(F32) for hot in hots)
    r_i = lax.broadcasted_iota(jnp.int32, (n, n), 0)
    c_i = lax.broadcasted_iota(jnp.int32, (n, n), 1)
    earlier = jnp.dot((c_i < r_i).astype(BF16), picked.astype(BF16), preferred_element_type=F32)
    rank = earlier + run_ref[...]
    total = run_ref[...] + jnp.sum(picked, axis=0, keepdims=True)
    run_ref[...] = total
    cnt_ref[...] = total
    col = lax.broadcasted_iota(jnp.int32, (n, LANES), 1)
    z = jnp.zeros((n, LANES), F32)
    for k in range(TOP_K):
        pos = jnp.sum(jnp.where(hots[k], rank, 0.0), axis=-1, keepdims=True)
        z = jnp.where(col == k, idxs[k], z)
        z = jnp.where(col == TOP_K + k, pos, z)
    meta_ref[0] = z.T[0:2 * TOP_K].astype(jnp.int32)


def _tail_outputs(n_tiles, tile, d, n_e, index):
    rt = d // LANES
    specs = [pl.BlockSpec((tile * rt, LANES), lambda b, i: (index(b, i), 0)),
             pl.BlockSpec((1, tile, TOP_K), lambda b, i: (index(b, i), 0, 0)),
             pl.BlockSpec((1, 2 * TOP_K, tile), lambda b, i: (index(b, i), 0, 0)),
             pl.BlockSpec((1, n_e), lambda b, i: (0, 0))]
    shapes = [jax.ShapeDtypeStruct((n_tiles * tile * rt, LANES), F32),
              jax.ShapeDtypeStruct((n_tiles, tile, TOP_K), F32),
              jax.ShapeDtypeStruct((n_tiles, 2 * TOP_K, tile), jnp.int32),
              jax.ShapeDtypeStruct((1, n_e), F32)]
    return specs, shapes


def _hg_readout_kernel(of_ref, ob_ref, sg_ref, hc_ref, hx_ref, m_ref, ng_ref, w_ref, g2_ref, rw_ref, rb_ref,
                       h1_ref, f_ref, tw_ref, meta_ref, cnt_ref, run_ref, *, n_ctx_tiles):
    d = hx_ref.shape[-1]
    o = of_ref[0] + ob_ref[0]
    parts = []
    for h in range(d // HEAD_DIM):
        oh = o[:, h * HEAD_DIM:(h + 1) * HEAD_DIM]
        parts.append(oh * lax.rsqrt(jnp.mean(oh * oh, axis=-1, keepdims=True) + RMS_EPS))
    o = jnp.concatenate(parts, axis=-1) * ng_ref[...]
    y = jnp.dot((o * sg_ref[0].astype(F32)).astype(BF16), w_ref[...], preferred_element_type=F32)
    first = (pl.program_id(0) == 0) & (pl.program_id(1) == 0)
    _mixer_tail(first, _residual_tile(hc_ref, hx_ref, n_ctx_tiles), y, m_ref[0], g2_ref[...], rw_ref[...],
                rb_ref[...], run_ref, h1_ref, f_ref, tw_ref.at[0], meta_ref, cnt_ref)


def _hg_readout(o_f, o_b, sg, ctx, x, mod, norm_g, w_out, g2, rw, rb, n_ctx_tiles):
    bsz, t, d = o_f.shape
    n_e = rw.shape[-1]
    nt = t // TOK_TILE
    tok = pl.BlockSpec((1, TOK_TILE, d), lambda b, i: (b, i, 0))
    const = lambda shape: pl.BlockSpec(shape, lambda b, i: (0,) * len(shape))
    tail_specs, tail_shapes = _tail_outputs(bsz * nt, TOK_TILE, d, n_e, lambda b, i: b * nt + i)
    return pl.pallas_call(
        functools.partial(_hg_readout_kernel, n_ctx_tiles=n_ctx_tiles),
        grid=(bsz, nt),
        in_specs=[tok, tok, tok] + _residual_specs(d, n_ctx_tiles) + [_mod_spec(d, n_ctx_tiles),
                  const((1, d)), const((d, d)), const((1, d)), const((d, n_e)), const((1, n_e))],
        out_specs=[tok] + tail_specs,
        out_shape=[jax.ShapeDtypeStruct((bsz, t, d), F32)] + tail_shapes,
        scratch_shapes=[pltpu.VMEM((1, n_e), F32)],
        compiler_params=_params("arbitrary", "arbitrary"),
        name="hg_readout",
    )(o_f, o_b, sg, ctx, x, mod, norm_g.reshape(1, d), w_out.astype(BF16), g2.reshape(1, d),
      rw, rb.reshape(1, n_e))


def _moe_plan(counts, n_asg, rows):
    counts = counts.reshape(-1).astype(jnp.int32)
    padded = (counts + rows - 1) // rows * rows
    pad_end = jnp.cumsum(padded)
    n_blocks = -(-n_asg // rows) + N_EXPERTS
    block_pos = jnp.arange(n_blocks, dtype=jnp.int32) * rows
    block_e = jnp.minimum(jnp.sum((block_pos[:, None] >= pad_end[None, :]).astype(jnp.int32), axis=1),
                          N_EXPERTS - 1)
    n_used = (pad_end[-1:] // rows).astype(jnp.int32)
    return pad_end - padded, block_e, n_used, n_blocks


def _invert_kernel(dest_ref, init_hbm, inv_ref, sem):
    i = pl.program_id(0)
    tile = dest_ref.shape[2]

    @pl.when(i == 0)
    def _():
        cp = pltpu.make_async_copy(init_hbm, inv_ref, sem)
        cp.start()
        cp.wait()

    base = i * (tile * TOP_K)
    for r in range(tile):
        for k in range(TOP_K):
            inv_ref[dest_ref[0, k, r]] = base + (r * TOP_K + k)


def _invert(meta, slot_start, n_slots):
    n_tiles, _, tile = meta.shape
    hot = meta[:, :TOP_K, :, None] == jnp.arange(N_EXPERTS, dtype=jnp.int32)
    dest = meta[:, TOP_K:] + jnp.sum(jnp.where(hot, slot_start, 0), axis=-1)
    return pl.pallas_call(
        _invert_kernel,
        grid=(n_tiles,),
        in_specs=[pl.BlockSpec((1, TOP_K, tile), lambda i: (i, 0, 0), memory_space=pltpu.SMEM),
                  pl.BlockSpec(memory_space=pl.ANY)],
        out_specs=pl.BlockSpec(memory_space=pltpu.SMEM),
        out_shape=jax.ShapeDtypeStruct((n_slots,), jnp.int32),
        scratch_shapes=[pltpu.SemaphoreType.DMA(())],
        compiler_params=_params("arbitrary"),
        name="moe_invert",
    )(dest.astype(jnp.int32), jnp.full((n_slots,), -1, jnp.int32))


def _ffn_kernel(be_ref, nu_ref, tok_ref, tok_next_ref, dst_prev_ref, dst_ref, x_hbm,
                w1_ref, b1_ref, w2_ref, b2_ref, y_hbm, buf_p, buf_q, zbuf, w1c, w2c, gsem, ssem, zsem):
    xbuf = (buf_p.at[0], buf_q.at[0])
    ybuf = (buf_q.at[1], buf_p.at[1])
    b = pl.program_id(0)
    nu = nu_ref[0]
    d, f2 = w1c.shape
    rt = d // LANES
    rows = buf_p.shape[1] // rt

    def tile_of(ref, idx):
        return ref.at[pl.ds(pl.multiple_of(idx * rt, rt), rt)]

    def gather(idx_ref, s):
        for r in range(rows):
            pltpu.make_async_copy(tile_of(x_hbm, idx_ref[0, 0, r]), xbuf[s].at[pl.ds(r * rt, rt)],
                                  gsem.at[s]).start(priority=r % 2)

    def scatter(idx_ref, s):
        for r in range(rows):
            pltpu.make_async_copy(ybuf[s].at[pl.ds(r * rt, rt)], tile_of(y_hbm, idx_ref[0, 0, r]),
                                  ssem.at[s]).start(priority=r % 2)

    def wait_gather(s):
        pltpu.make_async_copy(x_hbm.at[pl.ds(0, rows * rt)], xbuf[s], gsem.at[s]).wait()

    def wait_scatter(s):
        pltpu.make_async_copy(ybuf[s], y_hbm.at[pl.ds(0, rows * rt)], ssem.at[s]).wait()

    @pl.when(b == 0)
    def _():
        zbuf[...] = jnp.zeros_like(zbuf)
        buf_p[1] = jnp.zeros(buf_p.shape[1:], F32)
        gather(tok_ref, 0)

    @pl.when(b >= nu)
    def _():
        dst = y_hbm.at[pl.ds(pl.multiple_of(b * (rows * rt), rows * rt), rows * rt)]
        cp = pltpu.make_async_copy(zbuf, dst, zsem)
        cp.start()
        cp.wait()

    def used_block(s):
        @pl.when((b == 0) | (be_ref[b] != be_ref[jnp.maximum(b - 1, 0)]))
        def _():
            w1c[...] = w1_ref[0, 0].astype(BF16)
            w2c[...] = w2_ref[0, 0].astype(BF16)

        wait_gather(s)

        @pl.when(b >= 1)
        def _():
            wait_scatter(s)

        x_head = xbuf[s][0:rt]
        x = _from_token_tiles(xbuf[s], rows, d).astype(BF16)
        gather(tok_next_ref, 1 - s)
        scatter(dst_prev_ref, 1 - s)
        z = jnp.dot(x, w1c[...], preferred_element_type=F32) + b1_ref[0, 0]
        z_glu = jnp.minimum(z[:, :f2 // 2], SWIGLU_LIMIT)
        z_lin = jnp.clip(z[:, f2 // 2:], -SWIGLU_LIMIT, SWIGLU_LIMIT)
        act = z_glu * _sigmoid(SWIGLU_ALPHA * z_glu) * (z_lin + 1.0)
        y = jnp.dot(act.astype(BF16), w2c[...], preferred_element_type=F32) + b2_ref[0, 0]
        _to_token_tiles(ybuf[s], y)
        xbuf[s][0:rt] = x_head

        @pl.when(b == nu - 1)
        def _():
            scatter(dst_ref, s)
            wait_gather(1 - s)
            wait_scatter(1 - s)
            wait_scatter(s)

    for s in range(2):
        pl.when((b < nu) & (b % 2 == s))(functools.partial(used_block, s))


def _moe(f_tiles, meta, counts, layer, w1, b1, w2, b2):
    n_tiles, _, tile = meta.shape
    n_asg = n_tiles * tile * TOP_K
    _, n_e, d, f2 = w1.shape
    rt = d // LANES
    slot_start, block_e, n_used, n_blocks = _moe_plan(counts, n_asg, MOE_ROWS)
    n_slots = n_blocks * MOE_ROWS
    inv = _invert(meta, slot_start, n_slots)
    is_pad = inv < 0
    slot_tok = jnp.where(is_pad, 0, inv // TOP_K).reshape(n_blocks, 1, MOE_ROWS)
    spill = n_asg + jnp.cumsum(is_pad.astype(jnp.int32)) - 1
    slot_dst = jnp.where(is_pad, spill, inv)
    slot_dst = jnp.concatenate([slot_dst, jnp.arange(n_slots - MOE_ROWS, n_slots, dtype=jnp.int32)])
    slot_dst = slot_dst.reshape(n_blocks + 1, 1, MOE_ROWS)

    live = lambda b, nu: jnp.minimum(b, nu[0] - 1)
    smem = lambda imap: pl.BlockSpec((1, 1, MOE_ROWS), imap, memory_space=pltpu.SMEM)
    per_e = lambda shape: pl.BlockSpec((1, 1) + shape, lambda b, be, nu: (layer, be[live(b, nu)], 0, 0))
    buf = pltpu.VMEM((MOE_ROWS * rt, LANES), F32)
    bufs = pltpu.VMEM((2, MOE_ROWS * rt, LANES), F32)
    grid_spec = pltpu.PrefetchScalarGridSpec(
        num_scalar_prefetch=2,
        grid=(n_blocks,),
        in_specs=[smem(lambda b, be, nu: (live(b, nu), 0, 0)),
                  smem(lambda b, be, nu: (live(b + 1, nu), 0, 0)),
                  smem(lambda b, be, nu: (jnp.where(b == 0, n_blocks, live(b - 1, nu)), 0, 0)),
                  smem(lambda b, be, nu: (live(b, nu), 0, 0)),
                  pl.BlockSpec(memory_space=pl.ANY),
                  per_e((d, f2)), per_e((1, f2)), per_e((f2 // 2, d)), per_e((1, d))],
        out_specs=pl.BlockSpec(memory_space=pl.ANY),
        scratch_shapes=[bufs, bufs, buf,
                        pltpu.VMEM((d, f2), BF16), pltpu.VMEM((f2 // 2, d), BF16),
                        pltpu.SemaphoreType.DMA((2,)), pltpu.SemaphoreType.DMA((2,)),
                        pltpu.SemaphoreType.DMA(())])
    return pl.pallas_call(
        _ffn_kernel,
        grid_spec=grid_spec,
        out_shape=jax.ShapeDtypeStruct((n_slots * rt, LANES), F32),
        compiler_params=_params("arbitrary"),
        name="moe_ffn",
    )(block_e, n_used, slot_tok, slot_tok, slot_dst, slot_dst, f_tiles, w1, b1.reshape(b1.shape[0], n_e, 1, f2),
      w2, b2.reshape(b2.shape[0], n_e, 1, d))


def _combine_experts(y_ref, tw_ref, tile, d):
    rt = d // LANES
    tw = tw_ref[0]
    out = 0.0
    for k in range(TOP_K):
        yk = jnp.concatenate([y_ref[pl.ds(k * rt + c, tile, stride=TOP_K * rt), :] for c in range(rt)],
                             axis=-1)
        out = out + tw[:, k:k + 1] * yk
    return out


def _combine0_kernel(y_ref, tw_ref, h_ref, m0_ref, m1_ref, g_ref, hl_ref, al_ref, ac_ref, *, n_ctx_tiles):
    tile, d = h_ref.shape[1:]
    h2 = h_ref[0] + m0_ref[0][5:6] * _combine_experts(y_ref, tw_ref, tile, d)
    m1 = m1_ref[0]
    a = _rms(h2, g_ref[...]) * (1.0 + m1[1:2]) + m1[0:1]
    is_ctx = pl.program_id(1) < n_ctx_tiles

    @pl.when(is_ctx)
    def _():
        ac_ref[0] = a

    @pl.when(jnp.logical_not(is_ctx))
    def _():
        hl_ref[0] = h2
        al_ref[0] = a


def _combine0(y_asg, top_w, h1, mod0, mod1, g1_next, n_ctx_tiles):
    bsz, t, d = h1.shape
    nt = t // TOK_TILE
    n_ctx = n_ctx_tiles * TOK_TILE
    tok = pl.BlockSpec((1, TOK_TILE, d), lambda b, i: (b, i, 0))
    lat = pl.BlockSpec((1, TOK_TILE, d), lambda b, i: (b, jnp.maximum(i - n_ctx_tiles, 0), 0))
    ctx = pl.BlockSpec((1, TOK_TILE, d), lambda b, i: (b, jnp.minimum(i, n_ctx_tiles - 1), 0))
    return pl.pallas_call(
        functools.partial(_combine0_kernel, n_ctx_tiles=n_ctx_tiles),
        grid=(bsz, nt),
        in_specs=[pl.BlockSpec((TOK_TILE * TOP_K * d // LANES, LANES), lambda b, i: (b * nt + i, 0)),
                  pl.BlockSpec((1, TOK_TILE, TOP_K), lambda b, i: (b * nt + i, 0, 0)),
                  tok, _mod_spec(d, n_ctx_tiles), _mod_spec(d, n_ctx_tiles),
                  pl.BlockSpec((1, d), lambda b, i: (0, 0))],
        out_specs=[lat, lat, ctx],
        out_shape=[jax.ShapeDtypeStruct((bsz, t - n_ctx, d), F32),
                   jax.ShapeDtypeStruct((bsz, t - n_ctx, d), F32),
                   jax.ShapeDtypeStruct((bsz, n_ctx, d), F32)],
        compiler_params=_params("parallel", "parallel"),
        name="combine0",
    )(y_asg, top_w, h1, mod0, mod1, g1_next.reshape(1, d))


def _s5_matrices(a_re, a_im, log_dt, b_re, b_im, c_re, c_im):
    ng, p = a_re.shape[1:]
    gc = b_re.shape[-1]
    ll = S5_CHUNK
    gpb = LANES // gc
    nblk = ng // gpb
    lam_re = jnp.minimum(a_re, S5_EIG_MAX)
    lam_im = a_im
    dt = jnp.exp(log_dt)[..., None]
    j = jnp.arange(ll + 1, dtype=F32).reshape(-1, 1, 1, 1)
    mag = jnp.exp(j * (lam_re * dt))
    pw_re = mag * jnp.cos(j * (lam_im * dt))
    pw_im = mag * jnp.sin(j * (lam_im * dt))
    ab_re, ab_im = pw_re[1], pw_im[1]
    den = lam_re * lam_re + lam_im * lam_im
    coef_re = ((ab_re - 1.0) * lam_re + ab_im * lam_im) / den
    coef_im = (ab_im * lam_re - (ab_re - 1.0) * lam_im) / den
    bb_re = coef_re[..., None] * b_re - coef_im[..., None] * b_im
    bb_im = coef_re[..., None] * b_im + coef_im[..., None] * b_re
    drv_re = pw_re[..., None] * bb_re - pw_im[..., None] * bb_im
    drv_im = pw_re[..., None] * bb_im + pw_im[..., None] * bb_re
    rd_re = c_re * pw_re[:, :, :, None, :] - c_im * pw_im[:, :, :, None, :]
    rd_im = -(c_re * pw_im[:, :, :, None, :] + c_im * pw_re[:, :, :, None, :])
    taps = (jnp.einsum('dgop,jdgpi->jdgoi', c_re, drv_re[:ll], precision=HIGHEST)
            - jnp.einsum('dgop,jdgpi->jdgoi', c_im, drv_im[:ll], precision=HIGHEST))
    r = jnp.arange(ll)
    c_in, c_drv, c_rd = [], [], []
    for d in range(2):
        lag = (r[None, :] - r[:, None]) if d == 0 else (r[:, None] - r[None, :])
        tp = jnp.where((lag >= 0)[:, :, None, None, None],
                       taps[:, d][jnp.clip(lag, 0, ll - 1)], 0.0)
        tp = tp.reshape(ll, ll, nblk, gpb, gc, gc).transpose(2, 0, 3, 5, 1, 4)
        c_in.append(tp.reshape(nblk, ll * LANES, ll * gc))
        steps = (ll - 1 - r) if d == 0 else r
        dr = jnp.stack([drv_re[:, d][steps], drv_im[:, d][steps]], axis=1)
        dr = dr.reshape(ll, 2, nblk, gpb, p, gc).transpose(2, 0, 3, 5, 1, 4)
        c_drv.append(dr.reshape(nblk, ll * LANES, 2 * p))
        steps = (r + 1) if d == 0 else (ll - r)
        rd = jnp.stack([rd_re[:, d][steps], rd_im[:, d][steps]], axis=1)
        rd = rd.reshape(ll, 2, nblk, gpb, gc, p).transpose(2, 1, 3, 5, 0, 4)
        c_rd.append(rd.reshape(nblk, 2 * gpb * p, ll * gc))

    def expand(compact, row_unit, col_unit):
        compact = jnp.stack(compact).astype(BF16)
        n_r, n_c = compact.shape[2], compact.shape[3] * gpb
        col = jnp.arange(n_c)
        src = (col // (col_unit * gpb)) * col_unit + col % col_unit
        spread = (jnp.arange(n_c // gpb)[:, None] == src[None, :]).astype(BF16)
        full = jnp.einsum('dbrk,kc->dbrc', compact, spread, preferred_element_type=F32)
        same = ((jnp.arange(n_r) // row_unit) % gpb)[:, None] == ((col // col_unit) % gpb)[None, :]
        return jnp.where(same, full, 0.0).astype(BF16)

    a8 = jnp.stack([pw_re[ll], pw_im[ll]], axis=1).reshape(2, 2, nblk, gpb * p).transpose(0, 2, 1, 3)
    return expand(c_in, gc, gc), expand(c_drv, gc, p), expand(c_rd, p, gc), a8


def _s5_kernel(uc_ref, ul_ref, min_ref, mdrv_ref, mrd_ref, a8_ref, y_ref, v_ref, st_ref, *, n_ctx_cols):
    bsz, n_oct, ll, n_col, lanes = ul_ref.shape
    n_rows = bsz * n_oct * n_col
    half = st_ref.shape[-1] // 2
    d = pl.program_id(0)
    k = pl.program_id(2)

    @pl.when(k == 0)
    def _():
        st_ref[...] = jnp.zeros_like(st_ref)

    def chunks(u_ref):
        return jnp.concatenate([u_ref[:, :, r].reshape(n_rows, lanes) for r in range(ll)], axis=-1)

    x = jnp.where(k == 0, chunks(uc_ref), chunks(ul_ref)).astype(BF16)
    inj_all = jnp.dot(x, mdrv_ref[0, 0], preferred_element_type=F32)
    n_pl = v_ref.shape[0]
    for c in range(n_pl):
        v_ref[c] = inj_all[:, c * lanes:(c + 1) * lanes]

    a_re = jnp.broadcast_to(a8_ref[0, 0, 0:1], (bsz, half))
    a_im = jnp.broadcast_to(a8_ref[0, 0, 1:2], (bsz, half))
    def scan(n_steps):
        def step(i, carry):
            s_re, s_im = carry
            i = jnp.where(d == 0, i, n_steps - 1 - i)
            row = (i % n_oct) * n_col + i // n_oct
            rows = pl.ds(row, bsz, stride=n_oct * n_col)
            inj = jnp.concatenate([v_ref[c, rows, :] for c in range(n_pl)], axis=-1)
            for c in range(n_pl // 2):
                v_ref[c, rows, :] = s_re[:, c * lanes:(c + 1) * lanes]
                v_ref[n_pl // 2 + c, rows, :] = s_im[:, c * lanes:(c + 1) * lanes]
            return (a_re * s_re - a_im * s_im + inj[:, :half],
                    a_re * s_im + a_im * s_re + inj[:, half:])

        s_re, s_im = lax.fori_loop(0, n_steps, step, (st_ref[:, :half], st_ref[:, half:]), unroll=2)
        st_ref[:, :half] = s_re
        st_ref[:, half:] = s_im

    @pl.when(k == 0)
    def _():
        scan(n_ctx_cols * n_oct)

    @pl.when(k > 0)
    def _():
        scan(n_col * n_oct)

    s_start = jnp.concatenate([v_ref[c] for c in range(n_pl)], axis=-1).astype(BF16)
    y = (jnp.dot(x, min_ref[0, 0], preferred_element_type=F32)
         + jnp.dot(s_start, mrd_ref[0, 0], preferred_element_type=F32))
    for r in range(ll):
        y_ref[0, :, :, r] = y[:, r * lanes:(r + 1) * lanes].reshape(bsz, n_oct, n_col, lanes)


def _s5_scan(a_lat, a_ctx, mats):
    bsz, t, d = a_lat.shape
    n_ctx = a_ctx.shape[1]
    m_in, m_drv, m_rd, a8 = mats
    n_rows = t // GRID_W
    n_oct = n_rows // S5_CHUNK
    n_ctx_cols = n_ctx // n_rows
    u_lat = a_lat.reshape(bsz, n_oct, S5_CHUNK, GRID_W, d)
    u_ctx = a_ctx.reshape(bsz, n_ctx_cols, n_rows, d).transpose(0, 2, 1, 3)
    u_ctx = jnp.pad(u_ctx, ((0, 0), (0, 0), (0, S5_COLS - n_ctx_cols), (0, 0)))
    u_ctx = u_ctx.reshape(bsz, n_oct, S5_CHUNK, S5_COLS, d)
    n_lat = GRID_W // S5_COLS
    nblk = d // LANES

    def lat_tile(dd, k):
        kk = jnp.maximum(k, 1) - 1
        return jnp.where(dd == 0, kk, n_lat - 1 - kk)

    blk = (bsz, n_oct, S5_CHUNK, S5_COLS, LANES)
    wspec = lambda shape: pl.BlockSpec((1, 1) + shape, lambda dd, j, k: (dd, j, 0, 0))
    kl = S5_CHUNK * LANES
    ns = m_drv.shape[-1]
    y = pl.pallas_call(
        functools.partial(_s5_kernel, n_ctx_cols=n_ctx_cols),
        grid=(2, nblk, n_lat + 1),
        in_specs=[pl.BlockSpec(blk, lambda dd, j, k: (0, 0, 0, 0, j)),
                  pl.BlockSpec(blk, lambda dd, j, k: (0, 0, 0, lat_tile(dd, k), j)),
                  wspec((kl, kl)), wspec((kl, ns)), wspec((ns, kl)),
                  pl.BlockSpec((1, 1, 2, ns // 2), lambda dd, j, k: (dd, j, 0, 0))],
        out_specs=pl.BlockSpec((1,) + blk, lambda dd, j, k: (dd, 0, 0, 0, lat_tile(dd, k), j)),
        out_shape=jax.ShapeDtypeStruct((2,) + u_lat.shape, F32),
        scratch_shapes=[pltpu.VMEM((ns // LANES, bsz * n_oct * S5_COLS, LANES), F32),
                        pltpu.VMEM((bsz, ns), F32)],
        compiler_params=_params("parallel", "parallel", "arbitrary"),
        name="s5_scan",
    )(u_ctx, u_lat, m_in, m_drv, m_rd, a8)
    return y.reshape(2, bsz, n_rows, GRID_W, d)


def _s5_glu_kernel(y_ref, u_ref, h_ref, m_ref, dsk_ref, w_ref, bg_ref, g2_ref, rw_ref, rb_ref,
                   h1_ref, f_ref, tw_ref, meta_ref, cnt_ref, run_ref):
    d = h_ref.shape[-1]
    n = h_ref.shape[1] * h_ref.shape[2]
    y = (y_ref[0, 0] + y_ref[1, 0] + dsk_ref[...] * u_ref[0]).reshape(n, d)
    z = jnp.dot(jax.nn.gelu(y).astype(BF16), w_ref[...], preferred_element_type=F32) + bg_ref[...]
    y = z[:, :d] * _sigmoid(z[:, d:])
    first = (pl.program_id(0) == 0) & (pl.program_id(1) == 0)
    _mixer_tail(first, h_ref[0].reshape(n, d), y, m_ref[0], g2_ref[...], rw_ref[...], rb_ref[...],
                run_ref, h1_ref, f_ref, tw_ref.at[0], meta_ref, cnt_ref)


def _s5_glu(y, a_lat, h_lat, mod, d_skip, w_glu, b_glu, g2, rw, rb):
    bsz, t, d = h_lat.shape
    n_rows = t // GRID_W
    n_e = rw.shape[-1]
    n_tok = n_rows * S5_COLS
    nt = GRID_W // S5_COLS
    lat = pl.BlockSpec((1, n_rows, S5_COLS, d), lambda b, i: (b, 0, i, 0))
    tail_specs, tail_shapes = _tail_outputs(bsz * nt, n_tok, d, n_e, lambda b, i: b * nt + i)
    const = lambda shape: pl.BlockSpec(shape, lambda b, i: (0,) * len(shape))
    return pl.pallas_call(
        _s5_glu_kernel,
        grid=(bsz, nt),
        in_specs=[pl.BlockSpec((2, 1, n_rows, S5_COLS, d), lambda b, i: (0, b, 0, i, 0)),
                  lat, lat, pl.BlockSpec((1, 6, d), lambda b, i: (2 * b + 1, 0, 0)),
                  const((1, d)), const((d, 2 * d)), const((1, 2 * d)), const((1, d)),
                  const((d, n_e)), const((1, n_e))],
        out_specs=[lat] + tail_specs,
        out_shape=[jax.ShapeDtypeStruct((bsz, n_rows, GRID_W, d), F32)] + tail_shapes,
        scratch_shapes=[pltpu.VMEM((1, n_e), F32)],
        compiler_params=_params("arbitrary", "arbitrary"),
        name="s5_glu",
    )(y, a_lat.reshape(bsz, n_rows, GRID_W, d), h_lat.reshape(bsz, n_rows, GRID_W, d), mod,
      d_skip.reshape(1, d), w_glu.astype(BF16), b_glu.reshape(1, 2 * d),
      g2.reshape(1, d), rw, rb.reshape(1, n_e))


def _combine1_kernel(y_ref, tw_ref, h_ref, m_ref, g_ref, o_ref):
    d = h_ref.shape[-1]
    n = h_ref.shape[1] * h_ref.shape[2]
    h2 = h_ref[0].reshape(n, d) + m_ref[0][5:6] * _combine_experts(y_ref, tw_ref, n, d)
    o_ref[0] = _rms(h2, g_ref[...]).reshape(o_ref.shape[1:])


def _combine1(y_asg, top_w, h1, mod, final_g):
    bsz, n_rows, _, d = h1.shape
    nt = GRID_W // S5_COLS
    n_tok = n_rows * S5_COLS
    lat = pl.BlockSpec((1, n_rows, S5_COLS, d), lambda b, i: (b, 0, i, 0))
    return pl.pallas_call(
        _combine1_kernel,
        grid=(bsz, nt),
        in_specs=[pl.BlockSpec((n_tok * TOP_K * d // LANES, LANES), lambda b, i: (b * nt + i, 0)),
                  pl.BlockSpec((1, n_tok, TOP_K), lambda b, i: (b * nt + i, 0, 0)),
                  lat, pl.BlockSpec((1, 6, d), lambda b, i: (2 * b + 1, 0, 0)),
                  pl.BlockSpec((1, d), lambda b, i: (0, 0))],
        out_specs=lat,
        out_shape=jax.ShapeDtypeStruct(h1.shape, F32),
        compiler_params=_params("parallel", "parallel"),
        name="combine1",
    )(y_asg, top_w, h1, mod, final_g.reshape(1, d))


def kernel(x, c, ctx, c_ctx, mod_w, mod_b, norm1_g, norm2_g, hg_w_in, hg_lb_logits, hg_norm_g,
           hg_w_out, s5_a_re, s5_a_im, s5_log_dt, s5_b_re, s5_b_im, s5_c_re, s5_c_im, s5_d, s5_w_glu,
           s5_b_glu, router_w, router_b, moe_w1, moe_b1, moe_w2, moe_b2, final_g):
    bsz, seq, d = x.shape
    n_ctx = ctx.shape[1]
    assert mod_w.shape[0] == 2 and n_ctx % TOK_TILE == 0 and seq % TOK_TILE == 0
    n_ctx_tiles = n_ctx // TOK_TILE
    lb_all = jnp.cumsum(jax.nn.softmax(hg_lb_logits.astype(F32), axis=0), axis=0)
    mod = _modulation(c, c_ctx, mod_w, mod_b)

    qs, v, sg, kf, kb, lff, lfb = _hg_proj(ctx, x, mod[0], norm1_g[0], hg_w_in[0], lb_all[0], n_ctx_tiles)
    o_f, o_b = _gla(qs, v, kf, kb, lff, lfb, n_ctx // HG_CHUNK)
    h1, f, top_w, meta, counts = _hg_readout(o_f, o_b, sg, ctx, x, mod[0], hg_norm_g[0], hg_w_out[0],
                                             norm2_g[0], router_w[0], router_b[0], n_ctx_tiles)
    y_asg = _moe(f, meta, counts, 0, moe_w1, moe_b1, moe_w2, moe_b2)
    h_lat, a_lat, a_ctx = _combine0(y_asg, top_w, h1, mod[0], mod[1], norm1_g[1], n_ctx_tiles)

    mats = _s5_matrices(s5_a_re[0], s5_a_im[0], s5_log_dt[0], s5_b_re[0], s5_b_im[0],
                        s5_c_re[0], s5_c_im[0])
    y_s5 = _s5_scan(a_lat, a_ctx, mats)
    h1, f, top_w, meta, counts = _s5_glu(y_s5, a_lat, h_lat, mod[1], s5_d[0], s5_w_glu[0], s5_b_glu[0],
                                         norm2_g[1], router_w[1], router_b[1])
    y_asg = _moe(f, meta, counts, 1, moe_w1, moe_b1, moe_w2, moe_b2)
    out = _combine1(y_asg, top_w, h1, mod[1], final_g)
    return out.reshape(bsz, seq, d)
```

```python
import functools

import jax
import jax.numpy as jnp
from jax import lax
from jax.experimental import pallas as pl
from jax.experimental.pallas import tpu as pltpu

F32 = jnp.float32
BF16 = jnp.bfloat16
HIGHEST = lax.Precision.HIGHEST

RMS_EPS = 1e-6
GRID_W = 64
HEAD_DIM = 128
HG_CHUNK = 64
HG_BATCH = 2
S5_GROUP = 16
S5_STATE = 64
S5_EIG_MAX = -1e-4
S5_CHUNK = 8
S5_COLS = 8
LANES = 128
N_EXPERTS = 32
TOP_K = 4
SWIGLU_ALPHA = 1.702
SWIGLU_LIMIT = 7.0
TOK_TILE = 256
MOE_ROWS = 256
VMEM_LIMIT = 56 * 1024 * 1024


def _params(*sem):
    return pltpu.CompilerParams(dimension_semantics=sem, vmem_limit_bytes=VMEM_LIMIT)


def _rms(x, g):
    return x * lax.rsqrt(jnp.mean(x * x, axis=-1, keepdims=True) + RMS_EPS) * g


def _sigmoid(x):
    return 1.0 / (1.0 + jnp.exp(-x))


def _silu(x):
    return x * _sigmoid(x)


def _to_token_tiles(ref, x):
    n, d = x.shape
    for c in range(d // LANES):
        ref[pl.ds(c, n, stride=d // LANES), :] = x[:, c * LANES:(c + 1) * LANES]


def _from_token_tiles(ref, n, d):
    return jnp.concatenate([ref[pl.ds(c, n, stride=d // LANES), :] for c in range(d // LANES)], axis=-1)


def _mod_kernel(c_ref, w_ref, b_ref, o_ref):
    s = _silu(c_ref[...])
    o_ref[0] = jnp.dot(s, w_ref[0], precision=HIGHEST, preferred_element_type=F32) + b_ref[0]


def _modulation(c, c_ctx, mod_w, mod_b):
    bsz, d = c.shape
    depth = mod_w.shape[0]
    rows = jnp.concatenate([c, c_ctx[None], jnp.zeros((16 - bsz - 1, d), F32)], axis=0)
    bn = 6 * d // 4
    out = pl.pallas_call(
        _mod_kernel,
        grid=(depth, 4),
        in_specs=[pl.BlockSpec((16, d), lambda l, j: (0, 0)),
                  pl.BlockSpec((1, d, bn), lambda l, j: (l, 0, j)),
                  pl.BlockSpec((1, 1, bn), lambda l, j: (l, 0, j))],
        out_specs=pl.BlockSpec((1, 16, bn), lambda l, j: (l, 0, j)),
        out_shape=jax.ShapeDtypeStruct((depth, 16, 6 * d), F32),
        compiler_params=_params("parallel", "parallel"),
        name="modulation",
    )(rows, mod_w, mod_b.reshape(depth, 1, 6 * d))
    m_lat = out[:, :bsz].reshape(depth, bsz, 1, 6, d)
    m_ctx = jnp.broadcast_to(out[:, bsz].reshape(depth, 1, 1, 6, d), m_lat.shape)
    return jnp.concatenate([m_ctx, m_lat], axis=2).reshape(depth, 2 * bsz, 6, d)


def _residual_tile(hc_ref, hx_ref, n_ctx_tiles):
    return jnp.where(pl.program_id(1) < n_ctx_tiles, hc_ref[0], hx_ref[0])


def _residual_specs(d, n_ctx_tiles):
    return [pl.BlockSpec((1, TOK_TILE, d), lambda b, i: (b, jnp.minimum(i, n_ctx_tiles - 1), 0)),
            pl.BlockSpec((1, TOK_TILE, d), lambda b, i: (b, jnp.maximum(i - n_ctx_tiles, 0), 0))]


def _hg_proj_kernel(hc_ref, hx_ref, m_ref, g_ref, w_ref, lb_ref,
                    qs_ref, v_ref, sg_ref, kf_ref, kb_ref, lff_ref, lfb_ref, *, n_ctx_tiles):
    d = hx_ref.shape[-1]
    m = m_ref[0]
    a = _rms(_residual_tile(hc_ref, hx_ref, n_ctx_tiles), g_ref[...]) * (1.0 + m[1:2]) + m[0:1]
    p = jnp.dot(a.astype(BF16), w_ref[...], preferred_element_type=F32)
    qs_ref[0] = _silu(p[:, 0:d]).astype(BF16)
    v_ref[0] = p[:, d:2 * d].astype(BF16)
    sg_ref[0] = _silu(p[:, 4 * d:5 * d]).astype(BF16)
    for di, (k_ref, lf_ref) in enumerate(((kf_ref, lff_ref), (kb_ref, lfb_ref))):
        lb = lb_ref[di:di + 1]
        f = lb + (1.0 - lb) * _sigmoid(p[:, (2 + di) * d:(3 + di) * d])
        k_ref[0] = (1.0 - f).astype(BF16)
        lf_ref[0] = jnp.log(f)


def _mod_spec(d, n_ctx_tiles):
    return pl.BlockSpec((1, 6, d), lambda b, i: (2 * b + (i >= n_ctx_tiles).astype(jnp.int32), 0, 0))


def _hg_proj(ctx, x, mod, g1, w_in, lb, n_ctx_tiles):
    bsz, t, d = x.shape
    t += ctx.shape[1]
    tok = pl.BlockSpec((1, TOK_TILE, d), lambda b, i: (b, i, 0))
    bf = jax.ShapeDtypeStruct((bsz, t, d), BF16)
    ff = jax.ShapeDtypeStruct((bsz, t, d), F32)
    return pl.pallas_call(
        functools.partial(_hg_proj_kernel, n_ctx_tiles=n_ctx_tiles),
        grid=(bsz, t // TOK_TILE),
        in_specs=_residual_specs(d, n_ctx_tiles) + [_mod_spec(d, n_ctx_tiles),
                  pl.BlockSpec((1, d), lambda b, i: (0, 0)),
                  pl.BlockSpec((d, 5 * d), lambda b, i: (0, 0)),
                  pl.BlockSpec((2, d), lambda b, i: (0, 0))],
        out_specs=[tok] * 7,
        out_shape=[bf, bf, bf, bf, bf, ff, ff],
        compiler_params=_params("parallel", "parallel"),
        name="hg_proj",
    )(ctx, x, mod, g1.reshape(1, d), w_in.astype(BF16), lb)


def _split3(x):
    hi = x.astype(BF16)
    r = x - hi.astype(F32)
    mid = r.astype(BF16)
    lo = (r - mid.astype(F32)).astype(BF16)
    return hi, mid, lo


def _gla_kernel(qf_ref, vf_ref, kf_ref, lf_ref, qb_ref, vb_ref, kb_ref, lb_ref,
                of_ref, ob_ref, sf_ref, sb_ref):
    c = HG_CHUNK
    n_b = qf_ref.shape[0]
    n_heads = sf_ref.shape[0] // n_b

    @pl.when(pl.program_id(1) == 0)
    def _():
        sf_ref[...] = jnp.zeros_like(sf_ref)
        sb_ref[...] = jnp.zeros_like(sb_ref)

    row = lax.broadcasted_iota(jnp.int32, (c, c), 0)
    col = lax.broadcasted_iota(jnp.int32, (c, c), 1)
    dirs = ((qf_ref, vf_ref, kf_ref, lf_ref, of_ref, sf_ref, col <= row, c // 2 - 1, c - 1),
            (qb_ref, vb_ref, kb_ref, lb_ref, ob_ref, sb_ref, col >= row, c // 2, 0))
    for bb, (q_ref, v_ref, k_ref, l_ref, o_ref, s_ref, keep, r_ref, r_last) in (
            (bb, dr) for bb in range(n_b) for dr in dirs):
        tri = keep.astype(BF16)
        b = sum(jnp.dot(tri, part, preferred_element_type=F32) for part in _split3(l_ref[bb]))
        b_ref = b[r_ref:r_ref + 1]
        b_last = b[r_last:r_last + 1]
        a_in = q_ref[bb].astype(F32) * jnp.exp(b - b_ref)
        k_in = k_ref[bb].astype(F32) * jnp.exp(b_ref - b)
        q_st = (a_in * jnp.exp(b_ref)).astype(BF16)
        k_st = (k_in * jnp.exp(b_last - b_ref)).astype(BF16)
        a_in = a_in.astype(BF16)
        k_in = k_in.astype(BF16)
        decay = jnp.exp(b_last)
        v = v_ref[bb]
        for h in range(n_heads):
            sl = slice(h * HEAD_DIM, (h + 1) * HEAD_DIM)
            sc = lax.dot_general(a_in[:, sl], k_in[:, sl], (((1,), (1,)), ((), ())),
                                 preferred_element_type=F32)
            sc = jnp.where(keep, sc, 0.0).astype(BF16)
            o = jnp.dot(sc, v[:, sl], preferred_element_type=F32)
            st = s_ref[bb * n_heads + h]
            o = o + lax.dot_general(q_st[:, sl], st.astype(BF16), (((1,), (1,)), ((), ())),
                                    preferred_element_type=F32)
            s_ref[bb * n_heads + h] = st * decay[:, sl] + lax.dot_general(
                v[:, sl], k_st[:, sl], (((0,), (0,)), ((), ())), preferred_element_type=F32)
            o_ref[bb, :, sl] = o


def _gla(qs, v, kf, kb, lff, lfb, n_ctx_chunks):
    bsz, t, d = qs.shape
    n = t // HG_CHUNK
    n_heads = d // HEAD_DIM

    def rev(j):
        return jnp.where(j < n_ctx_chunks, n_ctx_chunks - 1 - j, n + n_ctx_chunks - 1 - j)

    fwd = pl.BlockSpec((HG_BATCH, HG_CHUNK, d), lambda b, j: (b, j, 0))
    bwd = pl.BlockSpec((HG_BATCH, HG_CHUNK, d), lambda b, j: (b, rev(j), 0))
    out = jax.ShapeDtypeStruct((bsz, t, d), F32)
    state = pltpu.VMEM((HG_BATCH * n_heads, HEAD_DIM, HEAD_DIM), F32)
    return pl.pallas_call(
        _gla_kernel,
        grid=(bsz // HG_BATCH, n),
        in_specs=[fwd, fwd, fwd, fwd, bwd, bwd, bwd, bwd],
        out_specs=[fwd, bwd],
        out_shape=[out, out],
        scratch_shapes=[state, state],
        compiler_params=_params("parallel", "arbitrary"),
        name="gla",
    )(qs, v, kf, lff, qs, v, kb, lfb)


def _mixer_tail(first, h, y, m, g2, rw, rb, run_ref, h1_ref, f_ref, tw_ref, meta_ref, cnt_ref):
    h1 = h + m[2:3] * y
    f = _rms(h1, g2) * (1.0 + m[4:5]) + m[3:4]
    h1_ref[...] = h1.reshape(h1_ref.shape)
    _to_token_tiles(f_ref, f)
    f_hi, f_lo, _ = _split3(f)
    w_hi, w_lo, _ = _split3(rw)
    logits = (jnp.dot(f_hi, w_hi, preferred_element_type=F32) + jnp.dot(f_hi, w_lo, preferred_element_type=F32)
              + jnp.dot(f_lo, w_hi, preferred_element_type=F32)) + rb
    n, n_e = logits.shape
    lane = lax.broadcasted_iota(jnp.int32, logits.shape, 1).astype(F32)
    vals, idxs, hots = [], [], []
    for _ in range(TOP_K):
        mx = jnp.max(logits, axis=-1, keepdims=True)
        ix = jnp.min(jnp.where(logits == mx, lane, float(n_e)), axis=-1, keepdims=True)
        hot = lane == ix
        vals.append(mx)
        idxs.append(ix)
        hots.append(hot)
        logits = jnp.where(hot, -jnp.inf, logits)
    es = [jnp.exp(x - vals[0]) for x in vals]
    tot = sum(es)
    for k in range(TOP_K):
        tw_ref[:, k:k + 1] = es[k] / tot

    @pl.when(first)
    def _():
        run_ref[...] = jnp.zeros_like(run_ref)

    picked = sum(hot.astype(F32) for hot in hots)
    r_i = lax.broadcasted_iota(jnp.int32, (n, n), 0)
    c_i = lax.broadcasted_iota(jnp.int32, (n, n), 1)
    earlier = jnp.dot((c_i < r_i).astype(BF16), picked.astype(BF16), preferred_element_type=F32)
    rank = earlier + run_ref[...]
    total = run_ref[...] + jnp.sum(picked, axis=0, keepdims=True)
    run_ref[...] = total
    cnt_ref[...] = total
    col = lax.broadcasted_iota(jnp.int32, (n, LANES), 1)
    z = jnp.zeros((n, LANES), F32)
    for k in range(TOP_K):
        pos = jnp.sum(jnp.where(hots[k], rank, 0.0), axis=-1, keepdims=True)
        z = jnp.where(col == k, idxs[k], z)
        z = jnp.where(col == TOP_K + k, pos, z)
    meta_ref[0] = z.T[0:2 * TOP_K].astype(jnp.int32)


def _tail_outputs(n_tiles, tile, d, n_e, index):
    rt = d // LANES
    specs = [pl.BlockSpec((tile * rt, LANES), lambda b, i: (index(b, i), 0)),
             pl.BlockSpec((1, tile, TOP_K), lambda b, i: (index(b, i), 0, 0)),
             pl.BlockSpec((1, 2 * TOP_K, tile), lambda b, i: (index(b, i), 0, 0)),
             pl.BlockSpec((1, n_e), lambda b, i: (0, 0))]
    shapes = [jax.ShapeDtypeStruct((n_tiles * tile * rt, LANES), F32),
              jax.ShapeDtypeStruct((n_tiles, tile, TOP_K), F32),
              jax.ShapeDtypeStruct((n_tiles, 2 * TOP_K, tile), jnp.int32),
              jax.ShapeDtypeStruct((1, n_e), F32)]
    return specs, shapes


def _hg_readout_kernel(of_ref, ob_ref, sg_ref, hc_ref, hx_ref, m_ref, ng_ref, w_ref, g2_ref, rw_ref, rb_ref,
                       h1_ref, f_ref, tw_ref, meta_ref, cnt_ref, run_ref, *, n_ctx_tiles):
    d = hx_ref.shape[-1]
    o = of_ref[0] + ob_ref[0]
    parts = []
    for h in range(d // HEAD_DIM):
        oh = o[:, h * HEAD_DIM:(h + 1) * HEAD_DIM]
        parts.append(oh * lax.rsqrt(jnp.mean(oh * oh, axis=-1, keepdims=True) + RMS_EPS))
    o = jnp.concatenate(parts, axis=-1) * ng_ref[...]
    y = jnp.dot((o * sg_ref[0].astype(F32)).astype(BF16), w_ref[...], preferred_element_type=F32)
    first = (pl.program_id(0) == 0) & (pl.program_id(1) == 0)
    _mixer_tail(first, _residual_tile(hc_ref, hx_ref, n_ctx_tiles), y, m_ref[0], g2_ref[...], rw_ref[...],
                rb_ref[...], run_ref, h1_ref, f_ref, tw_ref.at[0], meta_ref, cnt_ref)


def _hg_readout(o_f, o_b, sg, ctx, x, mod, norm_g, w_out, g2, rw, rb, n_ctx_tiles):
    bsz, t, d = o_f.shape
    n_e = rw.shape[-1]
    nt = t // TOK_TILE
    tok = pl.BlockSpec((1, TOK_TILE, d), lambda b, i: (b, i, 0))
    const = lambda shape: pl.BlockSpec(shape, lambda b, i: (0,) * len(shape))
    tail_specs, tail_shapes = _tail_outputs(bsz * nt, TOK_TILE, d, n_e, lambda b, i: b * nt + i)
    return pl.pallas_call(
        functools.partial(_hg_readout_kernel, n_ctx_tiles=n_ctx_tiles),
        grid=(bsz, nt),
        in_specs=[tok, tok, tok] + _residual_specs(d, n_ctx_tiles) + [_mod_spec(d, n_ctx_tiles),
                  const((1, d)), const((d, d)), const((1, d)), const((d, n_e)), const((1, n_e))],
        out_specs=[tok] + tail_specs,
        out_shape=[jax.ShapeDtypeStruct((bsz, t, d), F32)] + tail_shapes,
        scratch_shapes=[pltpu.VMEM((1, n_e), F32)],
        compiler_params=_params("arbitrary", "arbitrary"),
        name="hg_readout",
    )(o_f, o_b, sg, ctx, x, mod, norm_g.reshape(1, d), w_out.astype(BF16), g2.reshape(1, d),
      rw, rb.reshape(1, n_e))


def _moe_plan(counts, n_asg, rows):
    counts = counts.reshape(-1).astype(jnp.int32)
    padded = (counts + rows - 1) // rows * rows
    pad_end = jnp.cumsum(padded)
    n_blocks = -(-n_asg // rows) + N_EXPERTS
    block_pos = jnp.arange(n_blocks, dtype=jnp.int32) * rows
    block_e = jnp.minimum(jnp.sum((block_pos[:, None] >= pad_end[None, :]).astype(jnp.int32), axis=1),
                          N_EXPERTS - 1)
    n_used = (pad_end[-1:] // rows).astype(jnp.int32)
    return pad_end - padded, block_e, n_used, n_blocks


def _invert_kernel(dest_ref, init_hbm, inv_ref, sem):
    i = pl.program_id(0)
    tile = dest_ref.shape[2]

    @pl.when(i == 0)
    def _():
        cp = pltpu.make_async_copy(init_hbm, inv_ref, sem)
        cp.start()
        cp.wait()

    base = i * (tile * TOP_K)
    for r in range(tile):
        for k in range(TOP_K):
            inv_ref[dest_ref[0, k, r]] = base + (r * TOP_K + k)


def _invert(meta, slot_start, n_slots):
    n_tiles, _, tile = meta.shape
    hot = meta[:, :TOP_K, :, None] == jnp.arange(N_EXPERTS, dtype=jnp.int32)
    dest = meta[:, TOP_K:] + jnp.sum(jnp.where(hot, slot_start, 0), axis=-1)
    return pl.pallas_call(
        _invert_kernel,
        grid=(n_tiles,),
        in_specs=[pl.BlockSpec((1, TOP_K, tile), lambda i: (i, 0, 0), memory_space=pltpu.SMEM),
                  pl.BlockSpec(memory_space=pl.ANY)],
        out_specs=pl.BlockSpec(memory_space=pltpu.SMEM),
        out_shape=jax.ShapeDtypeStruct((n_slots,), jnp.int32),
        scratch_shapes=[pltpu.SemaphoreType.DMA(())],
        compiler_params=_params("arbitrary"),
        name="moe_invert",
    )(dest.astype(jnp.int32), jnp.full((n_slots,), -1, jnp.int32))


def _ffn_kernel(be_ref, nu_ref, tok_ref, tok_next_ref, dst_prev_ref, dst_ref, x_hbm,
                w1_ref, b1_ref, w2_ref, b2_ref, y_hbm, buf_p, buf_q, zbuf, w1c, w2c, gsem, ssem, zsem):
    xbuf = (buf_p.at[0], buf_q.at[0])
    ybuf = (buf_q.at[1], buf_p.at[1])
    b = pl.program_id(0)
    nu = nu_ref[0]
    d, f2 = w1c.shape
    rt = d // LANES
    rows = buf_p.shape[1] // rt

    def tile_of(ref, idx):
        return ref.at[pl.ds(pl.multiple_of(idx * rt, rt), rt)]

    def gather(idx_ref, s):
        for r in range(rows):
            pltpu.make_async_copy(tile_of(x_hbm, idx_ref[0, 0, r]), xbuf[s].at[pl.ds(r * rt, rt)],
                                  gsem.at[s]).start(priority=1)

    def scatter(idx_ref, s):
        for r in range(rows):
            pltpu.make_async_copy(ybuf[s].at[pl.ds(r * rt, rt)], tile_of(y_hbm, idx_ref[0, 0, r]),
                                  ssem.at[s]).start(priority=0)

    def wait_gather(s):
        pltpu.make_async_copy(x_hbm.at[pl.ds(0, rows * rt)], xbuf[s], gsem.at[s]).wait()

    def wait_scatter(s):
        pltpu.make_async_copy(ybuf[s], y_hbm.at[pl.ds(0, rows * rt)], ssem.at[s]).wait()

    @pl.when(b == 0)
    def _():
        zbuf[...] = jnp.zeros_like(zbuf)
        buf_p[1] = jnp.zeros(buf_p.shape[1:], F32)
        gather(tok_ref, 0)

    @pl.when(b >= nu)
    def _():
        dst = y_hbm.at[pl.ds(pl.multiple_of(b * (rows * rt), rows * rt), rows * rt)]
        cp = pltpu.make_async_copy(zbuf, dst, zsem)
        cp.start()
        cp.wait()

    def used_block(s):
        @pl.when((b == 0) | (be_ref[b] != be_ref[jnp.maximum(b - 1, 0)]))
        def _():
            w1c[...] = w1_ref[0, 0].astype(BF16)
            w2c[...] = w2_ref[0, 0].astype(BF16)

        wait_gather(s)

        @pl.when(b >= 1)
        def _():
            wait_scatter(s)

        x_head = xbuf[s][0:rt]
        x = _from_token_tiles(xbuf[s], rows, d).astype(BF16)
        gather(tok_next_ref, 1 - s)
        scatter(dst_prev_ref, 1 - s)
        z = jnp.dot(x, w1c[...], preferred_element_type=F32) + b1_ref[0, 0]
        z_glu = jnp.minimum(z[:, :f2 // 2], SWIGLU_LIMIT)
        z_lin = jnp.clip(z[:, f2 // 2:], -SWIGLU_LIMIT, SWIGLU_LIMIT)
        act = z_glu * _sigmoid(SWIGLU_ALPHA * z_glu) * (z_lin + 1.0)
        y = jnp.dot(act.astype(BF16), w2c[...], preferred_element_type=F32) + b2_ref[0, 0]
        _to_token_tiles(ybuf[s], y)
        xbuf[s][0:rt] = x_head

        @pl.when(b == nu - 1)
        def _():
            scatter(dst_ref, s)
            wait_gather(1 - s)
            wait_scatter(1 - s)
            wait_scatter(s)

    for s in range(2):
        pl.when((b < nu) & (b % 2 == s))(functools.partial(used_block, s))


def _moe(f_tiles, meta, counts, layer, w1, b1, w2, b2):
    n_tiles, _, tile = meta.shape
    n_asg = n_tiles * tile * TOP_K
    _, n_e, d, f2 = w1.shape
    rt = d // LANES
    slot_start, block_e, n_used, n_blocks = _moe_plan(counts, n_asg, MOE_ROWS)
    n_slots = n_blocks * MOE_ROWS
    inv = _invert(meta, slot_start, n_slots)
    is_pad = inv < 0
    slot_tok = jnp.where(is_pad, 0, inv // TOP_K).reshape(n_blocks, 1, MOE_ROWS)
    spill = n_asg + jnp.cumsum(is_pad.astype(jnp.int32)) - 1
    slot_dst = jnp.where(is_pad, spill, inv)
    slot_dst = jnp.concatenate([slot_dst, jnp.arange(n_slots - MOE_ROWS, n_slots, dtype=jnp.int32)])
    slot_dst = slot_dst.reshape(n_blocks + 1, 1, MOE_ROWS)

    live = lambda b, nu: jnp.minimum(b, nu[0] - 1)
    smem = lambda imap: pl.BlockSpec((1, 1, MOE_ROWS), imap, memory_space=pltpu.SMEM)
    per_e = lambda shape: pl.BlockSpec((1, 1) + shape, lambda b, be, nu: (layer, be[live(b, nu)], 0, 0))
    buf = pltpu.VMEM((MOE_ROWS * rt, LANES), F32)
    bufs = pltpu.VMEM((2, MOE_ROWS * rt, LANES), F32)
    grid_spec = pltpu.PrefetchScalarGridSpec(
        num_scalar_prefetch=2,
        grid=(n_blocks,),
        in_specs=[smem(lambda b, be, nu: (live(b, nu), 0, 0)),
                  smem(lambda b, be, nu: (live(b + 1, nu), 0, 0)),
                  smem(lambda b, be, nu: (jnp.where(b == 0, n_blocks, live(b - 1, nu)), 0, 0)),
                  smem(lambda b, be, nu: (live(b, nu), 0, 0)),
                  pl.BlockSpec(memory_space=pl.ANY),
                  per_e((d, f2)), per_e((1, f2)), per_e((f2 // 2, d)), per_e((1, d))],
        out_specs=pl.BlockSpec(memory_space=pl.ANY),
        scratch_shapes=[bufs, bufs, buf,
                        pltpu.VMEM((d, f2), BF16), pltpu.VMEM((f2 // 2, d), BF16),
                        pltpu.SemaphoreType.DMA((2,)), pltpu.SemaphoreType.DMA((2,)),
                        pltpu.SemaphoreType.DMA(())])
    return pl.pallas_call(
        _ffn_kernel,
        grid_spec=grid_spec,
        out_shape=jax.ShapeDtypeStruct((n_slots * rt, LANES), F32),
        compiler_params=_params("arbitrary"),
        name="moe_ffn",
    )(block_e, n_used, slot_tok, slot_tok, slot_dst, slot_dst, f_tiles, w1, b1.reshape(b1.shape[0], n_e, 1, f2),
      w2, b2.reshape(b2.shape[0], n_e, 1, d))


def _combine_experts(y_ref, tw_ref, tile, d):
    rt = d // LANES
    tw = tw_ref[0]
    out = 0.0
    for k in range(TOP_K):
        yk = jnp.concatenate([y_ref[pl.ds(k * rt + c, tile, stride=TOP_K * rt), :] for c in range(rt)],
                             axis=-1)
        out = out + tw[:, k:k + 1] * yk
    return out


def _combine0_kernel(y_ref, tw_ref, h_ref, m0_ref, m1_ref, g_ref, hl_ref, al_ref, ac_ref, *, n_ctx_tiles):
    tile, d = h_ref.shape[1:]
    h2 = h_ref[0] + m0_ref[0][5:6] * _combine_experts(y_ref, tw_ref, tile, d)
    m1 = m1_ref[0]
    a = _rms(h2, g_ref[...]) * (1.0 + m1[1:2]) + m1[0:1]
    is_ctx = pl.program_id(1) < n_ctx_tiles

    @pl.when(is_ctx)
    def _():
        ac_ref[0] = a

    @pl.when(jnp.logical_not(is_ctx))
    def _():
        hl_ref[0] = h2
        al_ref[0] = a


def _combine0(y_asg, top_w, h1, mod0, mod1, g1_next, n_ctx_tiles):
    bsz, t, d = h1.shape
    nt = t // TOK_TILE
    n_ctx = n_ctx_tiles * TOK_TILE
    tok = pl.BlockSpec((1, TOK_TILE, d), lambda b, i: (b, i, 0))
    lat = pl.BlockSpec((1, TOK_TILE, d), lambda b, i: (b, jnp.maximum(i - n_ctx_tiles, 0), 0))
    ctx = pl.BlockSpec((1, TOK_TILE, d), lambda b, i: (b, jnp.minimum(i, n_ctx_tiles - 1), 0))
    return pl.pallas_call(
        functools.partial(_combine0_kernel, n_ctx_tiles=n_ctx_tiles),
        grid=(bsz, nt),
        in_specs=[pl.BlockSpec((TOK_TILE * TOP_K * d // LANES, LANES), lambda b, i: (b * nt + i, 0)),
                  pl.BlockSpec((1, TOK_TILE, TOP_K), lambda b, i: (b * nt + i, 0, 0)),
                  tok, _mod_spec(d, n_ctx_tiles), _mod_spec(d, n_ctx_tiles),
                  pl.BlockSpec((1, d), lambda b, i: (0, 0))],
        out_specs=[lat, lat, ctx],
        out_shape=[jax.ShapeDtypeStruct((bsz, t - n_ctx, d), F32),
                   jax.ShapeDtypeStruct((bsz, t - n_ctx, d), F32),
                   jax.ShapeDtypeStruct((bsz, n_ctx, d), F32)],
        compiler_params=_params("parallel", "parallel"),
        name="combine0",
    )(y_asg, top_w, h1, mod0, mod1, g1_next.reshape(1, d))


def _s5_matrices(a_re, a_im, log_dt, b_re, b_im, c_re, c_im):
    ng, p = a_re.shape[1:]
    gc = b_re.shape[-1]
    ll = S5_CHUNK
    gpb = LANES // gc
    nblk = ng // gpb
    lam_re = jnp.minimum(a_re, S5_EIG_MAX)
    lam_im = a_im
    dt = jnp.exp(log_dt)[..., None]
    j = jnp.arange(ll + 1, dtype=F32).reshape(-1, 1, 1, 1)
    mag = jnp.exp(j * (lam_re * dt))
    pw_re = mag * jnp.cos(j * (lam_im * dt))
    pw_im = mag * jnp.sin(j * (lam_im * dt))
    ab_re, ab_im = pw_re[1], pw_im[1]
    den = lam_re * lam_re + lam_im * lam_im
    coef_re = ((ab_re - 1.0) * lam_re + ab_im * lam_im) / den
    coef_im = (ab_im * lam_re - (ab_re - 1.0) * lam_im) / den
    bb_re = coef_re[..., None] * b_re - coef_im[..., None] * b_im
    bb_im = coef_re[..., None] * b_im + coef_im[..., None] * b_re
    drv_re = pw_re[..., None] * bb_re - pw_im[..., None] * bb_im
    drv_im = pw_re[..., None] * bb_im + pw_im[..., None] * bb_re
    rd_re = c_re * pw_re[:, :, :, None, :] - c_im * pw_im[:, :, :, None, :]
    rd_im = -(c_re * pw_im[:, :, :, None, :] + c_im * pw_re[:, :, :, None, :])
    taps = (jnp.einsum('dgop,jdgpi->jdgoi', c_re, drv_re[:ll], precision=HIGHEST)
            - jnp.einsum('dgop,jdgpi->jdgoi', c_im, drv_im[:ll], precision=HIGHEST))
    r = jnp.arange(ll)
    c_in, c_drv, c_rd = [], [], []
    for d in range(2):
        lag = (r[None, :] - r[:, None]) if d == 0 else (r[:, None] - r[None, :])
        tp = jnp.where((lag >= 0)[:, :, None, None, None],
                       taps[:, d][jnp.clip(lag, 0, ll - 1)], 0.0)
        tp = tp.reshape(ll, ll, nblk, gpb, gc, gc).transpose(2, 0, 3, 5, 1, 4)
        c_in.append(tp.reshape(nblk, ll * LANES, ll * gc))
        steps = (ll - 1 - r) if d == 0 else r
        dr = jnp.stack([drv_re[:, d][steps], drv_im[:, d][steps]], axis=1)
        dr = dr.reshape(ll, 2, nblk, gpb, p, gc).transpose(2, 0, 3, 5, 1, 4)
        c_drv.append(dr.reshape(nblk, ll * LANES, 2 * p))
        steps = (r + 1) if d == 0 else (ll - r)
        rd = jnp.stack([rd_re[:, d][steps], rd_im[:, d][steps]], axis=1)
        rd = rd.reshape(ll, 2, nblk, gpb, gc, p).transpose(2, 1, 3, 5, 0, 4)
        c_rd.append(rd.reshape(nblk, 2 * gpb * p, ll * gc))

    def expand(compact, row_unit, col_unit):
        compact = jnp.stack(compact).astype(BF16)
        n_r, n_c = compact.shape[2], compact.shape[3] * gpb
        col = jnp.arange(n_c)
        src = (col // (col_unit * gpb)) * col_unit + col % col_unit
        spread = (jnp.arange(n_c // gpb)[:, None] == src[None, :]).astype(BF16)
        full = jnp.einsum('dbrk,kc->dbrc', compact, spread, preferred_element_type=F32)
        same = ((jnp.arange(n_r) // row_unit) % gpb)[:, None] == ((col // col_unit) % gpb)[None, :]
        return jnp.where(same, full, 0.0).astype(BF16)

    a8 = jnp.stack([pw_re[ll], pw_im[ll]], axis=1).reshape(2, 2, nblk, gpb * p).transpose(0, 2, 1, 3)
    return expand(c_in, gc, gc), expand(c_drv, gc, p), expand(c_rd, p, gc), a8


def _s5_kernel(uc_ref, ul_ref, min_ref, mdrv_ref, mrd_ref, a8_ref, y_ref, v_ref, st_ref, *, n_ctx_cols):
    bsz, n_oct, ll, n_col, lanes = ul_ref.shape
    n_rows = bsz * n_oct * n_col
    half = st_ref.shape[-1] // 2
    d = pl.program_id(0)
    k = pl.program_id(2)

    @pl.when(k == 0)
    def _():
        st_ref[...] = jnp.zeros_like(st_ref)

    def chunks(u_ref):
        return jnp.concatenate([u_ref[:, :, r].reshape(n_rows, lanes) for r in range(ll)], axis=-1)

    x = jnp.where(k == 0, chunks(uc_ref), chunks(ul_ref)).astype(BF16)
    inj_all = jnp.dot(x, mdrv_ref[0, 0], preferred_element_type=F32)
    n_pl = v_ref.shape[0]
    for c in range(n_pl):
        v_ref[c] = inj_all[:, c * lanes:(c + 1) * lanes]

    a_re = jnp.broadcast_to(a8_ref[0, 0, 0:1], (bsz, half))
    a_im = jnp.broadcast_to(a8_ref[0, 0, 1:2], (bsz, half))
    def scan(n_steps):
        def step(i, carry):
            s_re, s_im = carry
            i = jnp.where(d == 0, i, n_steps - 1 - i)
            row = (i % n_oct) * n_col + i // n_oct
            rows = pl.ds(row, bsz, stride=n_oct * n_col)
            inj = jnp.concatenate([v_ref[c, rows, :] for c in range(n_pl)], axis=-1)
            for c in range(n_pl // 2):
                v_ref[c, rows, :] = s_re[:, c * lanes:(c + 1) * lanes]
                v_ref[n_pl // 2 + c, rows, :] = s_im[:, c * lanes:(c + 1) * lanes]
            return (a_re * s_re - a_im * s_im + inj[:, :half],
                    a_re * s_im + a_im * s_re + inj[:, half:])

        s_re, s_im = lax.fori_loop(0, n_steps, step, (st_ref[:, :half], st_ref[:, half:]), unroll=2)
        st_ref[:, :half] = s_re
        st_ref[:, half:] = s_im

    @pl.when(k == 0)
    def _():
        scan(n_ctx_cols * n_oct)

    @pl.when(k > 0)
    def _():
        scan(n_col * n_oct)

    s_start = jnp.concatenate([v_ref[c] for c in range(n_pl)], axis=-1).astype(BF16)
    y = (jnp.dot(x, min_ref[0, 0], preferred_element_type=F32)
         + jnp.dot(s_start, mrd_ref[0, 0], preferred_element_type=F32))
    for r in range(ll):
        y_ref[0, :, :, r] = y[:, r * lanes:(r + 1) * lanes].reshape(bsz, n_oct, n_col, lanes)


def _s5_scan(a_lat, a_ctx, mats):
    bsz, t, d = a_lat.shape
    n_ctx = a_ctx.shape[1]
    m_in, m_drv, m_rd, a8 = mats
    n_rows = t // GRID_W
    n_oct = n_rows // S5_CHUNK
    n_ctx_cols = n_ctx // n_rows
    u_lat = a_lat.reshape(bsz, n_oct, S5_CHUNK, GRID_W, d)
    u_ctx = a_ctx.reshape(bsz, n_ctx_cols, n_rows, d).transpose(0, 2, 1, 3)
    u_ctx = jnp.pad(u_ctx, ((0, 0), (0, 0), (0, S5_COLS - n_ctx_cols), (0, 0)))
    u_ctx = u_ctx.reshape(bsz, n_oct, S5_CHUNK, S5_COLS, d)
    n_lat = GRID_W // S5_COLS
    nblk = d // LANES

    def lat_tile(dd, k):
        kk = jnp.maximum(k, 1) - 1
        return jnp.where(dd == 0, kk, n_lat - 1 - kk)

    blk = (bsz, n_oct, S5_CHUNK, S5_COLS, LANES)
    wspec = lambda shape: pl.BlockSpec((1, 1) + shape, lambda dd, j, k: (dd, j, 0, 0))
    kl = S5_CHUNK * LANES
    ns = m_drv.shape[-1]
    y = pl.pallas_call(
        functools.partial(_s5_kernel, n_ctx_cols=n_ctx_cols),
        grid=(2, nblk, n_lat + 1),
        in_specs=[pl.BlockSpec(blk, lambda dd, j, k: (0, 0, 0, 0, j)),
                  pl.BlockSpec(blk, lambda dd, j, k: (0, 0, 0, lat_tile(dd, k), j)),
                  wspec((kl, kl)), wspec((kl, ns)), wspec((ns, kl)),
                  pl.BlockSpec((1, 1, 2, ns // 2), lambda dd, j, k: (dd, j, 0, 0))],
        out_specs=pl.BlockSpec((1,) + blk, lambda dd, j, k: (dd, 0, 0, 0, lat_tile(dd, k), j)),
        out_shape=jax.ShapeDtypeStruct((2,) + u_lat.shape, F32),
        scratch_shapes=[pltpu.VMEM((ns // LANES, bsz * n_oct * S5_COLS, LANES), F32),
                        pltpu.VMEM((bsz, ns), F32)],
        compiler_params=_params("parallel", "parallel", "arbitrary"),
        name="s5_scan",
    )(u_ctx, u_lat, m_in, m_drv, m_rd, a8)
    return y.reshape(2, bsz, n_rows, GRID_W, d)


def _s5_glu_kernel(y_ref, u_ref, h_ref, m_ref, dsk_ref, w_ref, bg_ref, g2_ref, rw_ref, rb_ref,
                   h1_ref, f_ref, tw_ref, meta_ref, cnt_ref, run_ref):
    d = h_ref.shape[-1]
    n = h_ref.shape[1] * h_ref.shape[2]
    y = (y_ref[0, 0] + y_ref[1, 0] + dsk_ref[...] * u_ref[0]).reshape(n, d)
    z = jnp.dot(jax.nn.gelu(y).astype(BF16), w_ref[...], preferred_element_type=F32) + bg_ref[...]
    y = z[:, :d] * _sigmoid(z[:, d:])
    first = (pl.program_id(0) == 0) & (pl.program_id(1) == 0)
    _mixer_tail(first, h_ref[0].reshape(n, d), y, m_ref[0], g2_ref[...], rw_ref[...], rb_ref[...],
                run_ref, h1_ref, f_ref, tw_ref.at[0], meta_ref, cnt_ref)


def _s5_glu(y, a_lat, h_lat, mod, d_skip, w_glu, b_glu, g2, rw, rb):
    bsz, t, d = h_lat.shape
    n_rows = t // GRID_W
    n_e = rw.shape[-1]
    n_tok = n_rows * S5_COLS
    nt = GRID_W // S5_COLS
    lat = pl.BlockSpec((1, n_rows, S5_COLS, d), lambda b, i: (b, 0, i, 0))
    tail_specs, tail_shapes = _tail_outputs(bsz * nt, n_tok, d, n_e, lambda b, i: b * nt + i)
    const = lambda shape: pl.BlockSpec(shape, lambda b, i: (0,) * len(shape))
    return pl.pallas_call(
        _s5_glu_kernel,
        grid=(bsz, nt),
        in_specs=[pl.BlockSpec((2, 1, n_rows, S5_COLS, d), lambda b, i: (0, b, 0, i, 0)),
                  lat, lat, pl.BlockSpec((1, 6, d), lambda b, i: (2 * b + 1, 0, 0)),
                  const((1, d)), const((d, 2 * d)), const((1, 2 * d)), const((1, d)),
                  const((d, n_e)), const((1, n_e))],
        out_specs=[lat] + tail_specs,
        out_shape=[jax.ShapeDtypeStruct((bsz, n_rows, GRID_W, d), F32)] + tail_shapes,
        scratch_shapes=[pltpu.VMEM((1, n_e), F32)],
        compiler_params=_params("arbitrary", "arbitrary"),
        name="s5_glu",
    )(y, a_lat.reshape(bsz, n_rows, GRID_W, d), h_lat.reshape(bsz, n_rows, GRID_W, d), mod,
      d_skip.reshape(1, d), w_glu.astype(BF16), b_glu.reshape(1, 2 * d),
      g2.reshape(1, d), rw, rb.reshape(1, n_e))


def _combine1_kernel(y_ref, tw_ref, h_ref, m_ref, g_ref, o_ref):
    d = h_ref.shape[-1]
    n = h_ref.shape[1] * h_ref.shape[2]
    h2 = h_ref[0].reshape(n, d) + m_ref[0][5:6] * _combine_experts(y_ref, tw_ref, n, d)
    o_ref[0] = _rms(h2, g_ref[...]).reshape(o_ref.shape[1:])


def _combine1(y_asg, top_w, h1, mod, final_g):
    bsz, n_rows, _, d = h1.shape
    nt = GRID_W // S5_COLS
    n_tok = n_rows * S5_COLS
    lat = pl.BlockSpec((1, n_rows, S5_COLS, d), lambda b, i: (b, 0, i, 0))
    return pl.pallas_call(
        _combine1_kernel,
        grid=(bsz, nt),
        in_specs=[pl.BlockSpec((n_tok * TOP_K * d // LANES, LANES), lambda b, i: (b * nt + i, 0)),
                  pl.BlockSpec((1, n_tok, TOP_K), lambda b, i: (b * nt + i, 0, 0)),
                  lat, pl.BlockSpec((1, 6, d), lambda b, i: (2 * b + 1, 0, 0)),
                  pl.BlockSpec((1, d), lambda b, i: (0, 0))],
        out_specs=lat,
        out_shape=jax.ShapeDtypeStruct(h1.shape, F32),
        compiler_params=_params("parallel", "parallel"),
        name="combine1",
    )(y_asg, top_w, h1, mod, final_g.reshape(1, d))


def kernel(x, c, ctx, c_ctx, mod_w, mod_b, norm1_g, norm2_g, hg_w_in, hg_lb_logits, hg_norm_g,
           hg_w_out, s5_a_re, s5_a_im, s5_log_dt, s5_b_re, s5_b_im, s5_c_re, s5_c_im, s5_d, s5_w_glu,
           s5_b_glu, router_w, router_b, moe_w1, moe_b1, moe_w2, moe_b2, final_g):
    bsz, seq, d = x.shape
    n_ctx = ctx.shape[1]
    assert mod_w.shape[0] == 2 and n_ctx % TOK_TILE == 0 and seq % TOK_TILE == 0
    n_ctx_tiles = n_ctx // TOK_TILE
    lb_all = jnp.cumsum(jax.nn.softmax(hg_lb_logits.astype(F32), axis=0), axis=0)
    mod = _modulation(c, c_ctx, mod_w, mod_b)

    qs, v, sg, kf, kb, lff, lfb = _hg_proj(ctx, x, mod[0], norm1_g[0], hg_w_in[0], lb_all[0], n_ctx_tiles)
    o_f, o_b = _gla(qs, v, kf, kb, lff, lfb, n_ctx // HG_CHUNK)
    h1, f, top_w, meta, counts = _hg_readout(o_f, o_b, sg, ctx, x, mod[0], hg_norm_g[0], hg_w_out[0],
                                             norm2_g[0], router_w[0], router_b[0], n_ctx_tiles)
    y_asg = _moe(f, meta, counts, 0, moe_w1, moe_b1, moe_w2, moe_b2)
    h_lat, a_lat, a_ctx = _combine0(y_asg, top_w, h1, mod[0], mod[1], norm1_g[1], n_ctx_tiles)

    mats = _s5_matrices(s5_a_re[0], s5_a_im[0], s5_log_dt[0], s5_b_re[0], s5_b_im[0],
                        s5_c_re[0], s5_c_im[0])
    y_s5 = _s5_scan(a_lat, a_ctx, mats)
    h1, f, top_w, meta, counts = _s5_glu(y_s5, a_lat, h_lat, mod[1], s5_d[0], s5_w_glu[0], s5_b_glu[0],
                                         norm2_g[1], router_w[1], router_b[1])
    y_asg = _moe(f, meta, counts, 1, moe_w1, moe_b1, moe_w2, moe_b2)
    out = _combine1(y_asg, top_w, h1, mod[1], final_g)
    return out.reshape(bsz, seq, d)
```

```python
import functools

import jax
import jax.numpy as jnp
from jax import lax
from jax.experimental import pallas as pl
from jax.experimental.pallas import tpu as pltpu

F32 = jnp.float32
BF16 = jnp.bfloat16
HIGHEST = lax.Precision.HIGHEST

RMS_EPS = 1e-6
GRID_W = 64
HEAD_DIM = 128
HG_CHUNK = 64
HG_BATCH = 2
S5_GROUP = 16
S5_STATE = 64
S5_EIG_MAX = -1e-4
S5_CHUNK = 8
S5_COLS = 8
LANES = 128
N_EXPERTS = 32
TOP_K = 4
SWIGLU_ALPHA = 1.702
SWIGLU_LIMIT = 7.0
TOK_TILE = 256
MOE_ROWS = 256
VMEM_LIMIT = 56 * 1024 * 1024


def _params(*sem):
    return pltpu.CompilerParams(dimension_semantics=sem, vmem_limit_bytes=VMEM_LIMIT)


def _rms(x, g):
    return x * lax.rsqrt(jnp.mean(x * x, axis=-1, keepdims=True) + RMS_EPS) * g


def _sigmoid(x):
    return 1.0 / (1.0 + jnp.exp(-x))


def _silu(x):
    return x * _sigmoid(x)


def _to_token_tiles(ref, x):
    n, d = x.shape
    for c in range(d // LANES):
        ref[pl.ds(c, n, stride=d // LANES), :] = x[:, c * LANES:(c + 1) * LANES]


def _from_token_tiles(ref, n, d):
    return jnp.concatenate([ref[pl.ds(c, n, stride=d // LANES), :] for c in range(d // LANES)], axis=-1)


def _mod_kernel(c_ref, w_ref, b_ref, o_ref):
    s = _silu(c_ref[...])
    o_ref[0] = jnp.dot(s, w_ref[0], precision=HIGHEST, preferred_element_type=F32) + b_ref[0]


def _modulation(c, c_ctx, mod_w, mod_b):
    bsz, d = c.shape
    depth = mod_w.shape[0]
    rows = jnp.concatenate([c, c_ctx[None], jnp.zeros((16 - bsz - 1, d), F32)], axis=0)
    bn = 6 * d // 4
    out = pl.pallas_call(
        _mod_kernel,
        grid=(depth, 4),
        in_specs=[pl.BlockSpec((16, d), lambda l, j: (0, 0)),
                  pl.BlockSpec((1, d, bn), lambda l, j: (l, 0, j)),
                  pl.BlockSpec((1, 1, bn), lambda l, j: (l, 0, j))],
        out_specs=pl.BlockSpec((1, 16, bn), lambda l, j: (l, 0, j)),
        out_shape=jax.ShapeDtypeStruct((depth, 16, 6 * d), F32),
        compiler_params=_params("parallel", "parallel"),
        name="modulation",
    )(rows, mod_w, mod_b.reshape(depth, 1, 6 * d))
    m_lat = out[:, :bsz].reshape(depth, bsz, 1, 6, d)
    m_ctx = jnp.broadcast_to(out[:, bsz].reshape(depth, 1, 1, 6, d), m_lat.shape)
    return jnp.concatenate([m_ctx, m_lat], axis=2).reshape(depth, 2 * bsz, 6, d)


def _residual_tile(hc_ref, hx_ref, n_ctx_tiles):
    return jnp.where(pl.program_id(1) < n_ctx_tiles, hc_ref[0], hx_ref[0])


def _residual_specs(d, n_ctx_tiles):
    return [pl.BlockSpec((1, TOK_TILE, d), lambda b, i: (b, jnp.minimum(i, n_ctx_tiles - 1), 0)),
            pl.BlockSpec((1, TOK_TILE, d), lambda b, i: (b, jnp.maximum(i - n_ctx_tiles, 0), 0))]


def _hg_proj_kernel(hc_ref, hx_ref, m_ref, g_ref, w_ref, lb_ref,
                    qs_ref, v_ref, sg_ref, kf_ref, kb_ref, lff_ref, lfb_ref, *, n_ctx_tiles):
    d = hx_ref.shape[-1]
    m = m_ref[0]
    a = _rms(_residual_tile(hc_ref, hx_ref, n_ctx_tiles), g_ref[...]) * (1.0 + m[1:2]) + m[0:1]
    p = jnp.dot(a.astype(BF16), w_ref[...], preferred_element_type=F32)
    qs_ref[0] = _silu(p[:, 0:d]).astype(BF16)
    v_ref[0] = p[:, d:2 * d].astype(BF16)
    sg_ref[0] = _silu(p[:, 4 * d:5 * d]).astype(BF16)
    for di, (k_ref, lf_ref) in enumerate(((kf_ref, lff_ref), (kb_ref, lfb_ref))):
        lb = lb_ref[di:di + 1]
        f = lb + (1.0 - lb) * _sigmoid(p[:, (2 + di) * d:(3 + di) * d])
        k_ref[0] = (1.0 - f).astype(BF16)
        lf_ref[0] = jnp.log(f)


def _mod_spec(d, n_ctx_tiles):
    return pl.BlockSpec((1, 6, d), lambda b, i: (2 * b + (i >= n_ctx_tiles).astype(jnp.int32), 0, 0))


def _hg_proj(ctx, x, mod, g1, w_in, lb, n_ctx_tiles):
    bsz, t, d = x.shape
    t += ctx.shape[1]
    tok = pl.BlockSpec((1, TOK_TILE, d), lambda b, i: (b, i, 0))
    bf = jax.ShapeDtypeStruct((bsz, t, d), BF16)
    ff = jax.ShapeDtypeStruct((bsz, t, d), F32)
    return pl.pallas_call(
        functools.partial(_hg_proj_kernel, n_ctx_tiles=n_ctx_tiles),
        grid=(bsz, t // TOK_TILE),
        in_specs=_residual_specs(d, n_ctx_tiles) + [_mod_spec(d, n_ctx_tiles),
                  pl.BlockSpec((1, d), lambda b, i: (0, 0)),
                  pl.BlockSpec((d, 5 * d), lambda b, i: (0, 0)),
                  pl.BlockSpec((2, d), lambda b, i: (0, 0))],
        out_specs=[tok] * 7,
        out_shape=[bf, bf, bf, bf, bf, ff, ff],
        compiler_params=_params("parallel", "parallel"),
        name="hg_proj",
    )(ctx, x, mod, g1.reshape(1, d), w_in.astype(BF16), lb)


def _split3(x):
    hi = x.astype(BF16)
    r = x - hi.astype(F32)
    mid = r.astype(BF16)
    lo = (r - mid.astype(F32)).astype(BF16)
    return hi, mid, lo


def _gla_kernel(qf_ref, vf_ref, kf_ref, lf_ref, qb_ref, vb_ref, kb_ref, lb_ref,
                of_ref, ob_ref, sf_ref, sb_ref):
    c = HG_CHUNK
    n_b = qf_ref.shape[0]
    n_heads = sf_ref.shape[0] // n_b

    @pl.when(pl.program_id(1) == 0)
    def _():
        sf_ref[...] = jnp.zeros_like(sf_ref)
        sb_ref[...] = jnp.zeros_like(sb_ref)

    row = lax.broadcasted_iota(jnp.int32, (c, c), 0)
    col = lax.broadcasted_iota(jnp.int32, (c, c), 1)
    dirs = ((qf_ref, vf_ref, kf_ref, lf_ref, of_ref, sf_ref, col <= row, c // 2 - 1, c - 1),
            (qb_ref, vb_ref, kb_ref, lb_ref, ob_ref, sb_ref, col >= row, c // 2, 0))
    for bb, (q_ref, v_ref, k_ref, l_ref, o_ref, s_ref, keep, r_ref, r_last) in (
            (bb, dr) for bb in range(n_b) for dr in dirs):
        tri = keep.astype(BF16)
        b = sum(jnp.dot(tri, part, preferred_element_type=F32) for part in _split3(l_ref[bb]))
        b_ref = b[r_ref:r_ref + 1]
        b_last = b[r_last:r_last + 1]
        a_in = q_ref[bb].astype(F32) * jnp.exp(b - b_ref)
        k_in = k_ref[bb].astype(F32) * jnp.exp(b_ref - b)
        q_st = (a_in * jnp.exp(b_ref)).astype(BF16)
        k_st = (k_in * jnp.exp(b_last - b_ref)).astype(BF16)
        a_in = a_in.astype(BF16)
        k_in = k_in.astype(BF16)
        decay = jnp.exp(b_last)
        v = v_ref[bb]
        for h in range(n_heads):
            sl = slice(h * HEAD_DIM, (h + 1) * HEAD_DIM)
            sc = lax.dot_general(a_in[:, sl], k_in[:, sl], (((1,), (1,)), ((), ())),
                                 preferred_element_type=F32)
            sc = jnp.where(keep, sc, 0.0).astype(BF16)
            o = jnp.dot(sc, v[:, sl], preferred_element_type=F32)
            st = s_ref[bb * n_heads + h]
            o = o + lax.dot_general(q_st[:, sl], st.astype(BF16), (((1,), (1,)), ((), ())),
                                    preferred_element_type=F32)
            s_ref[bb * n_heads + h] = st * decay[:, sl] + lax.dot_general(
                v[:, sl], k_st[:, sl], (((0,), (0,)), ((), ())), preferred_element_type=F32)
            o_ref[bb, :, sl] = o


def _gla(qs, v, kf, kb, lff, lfb, n_ctx_chunks):
    bsz, t, d = qs.shape
    n = t // HG_CHUNK
    n_heads = d // HEAD_DIM

    def rev(j):
        return jnp.where(j < n_ctx_chunks, n_ctx_chunks - 1 - j, n + n_ctx_chunks - 1 - j)

    fwd = pl.BlockSpec((HG_BATCH, HG_CHUNK, d), lambda b, j: (b, j, 0))
    bwd = pl.BlockSpec((HG_BATCH, HG_CHUNK, d), lambda b, j: (b, rev(j), 0))
    out = jax.ShapeDtypeStruct((bsz, t, d), F32)
    state = pltpu.VMEM((HG_BATCH * n_heads, HEAD_DIM, HEAD_DIM), F32)
    return pl.pallas_call(
        _gla_kernel,
        grid=(bsz // HG_BATCH, n),
        in_specs=[fwd, fwd, fwd, fwd, bwd, bwd, bwd, bwd],
        out_specs=[fwd, bwd],
        out_shape=[out, out],
        scratch_shapes=[state, state],
        compiler_params=_params("parallel", "arbitrary"),
        name="gla",
    )(qs, v, kf, lff, qs, v, kb, lfb)


def _mixer_tail(first, h, y, m, g2, rw, rb, run_ref, h1_ref, f_ref, tw_ref, meta_ref, cnt_ref):
    h1 = h + m[2:3] * y
    f = _rms(h1, g2) * (1.0 + m[4:5]) + m[3:4]
    h1_ref[...] = h1.reshape(h1_ref.shape)
    _to_token_tiles(f_ref, f)
    f_hi, f_lo, _ = _split3(f)
    w_hi, w_lo, _ = _split3(rw)
    logits = (jnp.dot(f_hi, w_hi, preferred_element_type=F32) + jnp.dot(f_hi, w_lo, preferred_element_type=F32)
              + jnp.dot(f_lo, w_hi, preferred_element_type=F32)) + rb
    n, n_e = logits.shape
    lane = lax.broadcasted_iota(jnp.int32, logits.shape, 1).astype(F32)
    vals, idxs, hots = [], [], []
    for _ in range(TOP_K):
        mx = jnp.max(logits, axis=-1, keepdims=True)
        ix = jnp.min(jnp.where(logits == mx, lane, float(n_e)), axis=-1, keepdims=True)
        hot = lane == ix
        vals.append(mx)
        idxs.append(ix)
        hots.append(hot)
        logits = jnp.where(hot, -jnp.inf, logits)
    es = [jnp.exp(x - vals[0]) for x in vals]
    tot = sum(es)
    for k in range(TOP_K):
        tw_ref[:, k:k + 1] = es[k] / tot

    @pl.when(first)
    def _():
        run_ref[...] = jnp.zeros_like(run_ref)

    picked = sum(hot.astype(F32) for hot in hots)
    r_i = lax.broadcasted_iota(jnp.int32, (n, n), 0)
    c_i = lax.broadcasted_iota(jnp.int32, (n, n), 1)
    earlier = jnp.dot((c_i < r_i).astype(BF16), picked.astype(BF16), preferred_element_type=F32)
    rank = earlier + run_ref[...]
    total = run_ref[...] + jnp.sum(picked, axis=0, keepdims=True)
    run_ref[...] = total
    cnt_ref[...] = total
    col = lax.broadcasted_iota(jnp.int32, (n, LANES), 1)
    z = jnp.zeros((n, LANES), F32)
    for k in range(TOP_K):
        pos = jnp.sum(jnp.where(hots[k], rank, 0.0), axis=-1, keepdims=True)
        z = jnp.where(col == k, idxs[k], z)
        z = jnp.where(col == TOP_K + k, pos, z)
    meta_ref[0] = z.T[0:2 * TOP_K].astype(jnp.int32)


def _tail_outputs(n_tiles, tile, d, n_e, index):
    rt = d // LANES
    specs = [pl.BlockSpec((tile * rt, LANES), lambda b, i: (index(b, i), 0)),
             pl.BlockSpec((1, tile, TOP_K), lambda b, i: (index(b, i), 0, 0)),
             pl.BlockSpec((1, 2 * TOP_K, tile), lambda b, i: (index(b, i), 0, 0)),
             pl.BlockSpec((1, n_e), lambda b, i: (0, 0))]
    shapes = [jax.ShapeDtypeStruct((n_tiles * tile * rt, LANES), F32),
              jax.ShapeDtypeStruct((n_tiles, tile, TOP_K), F32),
              jax.ShapeDtypeStruct((n_tiles, 2 * TOP_K, tile), jnp.int32),
              jax.ShapeDtypeStruct((1, n_e), F32)]
    return specs, shapes


def _hg_readout_kernel(of_ref, ob_ref, sg_ref, hc_ref, hx_ref, m_ref, ng_ref, w_ref, g2_ref, rw_ref, rb_ref,
                       h1_ref, f_ref, tw_ref, meta_ref, cnt_ref, run_ref, *, n_ctx_tiles):
    d = hx_ref.shape[-1]
    o = of_ref[0] + ob_ref[0]
    parts = []
    for h in range(d // HEAD_DIM):
        oh = o[:, h * HEAD_DIM:(h + 1) * HEAD_DIM]
        parts.append(oh * lax.rsqrt(jnp.mean(oh * oh, axis=-1, keepdims=True) + RMS_EPS))
    o = jnp.concatenate(parts, axis=-1) * ng_ref[...]
    y = jnp.dot((o * sg_ref[0].astype(F32)).astype(BF16), w_ref[...], preferred_element_type=F32)
    first = (pl.program_id(0) == 0) & (pl.program_id(1) == 0)
    _mixer_tail(first, _residual_tile(hc_ref, hx_ref, n_ctx_tiles), y, m_ref[0], g2_ref[...], rw_ref[...],
                rb_ref[...], run_ref, h1_ref, f_ref, tw_ref.at[0], meta_ref, cnt_ref)


def _hg_readout(o_f, o_b, sg, ctx, x, mod, norm_g, w_out, g2, rw, rb, n_ctx_tiles):
    bsz, t, d = o_f.shape
    n_e = rw.shape[-1]
    nt = t // TOK_TILE
    tok = pl.BlockSpec((1, TOK_TILE, d), lambda b, i: (b, i, 0))
    const = lambda shape: pl.BlockSpec(shape, lambda b, i: (0,) * len(shape))
    tail_specs, tail_shapes = _tail_outputs(bsz * nt, TOK_TILE, d, n_e, lambda b, i: b * nt + i)
    return pl.pallas_call(
        functools.partial(_hg_readout_kernel, n_ctx_tiles=n_ctx_tiles),
        grid=(bsz, nt),
        in_specs=[tok, tok, tok] + _residual_specs(d, n_ctx_tiles) + [_mod_spec(d, n_ctx_tiles),
                  const((1, d)), const((d, d)), const((1, d)), const((d, n_e)), const((1, n_e))],
        out_specs=[tok] + tail_specs,
        out_shape=[jax.ShapeDtypeStruct((bsz, t, d), F32)] + tail_shapes,
        scratch_shapes=[pltpu.VMEM((1, n_e), F32)],
        compiler_params=_params("arbitrary", "arbitrary"),
        name="hg_readout",
    )(o_f, o_b, sg, ctx, x, mod, norm_g.reshape(1, d), w_out.astype(BF16), g2.reshape(1, d),
      rw, rb.reshape(1, n_e))


def _moe_plan(counts, n_asg, rows):
    counts = counts.reshape(-1).astype(jnp.int32)
    padded = (counts + rows - 1) // rows * rows
    pad_end = jnp.cumsum(padded)
    n_blocks = -(-n_asg // rows) + N_EXPERTS
    block_pos = jnp.arange(n_blocks, dtype=jnp.int32) * rows
    block_e = jnp.minimum(jnp.sum((block_pos[:, None] >= pad_end[None, :]).astype(jnp.int32), axis=1),
                          N_EXPERTS - 1)
    n_used = (pad_end[-1:] // rows).astype(jnp.int32)
    return pad_end - padded, block_e, n_used, n_blocks


def _invert_kernel(dest_ref, init_hbm, inv_ref, sem):
    i = pl.program_id(0)
    tile = dest_ref.shape[2]

    @pl.when(i == 0)
    def _():
        cp = pltpu.make_async_copy(init_hbm, inv_ref, sem)
        cp.start()
        cp.wait()

    base = i * (tile * TOP_K)
    for r in range(tile):
        for k in range(TOP_K):
            inv_ref[dest_ref[0, k, r]] = base + (r * TOP_K + k)


def _invert(meta, slot_start, n_slots):
    n_tiles, _, tile = meta.shape
    hot = meta[:, :TOP_K, :, None] == jnp.arange(N_EXPERTS, dtype=jnp.int32)
    dest = meta[:, TOP_K:] + jnp.sum(jnp.where(hot, slot_start, 0), axis=-1)
    return pl.pallas_call(
        _invert_kernel,
        grid=(n_tiles,),
        in_specs=[pl.BlockSpec((1, TOP_K, tile), lambda i: (i, 0, 0), memory_space=pltpu.SMEM),
                  pl.BlockSpec(memory_space=pl.ANY)],
        out_specs=pl.BlockSpec(memory_space=pltpu.SMEM),
        out_shape=jax.ShapeDtypeStruct((n_slots,), jnp.int32),
        scratch_shapes=[pltpu.SemaphoreType.DMA(())],
        compiler_params=_params("arbitrary"),
        name="moe_invert",
    )(dest.astype(jnp.int32), jnp.full((n_slots,), -1, jnp.int32))


def _ffn_kernel(be_ref, nu_ref, tok_ref, tok_next_ref, dst_prev_ref, dst_ref, x_hbm,
                w1_ref, b1_ref, w2_ref, b2_ref, y_hbm, buf_p, buf_q, zbuf, w1c, w2c, gsem, ssem, zsem):
    xbuf = (buf_p.at[0], buf_q.at[0])
    ybuf = (buf_q.at[1], buf_p.at[1])
    b = pl.program_id(0)
    nu = nu_ref[0]
    d, f2 = w1c.shape
    rt = d // LANES
    rows = buf_p.shape[1] // rt

    def tile_of(ref, idx):
        return ref.at[pl.ds(pl.multiple_of(idx * rt, rt), rt)]

    def gather(idx_ref, s):
        for r in range(rows):
            pltpu.make_async_copy(tile_of(x_hbm, idx_ref[0, 0, r]), xbuf[s].at[pl.ds(r * rt, rt)],
                                  gsem.at[s]).start(priority=r % 2)

    def scatter(idx_ref, s):
        for r in range(rows):
            pltpu.make_async_copy(ybuf[s].at[pl.ds(r * rt, rt)], tile_of(y_hbm, idx_ref[0, 0, r]),
                                  ssem.at[s]).start(priority=r % 2)

    def wait_gather(s):
        pltpu.make_async_copy(x_hbm.at[pl.ds(0, rows * rt)], xbuf[s], gsem.at[s]).wait()

    def wait_scatter(s):
        pltpu.make_async_copy(ybuf[s], y_hbm.at[pl.ds(0, rows * rt)], ssem.at[s]).wait()

    @pl.when(b == 0)
    def _():
        zbuf[...] = jnp.zeros_like(zbuf)
        buf_p[1] = jnp.zeros(buf_p.shape[1:], F32)
        gather(tok_ref, 0)

    @pl.when(b >= nu)
    def _():
        dst = y_hbm.at[pl.ds(pl.multiple_of(b * (rows * rt), rows * rt), rows * rt)]
        cp = pltpu.make_async_copy(zbuf, dst, zsem)
        cp.start()
        cp.wait()

    def used_block(s):
        @pl.when((b == 0) | (be_ref[b] != be_ref[jnp.maximum(b - 1, 0)]))
        def _():
            w1c[...] = w1_ref[0, 0].astype(BF16)
            w2c[...] = w2_ref[0, 0].astype(BF16)

        wait_gather(s)

        @pl.when(b >= 1)
        def _():
            wait_scatter(s)

        x_head = xbuf[s][0:rt]
        x = _from_token_tiles(xbuf[s], rows, d).astype(BF16)
        gather(tok_next_ref, 1 - s)
        scatter(dst_prev_ref, 1 - s)
        z = jnp.dot(x, w1c[...], preferred_element_type=F32) + b1_ref[0, 0]
        z_glu = jnp.minimum(z[:, :f2 // 2], SWIGLU_LIMIT)
        z_lin = jnp.clip(z[:, f2 // 2:], -SWIGLU_LIMIT, SWIGLU_LIMIT)
        act = z_glu * _sigmoid(SWIGLU_ALPHA * z_glu) * (z_lin + 1.0)
        y = jnp.dot(act.astype(BF16), w2c[...], preferred_element_type=F32) + b2_ref[0, 0]
        _to_token_tiles(ybuf[s], y)
        xbuf[s][0:rt] = x_head

        @pl.when(b == nu - 1)
        def _():
            scatter(dst_ref, s)
            wait_gather(1 - s)
            wait_scatter(1 - s)
            wait_scatter(s)

    for s in range(2):
        pl.when((b < nu) & (b % 2 == s))(functools.partial(used_block, s))


def _moe(f_tiles, meta, counts, layer, w1, b1, w2, b2):
    n_tiles, _, tile = meta.shape
    n_asg = n_tiles * tile * TOP_K
    _, n_e, d, f2 = w1.shape
    rt = d // LANES
    slot_start, block_e, n_used, n_blocks = _moe_plan(counts, n_asg, MOE_ROWS)
    n_slots = n_blocks * MOE_ROWS
    inv = _invert(meta, slot_start, n_slots)
    is_pad = inv < 0
    slot_tok = jnp.where(is_pad, 0, inv // TOP_K).reshape(n_blocks, 1, MOE_ROWS)
    spill = n_asg + jnp.cumsum(is_pad.astype(jnp.int32)) - 1
    slot_dst = jnp.where(is_pad, spill, inv)
    slot_dst = jnp.concatenate([slot_dst, jnp.arange(n_slots - MOE_ROWS, n_slots, dtype=jnp.int32)])
    slot_dst = slot_dst.reshape(n_blocks + 1, 1, MOE_ROWS)

    live = lambda b, nu: jnp.minimum(b, nu[0] - 1)
    smem = lambda imap: pl.BlockSpec((1, 1, MOE_ROWS), imap, memory_space=pltpu.SMEM)
    per_e = lambda shape: pl.BlockSpec((1, 1) + shape, lambda b, be, nu: (layer, be[live(b, nu)], 0, 0))
    buf = pltpu.VMEM((MOE_ROWS * rt, LANES), F32)
    bufs = pltpu.VMEM((2, MOE_ROWS * rt, LANES), F32)
    grid_spec = pltpu.PrefetchScalarGridSpec(
        num_scalar_prefetch=2,
        grid=(n_blocks,),
        in_specs=[smem(lambda b, be, nu: (live(b, nu), 0, 0)),
                  smem(lambda b, be, nu: (live(b + 1, nu), 0, 0)),
                  smem(lambda b, be, nu: (jnp.where(b == 0, n_blocks, live(b - 1, nu)), 0, 0)),
                  smem(lambda b, be, nu: (live(b, nu), 0, 0)),
                  pl.BlockSpec(memory_space=pl.ANY),
                  per_e((d, f2)), per_e((1, f2)), per_e((f2 // 2, d)), per_e((1, d))],
        out_specs=pl.BlockSpec(memory_space=pl.ANY),
        scratch_shapes=[bufs, bufs, buf,
                        pltpu.VMEM((d, f2), BF16), pltpu.VMEM((f2 // 2, d), BF16),
                        pltpu.SemaphoreType.DMA((2,)), pltpu.SemaphoreType.DMA((2,)),
                        pltpu.SemaphoreType.DMA(())])
    return pl.pallas_call(
        _ffn_kernel,
        grid_spec=grid_spec,
        out_shape=jax.ShapeDtypeStruct((n_slots * rt, LANES), F32),
        compiler_params=_params("arbitrary"),
        name="moe_ffn",
    )(block_e, n_used, slot_tok, slot_tok, slot_dst, slot_dst, f_tiles, w1, b1.reshape(b1.shape[0], n_e, 1, f2),
      w2, b2.reshape(b2.shape[0], n_e, 1, d))


def _combine_experts(y_ref, tw_ref, tile, d):
    rt = d // LANES
    tw = tw_ref[0]
    out = 0.0
    for k in range(TOP_K):
        yk = jnp.concatenate([y_ref[pl.ds(k * rt + c, tile, stride=TOP_K * rt), :] for c in range(rt)],
                             axis=-1)
        out = out + tw[:, k:k + 1] * yk
    return out


def _combine0_kernel(y_ref, tw_ref, h_ref, m0_ref, m1_ref, g_ref, hl_ref, al_ref, ac_ref, *, n_ctx_tiles):
    tile, d = h_ref.shape[1:]
    h2 = h_ref[0] + m0_ref[0][5:6] * _combine_experts(y_ref, tw_ref, tile, d)
    m1 = m1_ref[0]
    a = _rms(h2, g_ref[...]) * (1.0 + m1[1:2]) + m1[0:1]
    is_ctx = pl.program_id(1) < n_ctx_tiles

    @pl.when(is_ctx)
    def _():
        ac_ref[0] = a

    @pl.when(jnp.logical_not(is_ctx))
    def _():
        hl_ref[0] = h2
        al_ref[0] = a


def _combine0(y_asg, top_w, h1, mod0, mod1, g1_next, n_ctx_tiles):
    bsz, t, d = h1.shape
    nt = t // TOK_TILE
    n_ctx = n_ctx_tiles * TOK_TILE
    tok = pl.BlockSpec((1, TOK_TILE, d), lambda b, i: (b, i, 0))
    lat = pl.BlockSpec((1, TOK_TILE, d), lambda b, i: (b, jnp.maximum(i - n_ctx_tiles, 0), 0))
    ctx = pl.BlockSpec((1, TOK_TILE, d), lambda b, i: (b, jnp.minimum(i, n_ctx_tiles - 1), 0))
    return pl.pallas_call(
        functools.partial(_combine0_kernel, n_ctx_tiles=n_ctx_tiles),
        grid=(bsz, nt),
        in_specs=[pl.BlockSpec((TOK_TILE * TOP_K * d // LANES, LANES), lambda b, i: (b * nt + i, 0)),
                  pl.BlockSpec((1, TOK_TILE, TOP_K), lambda b, i: (b * nt + i, 0, 0)),
                  tok, _mod_spec(d, n_ctx_tiles), _mod_spec(d, n_ctx_tiles),
                  pl.BlockSpec((1, d), lambda b, i: (0, 0))],
        out_specs=[lat, lat, ctx],
        out_shape=[jax.ShapeDtypeStruct((bsz, t - n_ctx, d), F32),
                   jax.ShapeDtypeStruct((bsz, t - n_ctx, d), F32),
                   jax.ShapeDtypeStruct((bsz, n_ctx, d), F32)],
        compiler_params=_params("parallel", "parallel"),
        name="combine0",
    )(y_asg, top_w, h1, mod0, mod1, g1_next.reshape(1, d))


def _s5_matrices(a_re, a_im, log_dt, b_re, b_im, c_re, c_im):
    ng, p = a_re.shape[1:]
    gc = b_re.shape[-1]
    ll = S5_CHUNK
    gpb = LANES // gc
    nblk = ng // gpb
    lam_re = jnp.minimum(a_re, S5_EIG_MAX)
    lam_im = a_im
    dt = jnp.exp(log_dt)[..., None]
    j = jnp.arange(ll + 1, dtype=F32).reshape(-1, 1, 1, 1)
    mag = jnp.exp(j * (lam_re * dt))
    pw_re = mag * jnp.cos(j * (lam_im * dt))
    pw_im = mag * jnp.sin(j * (lam_im * dt))
    ab_re, ab_im = pw_re[1], pw_im[1]
    den = lam_re * lam_re + lam_im * lam_im
    coef_re = ((ab_re - 1.0) * lam_re + ab_im * lam_im) / den
    coef_im = (ab_im * lam_re - (ab_re - 1.0) * lam_im) / den
    bb_re = coef_re[..., None] * b_re - coef_im[..., None] * b_im
    bb_im = coef_re[..., None] * b_im + coef_im[..., None] * b_re
    drv_re = pw_re[..., None] * bb_re - pw_im[..., None] * bb_im
    drv_im = pw_re[..., None] * bb_im + pw_im[..., None] * bb_re
    rd_re = c_re * pw_re[:, :, :, None, :] - c_im * pw_im[:, :, :, None, :]
    rd_im = -(c_re * pw_im[:, :, :, None, :] + c_im * pw_re[:, :, :, None, :])
    taps = (jnp.einsum('dgop,jdgpi->jdgoi', c_re, drv_re[:ll], precision=HIGHEST)
            - jnp.einsum('dgop,jdgpi->jdgoi', c_im, drv_im[:ll], precision=HIGHEST))
    r = jnp.arange(ll)
    c_in, c_drv, c_rd = [], [], []
    for d in range(2):
        lag = (r[None, :] - r[:, None]) if d == 0 else (r[:, None] - r[None, :])
        tp = jnp.where((lag >= 0)[:, :, None, None, None],
                       taps[:, d][jnp.clip(lag, 0, ll - 1)], 0.0)
        tp = tp.reshape(ll, ll, nblk, gpb, gc, gc).transpose(2, 0, 3, 5, 1, 4)
        c_in.append(tp.reshape(nblk, ll * LANES, ll * gc))
        steps = (ll - 1 - r) if d == 0 else r
        dr = jnp.stack([drv_re[:, d][steps], drv_im[:, d][steps]], axis=1)
        dr = dr.reshape(ll, 2, nblk, gpb, p, gc).transpose(2, 0, 3, 5, 1, 4)
        c_drv.append(dr.reshape(nblk, ll * LANES, 2 * p))
        steps = (r + 1) if d == 0 else (ll - r)
        rd = jnp.stack([rd_re[:, d][steps], rd_im[:, d][steps]], axis=1)
        rd = rd.reshape(ll, 2, nblk, gpb, gc, p).transpose(2, 1, 3, 5, 0, 4)
        c_rd.append(rd.reshape(nblk, 2 * gpb * p, ll * gc))

    def expand(compact, row_unit, col_unit):
        compact = jnp.stack(compact).astype(BF16)
        n_r, n_c = compact.shape[2], compact.shape[3] * gpb
        col = jnp.arange(n_c)
        src = (col // (col_unit * gpb)) * col_unit + col % col_unit
        spread = (jnp.arange(n_c // gpb)[:, None] == src[None, :]).astype(BF16)
        full = jnp.einsum('dbrk,kc->dbrc', compact, spread, preferred_element_type=F32)
        same = ((jnp.arange(n_r) // row_unit) % gpb)[:, None] == ((col // col_unit) % gpb)[None, :]
        return jnp.where(same, full, 0.0).astype(BF16)

    a8 = jnp.stack([pw_re[ll], pw_im[ll]], axis=1).reshape(2, 2, nblk, gpb * p).transpose(0, 2, 1, 3)
    return expand([c_in[0] + c_in[1]], gc, gc)[0], expand(c_drv, gc, p), expand(c_rd, p, gc), a8


def _s5_kernel(uc_ref, ul_ref, min_ref, mdrv_ref, mrd_ref, a8_ref, y_ref, v_ref, st_ref, *, n_ctx_cols):
    bsz, n_oct, ll, n_col, lanes = ul_ref.shape
    n_rows = bsz * n_oct * n_col
    half = st_ref.shape[-1] // 2
    d = pl.program_id(0)
    k = pl.program_id(2)

    @pl.when(k == 0)
    def _():
        st_ref[...] = jnp.zeros_like(st_ref)

    def chunks(u_ref):
        return jnp.concatenate([u_ref[:, :, r].reshape(n_rows, lanes) for r in range(ll)], axis=-1)

    x = jnp.where(k == 0, chunks(uc_ref), chunks(ul_ref)).astype(BF16)
    inj_all = jnp.dot(x, mdrv_ref[0, 0], preferred_element_type=F32)
    n_pl = v_ref.shape[0]
    for c in range(n_pl):
        v_ref[c] = inj_all[:, c * lanes:(c + 1) * lanes]

    a_re = jnp.broadcast_to(a8_ref[0, 0, 0:1], (bsz, half))
    a_im = jnp.broadcast_to(a8_ref[0, 0, 1:2], (bsz, half))
    def scan(n_steps):
        def step(i, carry):
            s_re, s_im = carry
            i = jnp.where(d == 0, i, n_steps - 1 - i)
            row = (i % n_oct) * n_col + i // n_oct
            rows = pl.ds(row, bsz, stride=n_oct * n_col)
            inj = jnp.concatenate([v_ref[c, rows, :] for c in range(n_pl)], axis=-1)
            for c in range(n_pl // 2):
                v_ref[c, rows, :] = s_re[:, c * lanes:(c + 1) * lanes]
                v_ref[n_pl // 2 + c, rows, :] = s_im[:, c * lanes:(c + 1) * lanes]
            return (a_re * s_re - a_im * s_im + inj[:, :half],
                    a_re * s_im + a_im * s_re + inj[:, half:])

        s_re, s_im = lax.fori_loop(0, n_steps, step, (st_ref[:, :half], st_ref[:, half:]), unroll=2)
        st_ref[:, :half] = s_re
        st_ref[:, half:] = s_im

    @pl.when(k == 0)
    def _():
        scan(n_ctx_cols * n_oct)

    @pl.when(k > 0)
    def _():
        scan(n_col * n_oct)

    s_start = jnp.concatenate([v_ref[c] for c in range(n_pl)], axis=-1).astype(BF16)
    y = jnp.dot(s_start, mrd_ref[0, 0], preferred_element_type=F32)

    def emit(y):
        for r in range(ll):
            y_ref[0, :, :, r] = y[:, r * lanes:(r + 1) * lanes].reshape(bsz, n_oct, n_col, lanes)

    @pl.when(d == 0)
    def _():
        emit(y + jnp.dot(x, min_ref[0], preferred_element_type=F32))

    @pl.when(d != 0)
    def _():
        emit(y)


def _s5_scan(a_lat, a_ctx, mats):
    bsz, t, d = a_lat.shape
    n_ctx = a_ctx.shape[1]
    m_in, m_drv, m_rd, a8 = mats
    n_rows = t // GRID_W
    n_oct = n_rows // S5_CHUNK
    n_ctx_cols = n_ctx // n_rows
    u_lat = a_lat.reshape(bsz, n_oct, S5_CHUNK, GRID_W, d)
    u_ctx = a_ctx.reshape(bsz, n_ctx_cols, n_rows, d).transpose(0, 2, 1, 3)
    u_ctx = jnp.pad(u_ctx, ((0, 0), (0, 0), (0, S5_COLS - n_ctx_cols), (0, 0)))
    u_ctx = u_ctx.reshape(bsz, n_oct, S5_CHUNK, S5_COLS, d)
    n_lat = GRID_W // S5_COLS
    nblk = d // LANES

    def lat_tile(dd, k):
        kk = jnp.maximum(k, 1) - 1
        return jnp.where(dd == 0, kk, n_lat - 1 - kk)

    blk = (bsz, n_oct, S5_CHUNK, S5_COLS, LANES)
    wspec = lambda shape: pl.BlockSpec((1, 1) + shape, lambda dd, j, k: (dd, j, 0, 0))
    kl = S5_CHUNK * LANES
    ns = m_drv.shape[-1]
    y = pl.pallas_call(
        functools.partial(_s5_kernel, n_ctx_cols=n_ctx_cols),
        grid=(2, nblk, n_lat + 1),
        in_specs=[pl.BlockSpec(blk, lambda dd, j, k: (0, 0, 0, 0, j)),
                  pl.BlockSpec(blk, lambda dd, j, k: (0, 0, 0, lat_tile(dd, k), j)),
                  pl.BlockSpec((1, kl, kl), lambda dd, j, k: (j, 0, 0)), wspec((kl, ns)), wspec((ns, kl)),
                  pl.BlockSpec((1, 1, 2, ns // 2), lambda dd, j, k: (dd, j, 0, 0))],
        out_specs=pl.BlockSpec((1,) + blk, lambda dd, j, k: (dd, 0, 0, 0, lat_tile(dd, k), j)),
        out_shape=jax.ShapeDtypeStruct((2,) + u_lat.shape, F32),
        scratch_shapes=[pltpu.VMEM((ns // LANES, bsz * n_oct * S5_COLS, LANES), F32),
                        pltpu.VMEM((bsz, ns), F32)],
        compiler_params=_params("parallel", "parallel", "arbitrary"),
        name="s5_scan",
    )(u_ctx, u_lat, m_in, m_drv, m_rd, a8)
    return y.reshape(2, bsz, n_rows, GRID_W, d)


def _s5_glu_kernel(y_ref, u_ref, h_ref, m_ref, dsk_ref, w_ref, bg_ref, g2_ref, rw_ref, rb_ref,
                   h1_ref, f_ref, tw_ref, meta_ref, cnt_ref, run_ref):
    d = h_ref.shape[-1]
    n = h_ref.shape[1] * h_ref.shape[2]
    y = (y_ref[0, 0] + y_ref[1, 0] + dsk_ref[...] * u_ref[0]).reshape(n, d)
    z = jnp.dot(jax.nn.gelu(y).astype(BF16), w_ref[...], preferred_element_type=F32) + bg_ref[...]
    y = z[:, :d] * _sigmoid(z[:, d:])
    first = (pl.program_id(0) == 0) & (pl.program_id(1) == 0)
    _mixer_tail(first, h_ref[0].reshape(n, d), y, m_ref[0], g2_ref[...], rw_ref[...], rb_ref[...],
                run_ref, h1_ref, f_ref, tw_ref.at[0], meta_ref, cnt_ref)


def _s5_glu(y, a_lat, h_lat, mod, d_skip, w_glu, b_glu, g2, rw, rb):
    bsz, t, d = h_lat.shape
    n_rows = t // GRID_W
    n_e = rw.shape[-1]
    n_tok = n_rows * S5_COLS
    nt = GRID_W // S5_COLS
    lat = pl.BlockSpec((1, n_rows, S5_COLS, d), lambda b, i: (b, 0, i, 0))
    tail_specs, tail_shapes = _tail_outputs(bsz * nt, n_tok, d, n_e, lambda b, i: b * nt + i)
    const = lambda shape: pl.BlockSpec(shape, lambda b, i: (0,) * len(shape))
    return pl.pallas_call(
        _s5_glu_kernel,
        grid=(bsz, nt),
        in_specs=[pl.BlockSpec((2, 1, n_rows, S5_COLS, d), lambda b, i: (0, b, 0, i, 0)),
                  lat, lat, pl.BlockSpec((1, 6, d), lambda b, i: (2 * b + 1, 0, 0)),
                  const((1, d)), const((d, 2 * d)), const((1, 2 * d)), const((1, d)),
                  const((d, n_e)), const((1, n_e))],
        out_specs=[lat] + tail_specs,
        out_shape=[jax.ShapeDtypeStruct((bsz, n_rows, GRID_W, d), F32)] + tail_shapes,
        scratch_shapes=[pltpu.VMEM((1, n_e), F32)],
        compiler_params=_params("arbitrary", "arbitrary"),
        name="s5_glu",
    )(y, a_lat.reshape(bsz, n_rows, GRID_W, d), h_lat.reshape(bsz, n_rows, GRID_W, d), mod,
      d_skip.reshape(1, d), w_glu.astype(BF16), b_glu.reshape(1, 2 * d),
      g2.reshape(1, d), rw, rb.reshape(1, n_e))


def _combine1_kernel(y_ref, tw_ref, h_ref, m_ref, g_ref, o_ref):
    d = h_ref.shape[-1]
    n = h_ref.shape[1] * h_ref.shape[2]
    h2 = h_ref[0].reshape(n, d) + m_ref[0][5:6] * _combine_experts(y_ref, tw_ref, n, d)
    o_ref[0] = _rms(h2, g_ref[...]).reshape(o_ref.shape[1:])


def _combine1(y_asg, top_w, h1, mod, final_g):
    bsz, n_rows, _, d = h1.shape
    nt = GRID_W // S5_COLS
    n_tok = n_rows * S5_COLS
    lat = pl.BlockSpec((1, n_rows, S5_COLS, d), lambda b, i: (b, 0, i, 0))
    return pl.pallas_call(
        _combine1_kernel,
        grid=(bsz, nt),
        in_specs=[pl.BlockSpec((n_tok * TOP_K * d // LANES, LANES), lambda b, i: (b * nt + i, 0)),
                  pl.BlockSpec((1, n_tok, TOP_K), lambda b, i: (b * nt + i, 0, 0)),
                  lat, pl.BlockSpec((1, 6, d), lambda b, i: (2 * b + 1, 0, 0)),
                  pl.BlockSpec((1, d), lambda b, i: (0, 0))],
        out_specs=lat,
        out_shape=jax.ShapeDtypeStruct(h1.shape, F32),
        compiler_params=_params("parallel", "parallel"),
        name="combine1",
    )(y_asg, top_w, h1, mod, final_g.reshape(1, d))


def kernel(x, c, ctx, c_ctx, mod_w, mod_b, norm1_g, norm2_g, hg_w_in, hg_lb_logits, hg_norm_g,
           hg_w_out, s5_a_re, s5_a_im, s5_log_dt, s5_b_re, s5_b_im, s5_c_re, s5_c_im, s5_d, s5_w_glu,
           s5_b_glu, router_w, router_b, moe_w1, moe_b1, moe_w2, moe_b2, final_g):
    bsz, seq, d = x.shape
    n_ctx = ctx.shape[1]
    assert mod_w.shape[0] == 2 and n_ctx % TOK_TILE == 0 and seq % TOK_TILE == 0
    n_ctx_tiles = n_ctx // TOK_TILE
    lb_all = jnp.cumsum(jax.nn.softmax(hg_lb_logits.astype(F32), axis=0), axis=0)
    mod = _modulation(c, c_ctx, mod_w, mod_b)

    qs, v, sg, kf, kb, lff, lfb = _hg_proj(ctx, x, mod[0], norm1_g[0], hg_w_in[0], lb_all[0], n_ctx_tiles)
    o_f, o_b = _gla(qs, v, kf, kb, lff, lfb, n_ctx // HG_CHUNK)
    h1, f, top_w, meta, counts = _hg_readout(o_f, o_b, sg, ctx, x, mod[0], hg_norm_g[0], hg_w_out[0],
                                             norm2_g[0], router_w[0], router_b[0], n_ctx_tiles)
    y_asg = _moe(f, meta, counts, 0, moe_w1, moe_b1, moe_w2, moe_b2)
    h_lat, a_lat, a_ctx = _combine0(y_asg, top_w, h1, mod[0], mod[1], norm1_g[1], n_ctx_tiles)

    mats = _s5_matrices(s5_a_re[0], s5_a_im[0], s5_log_dt[0], s5_b_re[0], s5_b_im[0],
                        s5_c_re[0], s5_c_im[0])
    y_s5 = _s5_scan(a_lat, a_ctx, mats)
    h1, f, top_w, meta, counts = _s5_glu(y_s5, a_lat, h_lat, mod[1], s5_d[0], s5_w_glu[0], s5_b_glu[0],
                                         norm2_g[1], router_w[1], router_b[1])
    y_asg = _moe(f, meta, counts, 1, moe_w1, moe_b1, moe_w2, moe_b2)
    out = _combine1(y_asg, top_w, h1, mod[1], final_g)
    return out.reshape(bsz, seq, d)
```

```python
import functools

import jax
import jax.numpy as jnp
from jax import lax
from jax.experimental import pallas as pl
from jax.experimental.pallas import tpu as pltpu

F32 = jnp.float32
BF16 = jnp.bfloat16
HIGHEST = lax.Precision.HIGHEST

RMS_EPS = 1e-6
GRID_W = 64
HEAD_DIM = 128
HG_CHUNK = 64
HG_BATCH = 4
S5_GROUP = 16
S5_STATE = 64
S5_EIG_MAX = -1e-4
S5_CHUNK = 8
S5_COLS = 8
LANES = 128
N_EXPERTS = 32
TOP_K = 4
SWIGLU_ALPHA = 1.702
SWIGLU_LIMIT = 7.0
TOK_TILE = 256
MOE_ROWS = 256
VMEM_LIMIT = 56 * 1024 * 1024


def _params(*sem):
    return pltpu.CompilerParams(dimension_semantics=sem, vmem_limit_bytes=VMEM_LIMIT)


def _rms(x, g):
    return x * lax.rsqrt(jnp.mean(x * x, axis=-1, keepdims=True) + RMS_EPS) * g


def _sigmoid(x):
    return 1.0 / (1.0 + jnp.exp(-x))


def _silu(x):
    return x * _sigmoid(x)


def _to_token_tiles(ref, x):
    n, d = x.shape
    for c in range(d // LANES):
        ref[pl.ds(c, n, stride=d // LANES), :] = x[:, c * LANES:(c + 1) * LANES]


def _from_token_tiles(ref, n, d):
    return jnp.concatenate([ref[pl.ds(c, n, stride=d // LANES), :] for c in range(d // LANES)], axis=-1)


def _mod_kernel(c_ref, w_ref, b_ref, o_ref):
    s = _silu(c_ref[...])
    o_ref[0] = jnp.dot(s, w_ref[0], precision=HIGHEST, preferred_element_type=F32) + b_ref[0]


def _modulation(c, c_ctx, mod_w, mod_b):
    bsz, d = c.shape
    depth = mod_w.shape[0]
    rows = jnp.concatenate([c, c_ctx[None], jnp.zeros((16 - bsz - 1, d), F32)], axis=0)
    bn = 6 * d // 4
    out = pl.pallas_call(
        _mod_kernel,
        grid=(depth, 4),
        in_specs=[pl.BlockSpec((16, d), lambda l, j: (0, 0)),
                  pl.BlockSpec((1, d, bn), lambda l, j: (l, 0, j)),
                  pl.BlockSpec((1, 1, bn), lambda l, j: (l, 0, j))],
        out_specs=pl.BlockSpec((1, 16, bn), lambda l, j: (l, 0, j)),
        out_shape=jax.ShapeDtypeStruct((depth, 16, 6 * d), F32),
        compiler_params=_params("parallel", "parallel"),
        name="modulation",
    )(rows, mod_w, mod_b.reshape(depth, 1, 6 * d))
    m_lat = out[:, :bsz].reshape(depth, bsz, 1, 6, d)
    m_ctx = jnp.broadcast_to(out[:, bsz].reshape(depth, 1, 1, 6, d), m_lat.shape)
    return jnp.concatenate([m_ctx, m_lat], axis=2).reshape(depth, 2 * bsz, 6, d)


def _residual_tile(hc_ref, hx_ref, n_ctx_tiles):
    return jnp.where(pl.program_id(1) < n_ctx_tiles, hc_ref[0], hx_ref[0])


def _residual_specs(d, n_ctx_tiles):
    return [pl.BlockSpec((1, TOK_TILE, d), lambda b, i: (b, jnp.minimum(i, n_ctx_tiles - 1), 0)),
            pl.BlockSpec((1, TOK_TILE, d), lambda b, i: (b, jnp.maximum(i - n_ctx_tiles, 0), 0))]


def _hg_proj_kernel(hc_ref, hx_ref, m_ref, g_ref, w_ref, lb_ref,
                    qs_ref, v_ref, sg_ref, kf_ref, kb_ref, lff_ref, lfb_ref, *, n_ctx_tiles):
    d = hx_ref.shape[-1]
    m = m_ref[0]
    a = _rms(_residual_tile(hc_ref, hx_ref, n_ctx_tiles), g_ref[...]) * (1.0 + m[1:2]) + m[0:1]
    p = jnp.dot(a.astype(BF16), w_ref[...], preferred_element_type=F32)
    qs_ref[0] = _silu(p[:, 0:d]).astype(BF16)
    v_ref[0] = p[:, d:2 * d].astype(BF16)
    sg_ref[0] = _silu(p[:, 4 * d:5 * d]).astype(BF16)
    for di, (k_ref, lf_ref) in enumerate(((kf_ref, lff_ref), (kb_ref, lfb_ref))):
        lb = lb_ref[di:di + 1]
        f = lb + (1.0 - lb) * _sigmoid(p[:, (2 + di) * d:(3 + di) * d])
        k_ref[0] = (1.0 - f).astype(BF16)
        lf_ref[0] = jnp.log(f)


def _mod_spec(d, n_ctx_tiles):
    return pl.BlockSpec((1, 6, d), lambda b, i: (2 * b + (i >= n_ctx_tiles).astype(jnp.int32), 0, 0))


def _hg_proj(ctx, x, mod, g1, w_in, lb, n_ctx_tiles):
    bsz, t, d = x.shape
    t += ctx.shape[1]
    tok = pl.BlockSpec((1, TOK_TILE, d), lambda b, i: (b, i, 0))
    bf = jax.ShapeDtypeStruct((bsz, t, d), BF16)
    ff = jax.ShapeDtypeStruct((bsz, t, d), F32)
    return pl.pallas_call(
        functools.partial(_hg_proj_kernel, n_ctx_tiles=n_ctx_tiles),
        grid=(bsz, t // TOK_TILE),
        in_specs=_residual_specs(d, n_ctx_tiles) + [_mod_spec(d, n_ctx_tiles),
                  pl.BlockSpec((1, d), lambda b, i: (0, 0)),
                  pl.BlockSpec((d, 5 * d), lambda b, i: (0, 0)),
                  pl.BlockSpec((2, d), lambda b, i: (0, 0))],
        out_specs=[tok] * 7,
        out_shape=[bf, bf, bf, bf, bf, ff, ff],
        compiler_params=_params("parallel", "parallel"),
        name="hg_proj",
    )(ctx, x, mod, g1.reshape(1, d), w_in.astype(BF16), lb)


def _split3(x):
    hi = x.astype(BF16)
    r = x - hi.astype(F32)
    mid = r.astype(BF16)
    lo = (r - mid.astype(F32)).astype(BF16)
    return hi, mid, lo


def _gla_kernel(qf_ref, vf_ref, kf_ref, lf_ref, qb_ref, vb_ref, kb_ref, lb_ref,
                of_ref, ob_ref, sf_ref, sb_ref):
    c = HG_CHUNK
    n_b = qf_ref.shape[0]
    n_heads = sf_ref.shape[0] // n_b

    @pl.when(pl.program_id(1) == 0)
    def _():
        sf_ref[...] = jnp.zeros_like(sf_ref)
        sb_ref[...] = jnp.zeros_like(sb_ref)

    row = lax.broadcasted_iota(jnp.int32, (c, c), 0)
    col = lax.broadcasted_iota(jnp.int32, (c, c), 1)
    dirs = ((qf_ref, vf_ref, kf_ref, lf_ref, of_ref, sf_ref, col <= row, c // 2 - 1, c - 1),
            (qb_ref, vb_ref, kb_ref, lb_ref, ob_ref, sb_ref, col >= row, c // 2, 0))
    for bb, (q_ref, v_ref, k_ref, l_ref, o_ref, s_ref, keep, r_ref, r_last) in (
            (bb, dr) for bb in range(n_b) for dr in dirs):
        tri = keep.astype(BF16)
        b = sum(jnp.dot(tri, part, preferred_element_type=F32) for part in _split3(l_ref[bb]))
        b_ref = b[r_ref:r_ref + 1]
        b_last = b[r_last:r_last + 1]
        a_in = q_ref[bb].astype(F32) * jnp.exp(b - b_ref)
        k_in = k_ref[bb].astype(F32) * jnp.exp(b_ref - b)
        q_st = (a_in * jnp.exp(b_ref)).astype(BF16)
        k_st = (k_in * jnp.exp(b_last - b_ref)).astype(BF16)
        a_in = a_in.astype(BF16)
        k_in = k_in.astype(BF16)
        decay = jnp.exp(b_last)
        v = v_ref[bb]
        for h in range(n_heads):
            sl = slice(h * HEAD_DIM, (h + 1) * HEAD_DIM)
            sc = lax.dot_general(a_in[:, sl], k_in[:, sl], (((1,), (1,)), ((), ())),
                                 preferred_element_type=F32)
            sc = jnp.where(keep, sc, 0.0).astype(BF16)
            o = jnp.dot(sc, v[:, sl], preferred_element_type=F32)
            st = s_ref[bb * n_heads + h]
            o = o + lax.dot_general(q_st[:, sl], st.astype(BF16), (((1,), (1,)), ((), ())),
                                    preferred_element_type=F32)
            s_ref[bb * n_heads + h] = st * decay[:, sl] + lax.dot_general(
                v[:, sl], k_st[:, sl], (((0,), (0,)), ((), ())), preferred_element_type=F32)
            o_ref[bb, :, sl] = o


def _gla(qs, v, kf, kb, lff, lfb, n_ctx_chunks):
    bsz, t, d = qs.shape
    n = t // HG_CHUNK
    n_heads = d // HEAD_DIM

    def rev(j):
        return jnp.where(j < n_ctx_chunks, n_ctx_chunks - 1 - j, n + n_ctx_chunks - 1 - j)

    fwd = pl.BlockSpec((HG_BATCH, HG_CHUNK, d), lambda b, j: (b, j, 0))
    bwd = pl.BlockSpec((HG_BATCH, HG_CHUNK, d), lambda b, j: (b, rev(j), 0))
    out = jax.ShapeDtypeStruct((bsz, t, d), F32)
    state = pltpu.VMEM((HG_BATCH * n_heads, HEAD_DIM, HEAD_DIM), F32)
    return pl.pallas_call(
        _gla_kernel,
        grid=(bsz // HG_BATCH, n),
        in_specs=[fwd, fwd, fwd, fwd, bwd, bwd, bwd, bwd],
        out_specs=[fwd, bwd],
        out_shape=[out, out],
        scratch_shapes=[state, state],
        compiler_params=_params("parallel", "arbitrary"),
        name="gla",
    )(qs, v, kf, lff, qs, v, kb, lfb)


def _mixer_tail(first, h, y, m, g2, rw, rb, run_ref, h1_ref, f_ref, tw_ref, meta_ref, cnt_ref):
    h1 = h + m[2:3] * y
    f = _rms(h1, g2) * (1.0 + m[4:5]) + m[3:4]
    h1_ref[...] = h1.reshape(h1_ref.shape)
    _to_token_tiles(f_ref, f)
    f_hi, f_lo, _ = _split3(f)
    w_hi, w_lo, _ = _split3(rw)
    logits = (jnp.dot(f_hi, w_hi, preferred_element_type=F32) + jnp.dot(f_hi, w_lo, preferred_element_type=F32)
              + jnp.dot(f_lo, w_hi, preferred_element_type=F32)) + rb
    n, n_e = logits.shape
    lane = lax.broadcasted_iota(jnp.int32, logits.shape, 1).astype(F32)
    vals, idxs, hots = [], [], []
    for _ in range(TOP_K):
        mx = jnp.max(logits, axis=-1, keepdims=True)
        ix = jnp.min(jnp.where(logits == mx, lane, float(n_e)), axis=-1, keepdims=True)
        hot = lane == ix
        vals.append(mx)
        idxs.append(ix)
        hots.append(hot)
        logits = jnp.where(hot, -jnp.inf, logits)
    es = [jnp.exp(x - vals[0]) for x in vals]
    tot = sum(es)
    for k in range(TOP_K):
        tw_ref[:, k:k + 1] = es[k] / tot

    @pl.when(first)
    def _():
        run_ref[...] = jnp.zeros_like(run_ref)

    picked = sum(hot.astype(F32) for hot in hots)
    r_i = lax.broadcasted_iota(jnp.int32, (n, n), 0)
    c_i = lax.broadcasted_iota(jnp.int32, (n, n), 1)
    earlier = jnp.dot((c_i < r_i).astype(BF16), picked.astype(BF16), preferred_element_type=F32)
    rank = earlier + run_ref[...]
    total = run_ref[...] + jnp.sum(picked, axis=0, keepdims=True)
    run_ref[...] = total
    cnt_ref[...] = total
    col = lax.broadcasted_iota(jnp.int32, (n, LANES), 1)
    z = jnp.zeros((n, LANES), F32)
    for k in range(TOP_K):
        pos = jnp.sum(jnp.where(hots[k], rank, 0.0), axis=-1, keepdims=True)
        z = jnp.where(col == k, idxs[k], z)
        z = jnp.where(col == TOP_K + k, pos, z)
    meta_ref[0] = z.T[0:2 * TOP_K].astype(jnp.int32)


def _tail_outputs(n_tiles, tile, d, n_e, index):
    rt = d // LANES
    specs = [pl.BlockSpec((tile * rt, LANES), lambda b, i: (index(b, i), 0)),
             pl.BlockSpec((1, tile, TOP_K), lambda b, i: (index(b, i), 0, 0)),
             pl.BlockSpec((1, 2 * TOP_K, tile), lambda b, i: (index(b, i), 0, 0)),
             pl.BlockSpec((1, n_e), lambda b, i: (0, 0))]
    shapes = [jax.ShapeDtypeStruct((n_tiles * tile * rt, LANES), F32),
              jax.ShapeDtypeStruct((n_tiles, tile, TOP_K), F32),
              jax.ShapeDtypeStruct((n_tiles, 2 * TOP_K, tile), jnp.int32),
              jax.ShapeDtypeStruct((1, n_e), F32)]
    return specs, shapes


def _hg_readout_kernel(of_ref, ob_ref, sg_ref, hc_ref, hx_ref, m_ref, ng_ref, w_ref, g2_ref, rw_ref, rb_ref,
                       h1_ref, f_ref, tw_ref, meta_ref, cnt_ref, run_ref, *, n_ctx_tiles):
    d = hx_ref.shape[-1]
    o = of_ref[0] + ob_ref[0]
    parts = []
    for h in range(d // HEAD_DIM):
        oh = o[:, h * HEAD_DIM:(h + 1) * HEAD_DIM]
        parts.append(oh * lax.rsqrt(jnp.mean(oh * oh, axis=-1, keepdims=True) + RMS_EPS))
    o = jnp.concatenate(parts, axis=-1) * ng_ref[...]
    y = jnp.dot((o * sg_ref[0].astype(F32)).astype(BF16), w_ref[...], preferred_element_type=F32)
    first = (pl.program_id(0) == 0) & (pl.program_id(1) == 0)
    _mixer_tail(first, _residual_tile(hc_ref, hx_ref, n_ctx_tiles), y, m_ref[0], g2_ref[...], rw_ref[...],
                rb_ref[...], run_ref, h1_ref, f_ref, tw_ref.at[0], meta_ref, cnt_ref)


def _hg_readout(o_f, o_b, sg, ctx, x, mod, norm_g, w_out, g2, rw, rb, n_ctx_tiles):
    bsz, t, d = o_f.shape
    n_e = rw.shape[-1]
    nt = t // TOK_TILE
    tok = pl.BlockSpec((1, TOK_TILE, d), lambda b, i: (b, i, 0))
    const = lambda shape: pl.BlockSpec(shape, lambda b, i: (0,) * len(shape))
    tail_specs, tail_shapes = _tail_outputs(bsz * nt, TOK_TILE, d, n_e, lambda b, i: b * nt + i)
    return pl.pallas_call(
        functools.partial(_hg_readout_kernel, n_ctx_tiles=n_ctx_tiles),
        grid=(bsz, nt),
        in_specs=[tok, tok, tok] + _residual_specs(d, n_ctx_tiles) + [_mod_spec(d, n_ctx_tiles),
                  const((1, d)), const((d, d)), const((1, d)), const((d, n_e)), const((1, n_e))],
        out_specs=[tok] + tail_specs,
        out_shape=[jax.ShapeDtypeStruct((bsz, t, d), F32)] + tail_shapes,
        scratch_shapes=[pltpu.VMEM((1, n_e), F32)],
        compiler_params=_params("arbitrary", "arbitrary"),
        name="hg_readout",
    )(o_f, o_b, sg, ctx, x, mod, norm_g.reshape(1, d), w_out.astype(BF16), g2.reshape(1, d),
      rw, rb.reshape(1, n_e))


def _moe_plan(counts, n_asg, rows):
    counts = counts.reshape(-1).astype(jnp.int32)
    padded = (counts + rows - 1) // rows * rows
    pad_end = jnp.cumsum(padded)
    n_blocks = -(-n_asg // rows) + N_EXPERTS
    block_pos = jnp.arange(n_blocks, dtype=jnp.int32) * rows
    block_e = jnp.minimum(jnp.sum((block_pos[:, None] >= pad_end[None, :]).astype(jnp.int32), axis=1),
                          N_EXPERTS - 1)
    n_used = (pad_end[-1:] // rows).astype(jnp.int32)
    return pad_end - padded, block_e, n_used, n_blocks


def _invert_kernel(dest_ref, init_hbm, inv_ref, sem):
    i = pl.program_id(0)
    tile = dest_ref.shape[2]

    @pl.when(i == 0)
    def _():
        cp = pltpu.make_async_copy(init_hbm, inv_ref, sem)
        cp.start()
        cp.wait()

    base = i * (tile * TOP_K)
    for r in range(tile):
        for k in range(TOP_K):
            inv_ref[dest_ref[0, k, r]] = base + (r * TOP_K + k)


def _invert(meta, slot_start, n_slots):
    n_tiles, _, tile = meta.shape
    hot = meta[:, :TOP_K, :, None] == jnp.arange(N_EXPERTS, dtype=jnp.int32)
    dest = meta[:, TOP_K:] + jnp.sum(jnp.where(hot, slot_start, 0), axis=-1)
    return pl.pallas_call(
        _invert_kernel,
        grid=(n_tiles,),
        in_specs=[pl.BlockSpec((1, TOP_K, tile), lambda i: (i, 0, 0), memory_space=pltpu.SMEM),
                  pl.BlockSpec(memory_space=pl.ANY)],
        out_specs=pl.BlockSpec(memory_space=pltpu.SMEM),
        out_shape=jax.ShapeDtypeStruct((n_slots,), jnp.int32),
        scratch_shapes=[pltpu.SemaphoreType.DMA(())],
        compiler_params=_params("arbitrary"),
        name="moe_invert",
    )(dest.astype(jnp.int32), jnp.full((n_slots,), -1, jnp.int32))


def _ffn_kernel(be_ref, nu_ref, tok_ref, tok_next_ref, dst_prev_ref, dst_ref, x_hbm,
                w1_ref, b1_ref, w2_ref, b2_ref, y_hbm, buf_p, buf_q, zbuf, w1c, w2c, gsem, ssem, zsem):
    xbuf = (buf_p.at[0], buf_q.at[0])
    ybuf = (buf_q.at[1], buf_p.at[1])
    b = pl.program_id(0)
    nu = nu_ref[0]
    d, f2 = w1c.shape
    rt = d // LANES
    rows = buf_p.shape[1] // rt

    def tile_of(ref, idx):
        return ref.at[pl.ds(pl.multiple_of(idx * rt, rt), rt)]

    def gather(idx_ref, s):
        for r in range(rows):
            pltpu.make_async_copy(tile_of(x_hbm, idx_ref[0, 0, r]), xbuf[s].at[pl.ds(r * rt, rt)],
                                  gsem.at[s]).start(priority=r % 2)

    def scatter(idx_ref, s):
        for r in range(rows):
            pltpu.make_async_copy(ybuf[s].at[pl.ds(r * rt, rt)], tile_of(y_hbm, idx_ref[0, 0, r]),
                                  ssem.at[s]).start(priority=r % 2)

    def wait_gather(s):
        pltpu.make_async_copy(x_hbm.at[pl.ds(0, rows * rt)], xbuf[s], gsem.at[s]).wait()

    def wait_scatter(s):
        pltpu.make_async_copy(ybuf[s], y_hbm.at[pl.ds(0, rows * rt)], ssem.at[s]).wait()

    @pl.when(b == 0)
    def _():
        zbuf[...] = jnp.zeros_like(zbuf)
        buf_p[1] = jnp.zeros(buf_p.shape[1:], F32)
        gather(tok_ref, 0)

    @pl.when(b >= nu)
    def _():
        dst = y_hbm.at[pl.ds(pl.multiple_of(b * (rows * rt), rows * rt), rows * rt)]
        cp = pltpu.make_async_copy(zbuf, dst, zsem)
        cp.start()
        cp.wait()

    def used_block(s):
        @pl.when((b == 0) | (be_ref[b] != be_ref[jnp.maximum(b - 1, 0)]))
        def _():
            w1c[...] = w1_ref[0, 0].astype(BF16)
            w2c[...] = w2_ref[0, 0].astype(BF16)

        wait_gather(s)

        @pl.when(b >= 1)
        def _():
            wait_scatter(s)

        x_head = xbuf[s][0:rt]
        x = _from_token_tiles(xbuf[s], rows, d).astype(BF16)
        gather(tok_next_ref, 1 - s)
        scatter(dst_prev_ref, 1 - s)
        z = jnp.dot(x, w1c[...], preferred_element_type=F32) + b1_ref[0, 0]
        z_glu = jnp.minimum(z[:, :f2 // 2], SWIGLU_LIMIT)
        z_lin = jnp.clip(z[:, f2 // 2:], -SWIGLU_LIMIT, SWIGLU_LIMIT)
        act = z_glu * _sigmoid(SWIGLU_ALPHA * z_glu) * (z_lin + 1.0)
        y = jnp.dot(act.astype(BF16), w2c[...], preferred_element_type=F32) + b2_ref[0, 0]
        _to_token_tiles(ybuf[s], y)
        xbuf[s][0:rt] = x_head

        @pl.when(b == nu - 1)
        def _():
            scatter(dst_ref, s)
            wait_gather(1 - s)
            wait_scatter(1 - s)
            wait_scatter(s)

    for s in range(2):
        pl.when((b < nu) & (b % 2 == s))(functools.partial(used_block, s))


def _moe(f_tiles, meta, counts, layer, w1, b1, w2, b2):
    n_tiles, _, tile = meta.shape
    n_asg = n_tiles * tile * TOP_K
    _, n_e, d, f2 = w1.shape
    rt = d // LANES
    slot_start, block_e, n_used, n_blocks = _moe_plan(counts, n_asg, MOE_ROWS)
    n_slots = n_blocks * MOE_ROWS
    inv = _invert(meta, slot_start, n_slots)
    is_pad = inv < 0
    slot_tok = jnp.where(is_pad, 0, inv // TOP_K).reshape(n_blocks, 1, MOE_ROWS)
    spill = n_asg + jnp.cumsum(is_pad.astype(jnp.int32)) - 1
    slot_dst = jnp.where(is_pad, spill, inv)
    slot_dst = jnp.concatenate([slot_dst, jnp.arange(n_slots - MOE_ROWS, n_slots, dtype=jnp.int32)])
    slot_dst = slot_dst.reshape(n_blocks + 1, 1, MOE_ROWS)

    live = lambda b, nu: jnp.minimum(b, nu[0] - 1)
    smem = lambda imap: pl.BlockSpec((1, 1, MOE_ROWS), imap, memory_space=pltpu.SMEM)
    per_e = lambda shape: pl.BlockSpec((1, 1) + shape, lambda b, be, nu: (layer, be[live(b, nu)], 0, 0))
    buf = pltpu.VMEM((MOE_ROWS * rt, LANES), F32)
    bufs = pltpu.VMEM((2, MOE_ROWS * rt, LANES), F32)
    grid_spec = pltpu.PrefetchScalarGridSpec(
        num_scalar_prefetch=2,
        grid=(n_blocks,),
        in_specs=[smem(lambda b, be, nu: (live(b, nu), 0, 0)),
                  smem(lambda b, be, nu: (live(b + 1, nu), 0, 0)),
                  smem(lambda b, be, nu: (jnp.where(b == 0, n_blocks, live(b - 1, nu)), 0, 0)),
                  smem(lambda b, be, nu: (live(b, nu), 0, 0)),
                  pl.BlockSpec(memory_space=pl.ANY),
                  per_e((d, f2)), per_e((1, f2)), per_e((f2 // 2, d)), per_e((1, d))],
        out_specs=pl.BlockSpec(memory_space=pl.ANY),
        scratch_shapes=[bufs, bufs, buf,
                        pltpu.VMEM((d, f2), BF16), pltpu.VMEM((f2 // 2, d), BF16),
                        pltpu.SemaphoreType.DMA((2,)), pltpu.SemaphoreType.DMA((2,)),
                        pltpu.SemaphoreType.DMA(())])
    return pl.pallas_call(
        _ffn_kernel,
        grid_spec=grid_spec,
        out_shape=jax.ShapeDtypeStruct((n_slots * rt, LANES), F32),
        compiler_params=_params("arbitrary"),
        name="moe_ffn",
    )(block_e, n_used, slot_tok, slot_tok, slot_dst, slot_dst, f_tiles, w1, b1.reshape(b1.shape[0], n_e, 1, f2),
      w2, b2.reshape(b2.shape[0], n_e, 1, d))


def _combine_experts(y_ref, tw_ref, tile, d):
    rt = d // LANES
    tw = tw_ref[0]
    out = 0.0
    for k in range(TOP_K):
        yk = jnp.concatenate([y_ref[pl.ds(k * rt + c, tile, stride=TOP_K * rt), :] for c in range(rt)],
                             axis=-1)
        out = out + tw[:, k:k + 1] * yk
    return out


def _combine0_kernel(y_ref, tw_ref, h_ref, m0_ref, m1_ref, g_ref, hl_ref, al_ref, ac_ref, *, n_ctx_tiles):
    tile, d = h_ref.shape[1:]
    h2 = h_ref[0] + m0_ref[0][5:6] * _combine_experts(y_ref, tw_ref, tile, d)
    m1 = m1_ref[0]
    a = _rms(h2, g_ref[...]) * (1.0 + m1[1:2]) + m1[0:1]
    is_ctx = pl.program_id(1) < n_ctx_tiles

    @pl.when(is_ctx)
    def _():
        ac_ref[0] = a

    @pl.when(jnp.logical_not(is_ctx))
    def _():
        hl_ref[0] = h2
        al_ref[0] = a


def _combine0(y_asg, top_w, h1, mod0, mod1, g1_next, n_ctx_tiles):
    bsz, t, d = h1.shape
    nt = t // TOK_TILE
    n_ctx = n_ctx_tiles * TOK_TILE
    tok = pl.BlockSpec((1, TOK_TILE, d), lambda b, i: (b, i, 0))
    lat = pl.BlockSpec((1, TOK_TILE, d), lambda b, i: (b, jnp.maximum(i - n_ctx_tiles, 0), 0))
    ctx = pl.BlockSpec((1, TOK_TILE, d), lambda b, i: (b, jnp.minimum(i, n_ctx_tiles - 1), 0))
    return pl.pallas_call(
        functools.partial(_combine0_kernel, n_ctx_tiles=n_ctx_tiles),
        grid=(bsz, nt),
        in_specs=[pl.BlockSpec((TOK_TILE * TOP_K * d // LANES, LANES), lambda b, i: (b * nt + i, 0)),
                  pl.BlockSpec((1, TOK_TILE, TOP_K), lambda b, i: (b * nt + i, 0, 0)),
                  tok, _mod_spec(d, n_ctx_tiles), _mod_spec(d, n_ctx_tiles),
                  pl.BlockSpec((1, d), lambda b, i: (0, 0))],
        out_specs=[lat, lat, ctx],
        out_shape=[jax.ShapeDtypeStruct((bsz, t - n_ctx, d), F32),
                   jax.ShapeDtypeStruct((bsz, t - n_ctx, d), F32),
                   jax.ShapeDtypeStruct((bsz, n_ctx, d), F32)],
        compiler_params=_params("parallel", "parallel"),
        name="combine0",
    )(y_asg, top_w, h1, mod0, mod1, g1_next.reshape(1, d))


def _s5_matrices(a_re, a_im, log_dt, b_re, b_im, c_re, c_im):
    ng, p = a_re.shape[1:]
    gc = b_re.shape[-1]
    ll = S5_CHUNK
    gpb = LANES // gc
    nblk = ng // gpb
    lam_re = jnp.minimum(a_re, S5_EIG_MAX)
    lam_im = a_im
    dt = jnp.exp(log_dt)[..., None]
    j = jnp.arange(ll + 1, dtype=F32).reshape(-1, 1, 1, 1)
    mag = jnp.exp(j * (lam_re * dt))
    pw_re = mag * jnp.cos(j * (lam_im * dt))
    pw_im = mag * jnp.sin(j * (lam_im * dt))
    ab_re, ab_im = pw_re[1], pw_im[1]
    den = lam_re * lam_re + lam_im * lam_im
    coef_re = ((ab_re - 1.0) * lam_re + ab_im * lam_im) / den
    coef_im = (ab_im * lam_re - (ab_re - 1.0) * lam_im) / den
    bb_re = coef_re[..., None] * b_re - coef_im[..., None] * b_im
    bb_im = coef_re[..., None] * b_im + coef_im[..., None] * b_re
    drv_re = pw_re[..., None] * bb_re - pw_im[..., None] * bb_im
    drv_im = pw_re[..., None] * bb_im + pw_im[..., None] * bb_re
    rd_re = c_re * pw_re[:, :, :, None, :] - c_im * pw_im[:, :, :, None, :]
    rd_im = -(c_re * pw_im[:, :, :, None, :] + c_im * pw_re[:, :, :, None, :])
    taps = (jnp.einsum('dgop,jdgpi->jdgoi', c_re, drv_re[:ll], precision=HIGHEST)
            - jnp.einsum('dgop,jdgpi->jdgoi', c_im, drv_im[:ll], precision=HIGHEST))
    r = jnp.arange(ll)
    c_in, c_drv, c_rd = [], [], []
    for d in range(2):
        lag = (r[None, :] - r[:, None]) if d == 0 else (r[:, None] - r[None, :])
        tp = jnp.where((lag >= 0)[:, :, None, None, None],
                       taps[:, d][jnp.clip(lag, 0, ll - 1)], 0.0)
        tp = tp.reshape(ll, ll, nblk, gpb, gc, gc).transpose(2, 0, 3, 5, 1, 4)
        c_in.append(tp.reshape(nblk, ll * LANES, ll * gc))
        steps = (ll - 1 - r) if d == 0 else r
        dr = jnp.stack([drv_re[:, d][steps], drv_im[:, d][steps]], axis=1)
        dr = dr.reshape(ll, 2, nblk, gpb, p, gc).transpose(2, 0, 3, 5, 1, 4)
        c_drv.append(dr.reshape(nblk, ll * LANES, 2 * p))
        steps = (r + 1) if d == 0 else (ll - r)
        rd = jnp.stack([rd_re[:, d][steps], rd_im[:, d][steps]], axis=1)
        rd = rd.reshape(ll, 2, nblk, gpb, gc, p).transpose(2, 1, 3, 5, 0, 4)
        c_rd.append(rd.reshape(nblk, 2 * gpb * p, ll * gc))

    def expand(compact, row_unit, col_unit):
        compact = jnp.stack(compact).astype(BF16)
        n_r, n_c = compact.shape[2], compact.shape[3] * gpb
        col = jnp.arange(n_c)
        src = (col // (col_unit * gpb)) * col_unit + col % col_unit
        spread = (jnp.arange(n_c // gpb)[:, None] == src[None, :]).astype(BF16)
        full = jnp.einsum('dbrk,kc->dbrc', compact, spread, preferred_element_type=F32)
        same = ((jnp.arange(n_r) // row_unit) % gpb)[:, None] == ((col // col_unit) % gpb)[None, :]
        return jnp.where(same, full, 0.0).astype(BF16)

    a8 = jnp.stack([pw_re[ll], pw_im[ll]], axis=1).reshape(2, 2, nblk, gpb * p).transpose(0, 2, 1, 3)
    return expand([c_in[0] + c_in[1]], gc, gc)[0], expand(c_drv, gc, p), expand(c_rd, p, gc), a8


def _s5_kernel(uc_ref, ul_ref, min_ref, mdrv_ref, mrd_ref, a8_ref, y_ref, v_ref, st_ref, *, n_ctx_cols):
    bsz, n_oct, ll, n_col, lanes = ul_ref.shape
    n_rows = bsz * n_oct * n_col
    half = st_ref.shape[-1] // 2
    d = pl.program_id(0)
    k = pl.program_id(2)

    @pl.when(k == 0)
    def _():
        st_ref[...] = jnp.zeros_like(st_ref)

    def chunks(u_ref):
        return jnp.concatenate([u_ref[:, :, r].reshape(n_rows, lanes) for r in range(ll)], axis=-1)

    x = jnp.where(k == 0, chunks(uc_ref), chunks(ul_ref)).astype(BF16)
    inj_all = jnp.dot(x, mdrv_ref[0, 0], preferred_element_type=F32)
    n_pl = v_ref.shape[0]
    for c in range(n_pl):
        v_ref[c] = inj_all[:, c * lanes:(c + 1) * lanes]

    a_re = jnp.broadcast_to(a8_ref[0, 0, 0:1], (bsz, half))
    a_im = jnp.broadcast_to(a8_ref[0, 0, 1:2], (bsz, half))
    def scan(n_steps):
        def step(i, carry):
            s_re, s_im = carry
            i = jnp.where(d == 0, i, n_steps - 1 - i)
            row = (i % n_oct) * n_col + i // n_oct
            rows = pl.ds(row, bsz, stride=n_oct * n_col)
            inj = jnp.concatenate([v_ref[c, rows, :] for c in range(n_pl)], axis=-1)
            for c in range(n_pl // 2):
                v_ref[c, rows, :] = s_re[:, c * lanes:(c + 1) * lanes]
                v_ref[n_pl // 2 + c, rows, :] = s_im[:, c * lanes:(c + 1) * lanes]
            return (a_re * s_re - a_im * s_im + inj[:, :half],
                    a_re * s_im + a_im * s_re + inj[:, half:])

        s_re, s_im = lax.fori_loop(0, n_steps, step, (st_ref[:, :half], st_ref[:, half:]), unroll=4)
        st_ref[:, :half] = s_re
        st_ref[:, half:] = s_im

    @pl.when(k == 0)
    def _():
        scan(n_ctx_cols * n_oct)

    @pl.when(k > 0)
    def _():
        scan(n_col * n_oct)

    s_start = jnp.concatenate([v_ref[c] for c in range(n_pl)], axis=-1).astype(BF16)
    y = jnp.dot(s_start, mrd_ref[0, 0], preferred_element_type=F32)

    def emit(y):
        for r in range(ll):
            y_ref[0, :, :, r] = y[:, r * lanes:(r + 1) * lanes].reshape(bsz, n_oct, n_col, lanes)

    @pl.when(d == 0)
    def _():
        emit(y + jnp.dot(x, min_ref[0], preferred_element_type=F32))

    @pl.when(d != 0)
    def _():
        emit(y)


def _s5_scan(a_lat, a_ctx, mats):
    bsz, t, d = a_lat.shape
    n_ctx = a_ctx.shape[1]
    m_in, m_drv, m_rd, a8 = mats
    n_rows = t // GRID_W
    n_oct = n_rows // S5_CHUNK
    n_ctx_cols = n_ctx // n_rows
    u_lat = a_lat.reshape(bsz, n_oct, S5_CHUNK, GRID_W, d)
    u_ctx = a_ctx.reshape(bsz, n_ctx_cols, n_rows, d).transpose(0, 2, 1, 3)
    u_ctx = jnp.pad(u_ctx, ((0, 0), (0, 0), (0, S5_COLS - n_ctx_cols), (0, 0)))
    u_ctx = u_ctx.reshape(bsz, n_oct, S5_CHUNK, S5_COLS, d)
    n_lat = GRID_W // S5_COLS
    nblk = d // LANES

    def lat_tile(dd, k):
        kk = jnp.maximum(k, 1) - 1
        return jnp.where(dd == 0, kk, n_lat - 1 - kk)

    blk = (bsz, n_oct, S5_CHUNK, S5_COLS, LANES)
    wspec = lambda shape: pl.BlockSpec((1, 1) + shape, lambda dd, j, k: (dd, j, 0, 0))
    kl = S5_CHUNK * LANES
    ns = m_drv.shape[-1]
    y = pl.pallas_call(
        functools.partial(_s5_kernel, n_ctx_cols=n_ctx_cols),
        grid=(2, nblk, n_lat + 1),
        in_specs=[pl.BlockSpec(blk, lambda dd, j, k: (0, 0, 0, 0, j)),
                  pl.BlockSpec(blk, lambda dd, j, k: (0, 0, 0, lat_tile(dd, k), j)),
                  pl.BlockSpec((1, kl, kl), lambda dd, j, k: (j, 0, 0)), wspec((kl, ns)), wspec((ns, kl)),
                  pl.BlockSpec((1, 1, 2, ns // 2), lambda dd, j, k: (dd, j, 0, 0))],
        out_specs=pl.BlockSpec((1,) + blk, lambda dd, j, k: (dd, 0, 0, 0, lat_tile(dd, k), j)),
        out_shape=jax.ShapeDtypeStruct((2,) + u_lat.shape, F32),
        scratch_shapes=[pltpu.VMEM((ns // LANES, bsz * n_oct * S5_COLS, LANES), F32),
                        pltpu.VMEM((bsz, ns), F32)],
        compiler_params=_params("parallel", "parallel", "arbitrary"),
        name="s5_scan",
    )(u_ctx, u_lat, m_in, m_drv, m_rd, a8)
    return y.reshape(2, bsz, n_rows, GRID_W, d)


def _s5_glu_kernel(y_ref, u_ref, h_ref, m_ref, dsk_ref, w_ref, bg_ref, g2_ref, rw_ref, rb_ref,
                   h1_ref, f_ref, tw_ref, meta_ref, cnt_ref, run_ref):
    d = h_ref.shape[-1]
    n = h_ref.shape[1] * h_ref.shape[2]
    y = (y_ref[0, 0] + y_ref[1, 0] + dsk_ref[...] * u_ref[0]).reshape(n, d)
    z = jnp.dot(jax.nn.gelu(y).astype(BF16), w_ref[...], preferred_element_type=F32) + bg_ref[...]
    y = z[:, :d] * _sigmoid(z[:, d:])
    first = (pl.program_id(0) == 0) & (pl.program_id(1) == 0)
    _mixer_tail(first, h_ref[0].reshape(n, d), y, m_ref[0], g2_ref[...], rw_ref[...], rb_ref[...],
                run_ref, h1_ref, f_ref, tw_ref.at[0], meta_ref, cnt_ref)


def _s5_glu(y, a_lat, h_lat, mod, d_skip, w_glu, b_glu, g2, rw, rb):
    bsz, t, d = h_lat.shape
    n_rows = t // GRID_W
    n_e = rw.shape[-1]
    n_tok = n_rows * S5_COLS
    nt = GRID_W // S5_COLS
    lat = pl.BlockSpec((1, n_rows, S5_COLS, d), lambda b, i: (b, 0, i, 0))
    tail_specs, tail_shapes = _tail_outputs(bsz * nt, n_tok, d, n_e, lambda b, i: b * nt + i)
    const = lambda shape: pl.BlockSpec(shape, lambda b, i: (0,) * len(shape))
    return pl.pallas_call(
        _s5_glu_kernel,
        grid=(bsz, nt),
        in_specs=[pl.BlockSpec((2, 1, n_rows, S5_COLS, d), lambda b, i: (0, b, 0, i, 0)),
                  lat, lat, pl.BlockSpec((1, 6, d), lambda b, i: (2 * b + 1, 0, 0)),
                  const((1, d)), const((d, 2 * d)), const((1, 2 * d)), const((1, d)),
                  const((d, n_e)), const((1, n_e))],
        out_specs=[lat] + tail_specs,
        out_shape=[jax.ShapeDtypeStruct((bsz, n_rows, GRID_W, d), F32)] + tail_shapes,
        scratch_shapes=[pltpu.VMEM((1, n_e), F32)],
        compiler_params=_params("arbitrary", "arbitrary"),
        name="s5_glu",
    )(y, a_lat.reshape(bsz, n_rows, GRID_W, d), h_lat.reshape(bsz, n_rows, GRID_W, d), mod,
      d_skip.reshape(1, d), w_glu.astype(BF16), b_glu.reshape(1, 2 * d),
      g2.reshape(1, d), rw, rb.reshape(1, n_e))


def _combine1_kernel(y_ref, tw_ref, h_ref, m_ref, g_ref, o_ref):
    d = h_ref.shape[-1]
    n = h_ref.shape[1] * h_ref.shape[2]
    h2 = h_ref[0].reshape(n, d) + m_ref[0][5:6] * _combine_experts(y_ref, tw_ref, n, d)
    o_ref[0] = _rms(h2, g_ref[...]).reshape(o_ref.shape[1:])


def _combine1(y_asg, top_w, h1, mod, final_g):
    bsz, n_rows, _, d = h1.shape
    nt = GRID_W // S5_COLS
    n_tok = n_rows * S5_COLS
    lat = pl.BlockSpec((1, n_rows, S5_COLS, d), lambda b, i: (b, 0, i, 0))
    return pl.pallas_call(
        _combine1_kernel,
        grid=(bsz, nt),
        in_specs=[pl.BlockSpec((n_tok * TOP_K * d // LANES, LANES), lambda b, i: (b * nt + i, 0)),
                  pl.BlockSpec((1, n_tok, TOP_K), lambda b, i: (b * nt + i, 0, 0)),
                  lat, pl.BlockSpec((1, 6, d), lambda b, i: (2 * b + 1, 0, 0)),
                  pl.BlockSpec((1, d), lambda b, i: (0, 0))],
        out_specs=lat,
        out_shape=jax.ShapeDtypeStruct(h1.shape, F32),
        compiler_params=_params("parallel", "parallel"),
        name="combine1",
    )(y_asg, top_w, h1, mod, final_g.reshape(1, d))


def kernel(x, c, ctx, c_ctx, mod_w, mod_b, norm1_g, norm2_g, hg_w_in, hg_lb_logits, hg_norm_g,
           hg_w_out, s5_a_re, s5_a_im, s5_log_dt, s5_b_re, s5_b_im, s5_c_re, s5_c_im, s5_d, s5_w_glu,
           s5_b_glu, router_w, router_b, moe_w1, moe_b1, moe_w2, moe_b2, final_g):
    bsz, seq, d = x.shape
    n_ctx = ctx.shape[1]
    assert mod_w.shape[0] == 2 and n_ctx % TOK_TILE == 0 and seq % TOK_TILE == 0
    n_ctx_tiles = n_ctx // TOK_TILE
    lb_all = jnp.cumsum(jax.nn.softmax(hg_lb_logits.astype(F32), axis=0), axis=0)
    mod = _modulation(c, c_ctx, mod_w, mod_b)

    qs, v, sg, kf, kb, lff, lfb = _hg_proj(ctx, x, mod[0], norm1_g[0], hg_w_in[0], lb_all[0], n_ctx_tiles)
    o_f, o_b = _gla(qs, v, kf, kb, lff, lfb, n_ctx // HG_CHUNK)
    h1, f, top_w, meta, counts = _hg_readout(o_f, o_b, sg, ctx, x, mod[0], hg_norm_g[0], hg_w_out[0],
                                             norm2_g[0], router_w[0], router_b[0], n_ctx_tiles)
    y_asg = _moe(f, meta, counts, 0, moe_w1, moe_b1, moe_w2, moe_b2)
    h_lat, a_lat, a_ctx = _combine0(y_asg, top_w, h1, mod[0], mod[1], norm1_g[1], n_ctx_tiles)

    mats = _s5_matrices(s5_a_re[0], s5_a_im[0], s5_log_dt[0], s5_b_re[0], s5_b_im[0],
                        s5_c_re[0], s5_c_im[0])
    y_s5 = _s5_scan(a_lat, a_ctx, mats)
    h1, f, top_w, meta, counts = _s5_glu(y_s5, a_lat, h_lat, mod[1], s5_d[0], s5_w_glu[0], s5_b_glu[0],
                                         norm2_g[1], router_w[1], router_b[1])
    y_asg = _moe(f, meta, counts, 1, moe_w1, moe_b1, moe_w2, moe_b2)
    out = _combine1(y_asg, top_w, h1, mod[1], final_g)
    return out.reshape(bsz, seq, d)
```
